```python
import math
import jax, jax.numpy as jnp
from jax import lax
import numpy as np

D_MODEL = 1024
BATCH = 4
SEQ = 4096
DEPTH = 4
DEC_BATCH = 128
DEC_SEQ = 8
PAST_LEN = 8192
PAGE_SIZE = 128

CHUNK = 128
A_WIDTH = D_MODEL
A_GROUPS = 8
A_GDIM = A_WIDTH // A_GROUPS
N_HEADS = 16
HEAD_DIM = 64
N_KV_HEADS = 4
Q_PER_KV = N_HEADS // N_KV_HEADS
WINDOW = 128
N_BUCKETS = 32
MAX_DISTANCE = 128
D_FF = -(-8 * D_MODEL // (3 * 256)) * 256
DEEPNORM_ALPHA = (2 * DEPTH) ** 0.25
DEEPNORM_BETA = (8 * DEPTH) ** -0.25
LN_EPS = 1e-5
NEG_INF = -1e30

O_U = 0
O_V = O_U + A_WIDTH
O_Q = O_V + A_WIDTH
O_K = O_Q + N_HEADS * HEAD_DIM
O_VV = O_K + N_KV_HEADS * HEAD_DIM
O_G = O_VV + N_KV_HEADS * HEAD_DIM
IN_WIDTH = O_G + 2 * D_MODEL

kernel_name = "hybrid_gmlp_swa_sink_decoder_step"


def layer_norm(x, g, b):
    xf = x.astype(jnp.float32)
    mu = jnp.mean(xf, axis=-1, keepdims=True)
    var = jnp.mean(jnp.square(xf - mu), axis=-1, keepdims=True)
    return ((xf - mu) * lax.rsqrt(var + LN_EPS) * g.astype(jnp.float32) + b.astype(jnp.float32)).astype(x.dtype)


def rel_bucket(dist):
    n = jnp.maximum(dist, 0)
    max_exact = N_BUCKETS // 2
    nf = jnp.maximum(n, 1).astype(jnp.float32)
    large = max_exact + (jnp.log(nf / max_exact) / math.log(MAX_DISTANCE / max_exact)
                         * (N_BUCKETS - max_exact)).astype(jnp.int32)
    large = jnp.minimum(large, N_BUCKETS - 1)
    return jnp.where(n < max_exact, n, large)


def rel_bias_blocks(rel_bias, dist):
    b = rel_bias[rel_bucket(dist)].astype(jnp.float32)
    b = jnp.transpose(b, (2, 0, 1))
    return b.reshape(N_KV_HEADS, Q_PER_KV, dist.shape[0], dist.shape[1])


def sink_attention(q, k, v, bias, mask, sinks):
    s = jnp.einsum('bnqgrd,bnkgd->bngrqk', q, k).astype(jnp.float32) * (HEAD_DIM ** -0.5) + bias
    s = jnp.where(mask[None, :, None, None], s, NEG_INF)
    sink = sinks.astype(jnp.float32).reshape(N_KV_HEADS, Q_PER_KV)[None, None, :, :, None, None]
    m = jnp.maximum(jnp.max(s, axis=-1, keepdims=True), sink)
    p = jnp.exp(s - m)
    p = p / (jnp.sum(p, axis=-1, keepdims=True) + jnp.exp(sink - m))
    return jnp.einsum('bngrqk,bnkgd->bnqgrd', p.astype(v.dtype), v)


def causal_chunk_mask():
    return jnp.tril(jnp.ones((CHUNK, CHUNK), dtype=bool))


def sgu_prompt(u, v, w_s, b_s):
    b, s, _ = v.shape
    vv = v.reshape(b, s // CHUNK, CHUNK, A_GROUPS, A_GDIM)
    w = jnp.where(causal_chunk_mask()[None], w_s, 0.0).astype(v.dtype)
    mixed = jnp.einsum('gts,bcsgd->bctgd', w, vv) + b_s.T.astype(v.dtype)[None, None, :, :, None]
    return u * mixed.reshape(b, s, A_WIDTH)


def sgu_sample(u, v, w_s, b_s):
    b, s, _ = v.shape
    vv = v.reshape(b, s, A_GROUPS, A_GDIM)
    w = jnp.where(causal_chunk_mask()[None], w_s, 0.0)[:, :s, :s].astype(v.dtype)
    mixed = jnp.einsum('gts,bsgd->btgd', w, vv) + b_s[:, :s].T.astype(v.dtype)[None, :, :, None]
    return u * mixed.reshape(b, s, A_WIDTH)


def swa_prompt(q, k, v, bias, mask, sinks):
    b, s = q.shape[0], q.shape[1]
    nb = s // WINDOW
    pad = jnp.zeros((b, WINDOW, N_KV_HEADS, HEAD_DIM), k.dtype)
    kp = jnp.concatenate([pad, k], axis=1)[:, :s]
    vp = jnp.concatenate([pad, v], axis=1)[:, :s]
    blk = lambda t: t.reshape(b, nb, WINDOW, N_KV_HEADS, HEAD_DIM)
    kk = jnp.concatenate([blk(kp), blk(k)], axis=2)
    vv = jnp.concatenate([blk(vp), blk(v)], axis=2)
    qb = q.reshape(b, nb, WINDOW, N_KV_HEADS, Q_PER_KV, HEAD_DIM)
    o = sink_attention(qb, kk, vv, bias, mask, sinks)
    return o.reshape(b, s, N_HEADS * HEAD_DIM)


def swa_sample(q, k_all, v_all, bias, mask, sinks):
    o = sink_attention(q[:, None], k_all[:, None], v_all[:, None], bias, mask, sinks)
    return o[:, 0].reshape(q.shape[0], q.shape[1], N_HEADS * HEAD_DIM)


def trunk_layer(x, w_in, ln_v_g, ln_v_b, w_s, b_s, sinks, w_pa, w_pb, w_o,
                ln1_g, ln1_b, w_gate, w_up, w_down, ln2_g, ln2_b,
                bias, mask, k_buf, v_buf):
    b, s, _ = x.shape
    h = x @ w_in
    u = jax.nn.gelu(h[..., O_U:O_V])
    va = layer_norm(jax.nn.gelu(h[..., O_V:O_Q]), ln_v_g, ln_v_b)
    q = h[..., O_Q:O_K].reshape(b, s, N_KV_HEADS, Q_PER_KV, HEAD_DIM)
    k = h[..., O_K:O_VV].reshape(b, s, N_KV_HEADS, HEAD_DIM)
    v = h[..., O_VV:O_G].reshape(b, s, N_KV_HEADS, HEAD_DIM)
    g_a = jax.nn.sigmoid(h[..., O_G:O_G + D_MODEL])
    g_b = jax.nn.sigmoid(h[..., O_G + D_MODEL:])
    if k_buf is None:
        y_a = sgu_prompt(u, va, w_s, b_s)
        y_b = swa_prompt(q, k, v, bias, mask, sinks)
        new_k, new_v = k[:, -WINDOW:], v[:, -WINDOW:]
    else:
        y_a = sgu_sample(u, va, w_s, b_s)
        k_all = jnp.concatenate([k_buf, k], axis=1)
        v_all = jnp.concatenate([v_buf, v], axis=1)
        y_b = swa_sample(q, k_all, v_all, bias, mask, sinks)
        new_k, new_v = k_all[:, -WINDOW:], v_all[:, -WINDOW:]
    mix = (g_a * (y_a @ w_pa) + g_b * (y_b @ w_pb)) @ w_o
    x = layer_norm(DEEPNORM_ALPHA * x + mix, ln1_g, ln1_b)
    ffn = (jax.nn.silu(x @ w_gate) * (x @ w_up)) @ w_down
    x = layer_norm(DEEPNORM_ALPHA * x + ffn, ln2_g, ln2_b)
    return x, new_k, new_v, va


def setup_inputs(seed: int = 0) -> dict:
    key = jax.random.key(seed)
    ks = jax.random.split(key, 24)
    f32 = jnp.float32
    nrm = lambda k, shp, sc: jax.random.normal(k, shp, f32) * sc
    return {
        "x_prompt": nrm(ks[0], (BATCH, SEQ, D_MODEL), 1.0),
        "x_sample": nrm(ks[1], (DEC_BATCH, DEC_SEQ, D_MODEL), 1.0),
        "cache_swa_k": nrm(ks[2], (DEPTH, DEC_BATCH, WINDOW, N_KV_HEADS, HEAD_DIM), 1.0),
        "cache_swa_v": nrm(ks[3], (DEPTH, DEC_BATCH, WINDOW, N_KV_HEADS, HEAD_DIM), 1.0),
        "rel_bias": nrm(ks[4], (N_BUCKETS, N_HEADS), 0.5),
        "w_in": nrm(ks[5], (DEPTH, D_MODEL, IN_WIDTH), D_MODEL ** -0.5),
        "ln_v_g": 1.0 + nrm(ks[6], (DEPTH, A_WIDTH), 0.05),
        "ln_v_b": nrm(ks[7], (DEPTH, A_WIDTH), 0.02),
        "w_s": nrm(ks[8], (DEPTH, A_GROUPS, CHUNK, CHUNK), CHUNK ** -0.5),
        "b_s": 1.0 + nrm(ks[9], (DEPTH, A_GROUPS, CHUNK), 0.1),
        "sinks": nrm(ks[10], (DEPTH, N_HEADS), 0.5),
        "w_pa": nrm(ks[11], (DEPTH, A_WIDTH, D_MODEL), A_WIDTH ** -0.5),
        "w_pb": nrm(ks[12], (DEPTH, N_HEADS * HEAD_DIM, D_MODEL), (N_HEADS * HEAD_DIM) ** -0.5),
        "w_o": nrm(ks[13], (DEPTH, D_MODEL, D_MODEL), DEEPNORM_BETA * D_MODEL ** -0.5),
        "ln1_g": 1.0 + nrm(ks[14], (DEPTH, D_MODEL), 0.05),
        "ln1_b": nrm(ks[15], (DEPTH, D_MODEL), 0.02),
        "w_gate": nrm(ks[16], (DEPTH, D_MODEL, D_FF), D_MODEL ** -0.5),
        "w_up": nrm(ks[17], (DEPTH, D_MODEL, D_FF), D_MODEL ** -0.5),
        "w_down": nrm(ks[18], (DEPTH, D_FF, D_MODEL), DEEPNORM_BETA * D_FF ** -0.5),
        "ln2_g": 1.0 + nrm(ks[19], (DEPTH, D_MODEL), 0.05),
        "ln2_b": nrm(ks[20], (DEPTH, D_MODEL), 0.02),
    }


def reference(x_prompt, x_sample, cache_swa_k, cache_swa_v, rel_bias, w_in, ln_v_g, ln_v_b,
              w_s, b_s, sinks, w_pa, w_pb, w_o, ln1_g, ln1_b, w_gate, w_up, w_down,
              ln2_g, ln2_b):
    n_blocks = x_prompt.shape[1] // WINDOW
    qi = jnp.arange(WINDOW, dtype=jnp.int32)[:, None]
    kj = jnp.arange(2 * WINDOW, dtype=jnp.int32)[None, :]
    dist_p = qi + WINDOW - kj
    bias_p = rel_bias_blocks(rel_bias, dist_p)
    blk = jnp.arange(n_blocks, dtype=jnp.int32)[:, None, None]
    mask_p = (dist_p >= 0)[None] & (dist_p < WINDOW)[None] & ((blk - 1) * WINDOW + kj[None] >= 0)
    n_new = x_sample.shape[1]
    qs = jnp.arange(n_new, dtype=jnp.int32)[:, None]
    ksj = jnp.arange(WINDOW + n_new, dtype=jnp.int32)[None, :]
    dist_s = qs + WINDOW - ksj
    bias_s = rel_bias_blocks(rel_bias, dist_s)
    mask_s = ((dist_s >= 0) & (dist_s < WINDOW))[None]

    xp, xs = x_prompt, x_sample
    kp_l, vp_l, ks_l, vs_l, ga_l = [], [], [], [], []
    for l in range(DEPTH):
        w = (w_in[l], ln_v_g[l], ln_v_b[l], w_s[l], b_s[l], sinks[l], w_pa[l], w_pb[l], w_o[l],
             ln1_g[l], ln1_b[l], w_gate[l], w_up[l], w_down[l], ln2_g[l], ln2_b[l])
        xp, kp, vp, _ = trunk_layer(xp, *w, bias_p, mask_p, None, None)
        xs, kn, vn, va_s = trunk_layer(xs, *w, bias_s, mask_s, cache_swa_k[l], cache_swa_v[l])
        kp_l.append(kp); vp_l.append(vp); ks_l.append(kn); vs_l.append(vn); ga_l.append(va_s)
    swa_k_prompt = jnp.stack(kp_l)
    swa_v_prompt = jnp.stack(vp_l)
    swa_k_sample = jnp.stack(ks_l)
    swa_v_sample = jnp.stack(vs_l)
    gmlp_v_sample = jnp.stack(ga_l)
    return (xp, xs, swa_k_prompt, swa_v_prompt, swa_k_sample, swa_v_sample, gmlp_v_sample)
```

```python
import functools
import math

import jax
import jax.numpy as jnp
from jax import lax
from jax.experimental import pallas as pl
from jax.experimental.pallas import tpu as pltpu

D_MODEL = 1024
CHUNK = 128
A_WIDTH = D_MODEL
A_GROUPS = 8
N_HEADS = 16
HEAD_DIM = 64
N_KV_HEADS = 4
Q_PER_KV = N_HEADS // N_KV_HEADS
WINDOW = 128
N_BUCKETS = 32
MAX_DISTANCE = 128
D_FF = 2816
LN_EPS = 1e-5
NEG_INF = -1e30

KV_WIDTH = N_KV_HEADS * HEAD_DIM
Q_WIDTH = N_HEADS * HEAD_DIM
O_U = 0
O_V = O_U + A_WIDTH
O_Q = O_V + A_WIDTH
O_K = O_Q + Q_WIDTH
O_G = O_K + 2 * KV_WIDTH
IN_WIDTH = O_G + 2 * D_MODEL

LANES = 128
SUBLANES = 8
ROW_TILE = 512
MIX_CHUNKS = 2
SAMPLE_SEQS = 8
KEY_PAD = 2 * WINDOW
VMEM_LIMIT = 56 * 1024 * 1024

BF16 = jnp.bfloat16
F32 = jnp.float32


def _layer_norm(x, g, b):
    mu = jnp.mean(x, axis=-1, keepdims=True)
    xc = x - mu
    var = jnp.mean(xc * xc, axis=-1, keepdims=True)
    return xc * lax.rsqrt(var + LN_EPS) * g + b


def _dot(a, b):
    return jnp.dot(a, b, preferred_element_type=F32)


def _dot_nt(a, b):
    return lax.dot_general(a, b, (((1,), (1,)), ((), ())), preferred_element_type=F32)


def _resident(block_shape, index_map):
    return pl.BlockSpec(block_shape, index_map, pipeline_mode=pl.Buffered(1))


def _params(n_axes):
    return pltpu.CompilerParams(dimension_semantics=("arbitrary",) * n_axes,
                                vmem_limit_bytes=VMEM_LIMIT)


def _bias_kernel(rb_ref, bucket_ref, out_ref):
    bk = bucket_ref[...]
    for h in range(N_HEADS):
        acc = jnp.full(bk.shape, NEG_INF, F32)
        for b in range(N_BUCKETS):
            acc = jnp.where(bk == b, rb_ref[b, h], acc)
        out_ref[h] = acc


def _bias_tables(rel_bias, buckets):
    n, t, kp = buckets.shape
    return pl.pallas_call(
        _bias_kernel,
        out_shape=jax.ShapeDtypeStruct((n, N_HEADS, t, kp), F32),
        grid=(n,),
        in_specs=[pl.BlockSpec(memory_space=pltpu.SMEM),
                  pl.BlockSpec((None, t, kp), lambda i: (i, 0, 0))],
        out_specs=pl.BlockSpec((None, N_HEADS, t, kp), lambda i: (i, 0, 0, 0)),
        compiler_params=_params(1),
        name="bias_tables",
    )(rel_bias, buckets)


def _rel_bucket(dist):
    n = jnp.maximum(dist, 0)
    max_exact = N_BUCKETS // 2
    nf = jnp.maximum(n, 1).astype(F32)
    large = max_exact + (jnp.log(nf / max_exact) / math.log(MAX_DISTANCE / max_exact)
                         * (N_BUCKETS - max_exact)).astype(jnp.int32)
    large = jnp.minimum(large, N_BUCKETS - 1)
    return jnp.where(n < max_exact, n, large)


def _masked_buckets(n_q, n_keys, first_block):
    qi = jnp.arange(n_q, dtype=jnp.int32)[:, None]
    kj = jnp.arange(KEY_PAD, dtype=jnp.int32)[None, :]
    dist = qi + WINDOW - kj
    ok = (dist >= 0) & (dist < WINDOW) & (kj < n_keys)
    if first_block:
        ok = ok & (kj >= WINDOW)
    return jnp.where(ok, _rel_bucket(dist), -1)


def _inproj_kernel(x_ref, w_ref, g_ref, b_ref, u_ref, va_ref, q_ref, kv_ref, gate_ref, *va32_ref):
    xb = x_ref[...].astype(BF16)

    def proj(c0, c1):
        return _dot(xb, w_ref[:, c0:c1])

    half = A_WIDTH // 2
    for c0 in range(O_U, O_V, half):
        u_ref[:, c0:c0 + half] = jax.nn.gelu(proj(c0, c0 + half)).astype(BF16)
    va = _layer_norm(jax.nn.gelu(proj(O_V, O_Q)), g_ref[...], b_ref[...])
    va_ref[...] = va.astype(BF16)
    if va32_ref:
        va32_ref[0][...] = va
    for c0 in range(0, Q_WIDTH, half):
        q_ref[:, c0:c0 + half] = (proj(O_Q + c0, O_Q + c0 + half) * (HEAD_DIM ** -0.5)).astype(BF16)
    kv_ref[...] = proj(O_K, O_G)
    for c0 in range(0, 2 * D_MODEL, half):
        gate_ref[:, c0:c0 + half] = jax.nn.sigmoid(proj(O_G + c0, O_G + c0 + half)).astype(BF16)


def _inproj(x, w_in, ln_g, ln_b, layer, with_va32):
    rows = x.shape[0]
    row_spec = lambda width: pl.BlockSpec((ROW_TILE, width), lambda i: (i, 0))
    vec_spec = pl.BlockSpec((None, 1, A_WIDTH), lambda i: (layer, 0, 0))
    out_shape = [jax.ShapeDtypeStruct((rows, A_WIDTH), BF16),
                 jax.ShapeDtypeStruct((rows, A_WIDTH), BF16),
                 jax.ShapeDtypeStruct((rows, Q_WIDTH), BF16),
                 jax.ShapeDtypeStruct((rows, 2 * KV_WIDTH), F32),
                 jax.ShapeDtypeStruct((rows, 2 * D_MODEL), BF16)]
    out_specs = [row_spec(A_WIDTH), row_spec(A_WIDTH), row_spec(Q_WIDTH),
                 row_spec(2 * KV_WIDTH), row_spec(2 * D_MODEL)]
    if with_va32:
        out_shape.append(jax.ShapeDtypeStruct((rows, A_WIDTH), F32))
        out_specs.append(row_spec(A_WIDTH))
    return pl.pallas_call(
        _inproj_kernel,
        out_shape=out_shape,
        grid=(rows // ROW_TILE,),
        in_specs=[row_spec(D_MODEL),
                  _resident((None, D_MODEL, IN_WIDTH), lambda i: (layer, 0, 0)),
                  vec_spec, vec_spec],
        out_specs=out_specs,
        compiler_params=_params(1),
        name="inproj",
    )(x, w_in, ln_g, ln_b)


def _softmax_sink_pv(s, sink, v):
    m = jnp.maximum(jnp.max(s, axis=-1, keepdims=True), sink)
    p = jnp.exp(s - m)
    denom = jnp.sum(p, axis=-1, keepdims=True) + jnp.exp(sink - m)
    return _dot(p.astype(BF16), v) * (1.0 / denom)


def _prompt_mix_kernel(sinks_ref, u_ref, va_ref, q_ref, kv_ref, kvp_ref, ws_ref, bs_ref, bias_ref,
                       ya_ref, yb_ref):
    first_tile = pl.program_id(1) == 0
    tri = (lax.broadcasted_iota(jnp.int32, (CHUNK, CHUNK), 0)
           >= lax.broadcasted_iota(jnp.int32, (CHUNK, CHUNK), 1))
    low_half = lax.broadcasted_iota(jnp.int32, (1, LANES), 1) < HEAD_DIM
    zero = jnp.zeros((), BF16)

    for c in range(MIX_CHUNKS):
        rows = slice(c * CHUNK, (c + 1) * CHUNK)

        for g in range(A_GROUPS):
            cols = slice(g * LANES, (g + 1) * LANES)
            w = jnp.where(tri, ws_ref[g], 0.0).astype(BF16)
            mixed = _dot(w, va_ref[rows, cols]) + bs_ref[:, g:g + 1]
            ya_ref[rows, cols] = (u_ref[rows, cols].astype(F32) * mixed).astype(BF16)

        prev = kvp_ref[...] if c == 0 else kv_ref[(c - 1) * CHUNK:c * CHUNK, :]
        band = jnp.concatenate([prev, kv_ref[rows, :]], axis=0).astype(BF16)
        table = jnp.where(first_tile, 1, 0) if c == 0 else 0
        for pair in range(N_HEADS // 2):
            r, gp = pair // 2, pair % 2
            qp = q_ref[rows, pair * LANES:(pair + 1) * LANES]
            kn = band[:, gp * LANES:(gp + 1) * LANES]
            vn = band[:, KV_WIDTH + gp * LANES:KV_WIDTH + (gp + 1) * LANES]
            out = None
            for hi in range(2):
                keep = low_half if hi == 0 else jnp.logical_not(low_half)
                head = Q_PER_KV * (2 * gp + hi) + r
                s = _dot_nt(qp, jnp.where(keep, kn, zero)) + bias_ref[table, head]
                o = _softmax_sink_pv(s, sinks_ref[head], jnp.where(keep, vn, zero))
                out = o if out is None else jnp.where(low_half, out, o)
            yb_ref[rows, pair * LANES:(pair + 1) * LANES] = out.astype(BF16)


def _prompt_mix(sinks, u, va, q, kv, w_s, b_s_t, bias, layer, batch, seq):
    tile = MIX_CHUNKS * CHUNK
    n_tiles = seq // tile
    u, va, q = (t.reshape(batch, seq, t.shape[-1]) for t in (u, va, q))
    kv = kv.reshape(batch, seq, 2 * KV_WIDTH)
    tok_spec = lambda width: pl.BlockSpec((None, tile, width), lambda b, i: (b, i, 0))
    prev_spec = pl.BlockSpec((None, CHUNK, 2 * KV_WIDTH),
                             lambda b, i: (b, jnp.maximum(i * MIX_CHUNKS - 1, 0), 0))
    ya, yb = pl.pallas_call(
        _prompt_mix_kernel,
        out_shape=[jax.ShapeDtypeStruct((batch, seq, A_WIDTH), BF16),
                   jax.ShapeDtypeStruct((batch, seq, Q_WIDTH), BF16)],
        grid=(batch, n_tiles),
        in_specs=[pl.BlockSpec(memory_space=pltpu.SMEM),
                  tok_spec(A_WIDTH), tok_spec(A_WIDTH), tok_spec(Q_WIDTH), tok_spec(2 * KV_WIDTH),
                  prev_spec,
                  pl.BlockSpec((None, A_GROUPS, CHUNK, CHUNK), lambda b, i: (layer, 0, 0, 0)),
                  pl.BlockSpec((None, CHUNK, A_GROUPS), lambda b, i: (layer, 0, 0)),
                  pl.BlockSpec((2, N_HEADS, WINDOW, KEY_PAD), lambda b, i: (0, 0, 0, 0))],
        out_specs=[tok_spec(A_WIDTH), tok_spec(Q_WIDTH)],
        compiler_params=_params(2),
        name="prompt_mix",
    )(sinks, u, va, q, kv, kv, w_s, b_s_t, bias)
    return ya.reshape(batch * seq, A_WIDTH), yb.reshape(batch * seq, Q_WIDTH)


def _sample_mix_kernel(sinks_ref, u_ref, va_ref, q_ref, kvn_ref, ck_ref, cv_ref, wexp_ref, bsexp_ref,
                       bias_ref, ya_ref, yb_ref, ko_ref, vo_ref, yb_acc):
    n_new = SUBLANES
    rows = SAMPLE_SEQS * n_new

    va3 = va_ref[...].reshape(SAMPLE_SEQS, n_new, A_WIDTH)
    t_idx = lax.broadcasted_iota(jnp.int32, (n_new, A_WIDTH), 0)
    mixed = jnp.broadcast_to(bsexp_ref[...][None], va3.shape)
    for s in range(n_new):
        w = jnp.where(t_idx >= s, wexp_ref[s], 0.0)
        mixed = mixed + w[None] * va3[:, s:s + 1, :]
    u3 = u_ref[...].astype(F32).reshape(SAMPLE_SEQS, n_new, A_WIDTH)
    ya_ref[...] = (u3 * mixed).reshape(rows, A_WIDTH).astype(BF16)

    group_of_lane = lax.broadcasted_iota(jnp.int32, (1, KV_WIDTH), 1) // HEAD_DIM
    row = lax.broadcasted_iota(jnp.int32, (N_HEADS * n_new, 1), 0)
    head_of_row = Q_PER_KV * ((row // n_new) % N_KV_HEADS) + row // (n_new * N_KV_HEADS)
    sink = jnp.zeros((N_HEADS * n_new, 1), F32)
    for h in range(N_HEADS):
        sink = jnp.where(head_of_row == h, sinks_ref[h], sink)
    q32 = q_ref[...].astype(F32)
    pad = jnp.zeros((KEY_PAD - WINDOW - n_new, KV_WIDTH), F32)
    bias = bias_ref[...]
    for b in range(SAMPLE_SEQS):
        new = slice(b * n_new, (b + 1) * n_new)
        k_new, v_new = kvn_ref[new, 0:KV_WIDTH], kvn_ref[new, KV_WIDTH:2 * KV_WIDTH]
        pieces = []
        for r in range(Q_PER_KV):
            blk = q32[new, r * KV_WIDTH:(r + 1) * KV_WIDTH]
            for g in range(N_KV_HEADS):
                pieces.append(jnp.where(group_of_lane == g, blk, 0.0))
        q_rows = jnp.concatenate(pieces, axis=0).astype(BF16)
        k_all = jnp.concatenate([ck_ref[b], k_new, pad], axis=0).astype(BF16)
        v_all = jnp.concatenate([cv_ref[b], v_new, pad], axis=0).astype(BF16)
        o = _softmax_sink_pv(_dot_nt(q_rows, k_all) + bias, sink, v_all)
        for r in range(Q_PER_KV):
            acc = jnp.zeros((n_new, KV_WIDTH), F32)
            for g in range(N_KV_HEADS):
                r0 = (r * N_KV_HEADS + g) * n_new
                acc = jnp.where(group_of_lane == g, o[r0:r0 + n_new, :], acc)
            yb_acc[new, r * KV_WIDTH:(r + 1) * KV_WIDTH] = acc
        ko_ref[b, 0:WINDOW - n_new, :] = ck_ref[b, n_new:WINDOW, :]
        ko_ref[b, WINDOW - n_new:WINDOW, :] = k_new
        vo_ref[b, 0:WINDOW - n_new, :] = cv_ref[b, n_new:WINDOW, :]
        vo_ref[b, WINDOW - n_new:WINDOW, :] = v_new
    yb_ref[...] = yb_acc[...].astype(BF16)


def _sample_mix(sinks, u, va32, q, kvn, cache_k, cache_v, wexp, bsexp, bias, layer, n_seq):
    rows = SAMPLE_SEQS * SUBLANES
    row_spec = lambda width: pl.BlockSpec((rows, width), lambda i: (i, 0))
    cache_spec = pl.BlockSpec((None, SAMPLE_SEQS, WINDOW, KV_WIDTH), lambda i: (layer, i, 0, 0))
    new_cache_spec = pl.BlockSpec((SAMPLE_SEQS, WINDOW, KV_WIDTH), lambda i: (i, 0, 0))
    return pl.pallas_call(
        _sample_mix_kernel,
        out_shape=[jax.ShapeDtypeStruct((n_seq * SUBLANES, A_WIDTH), BF16),
                   jax.ShapeDtypeStruct((n_seq * SUBLANES, Q_WIDTH), BF16),
                   jax.ShapeDtypeStruct((n_seq, WINDOW, KV_WIDTH), F32),
                   jax.ShapeDtypeStruct((n_seq, WINDOW, KV_WIDTH), F32)],
        grid=(n_seq // SAMPLE_SEQS,),
        in_specs=[pl.BlockSpec(memory_space=pltpu.SMEM),
                  row_spec(A_WIDTH), row_spec(A_WIDTH), row_spec(Q_WIDTH), row_spec(2 * KV_WIDTH),
                  cache_spec, cache_spec,
                  pl.BlockSpec((None, SUBLANES, SUBLANES, A_WIDTH), lambda i: (layer, 0, 0, 0)),
                  pl.BlockSpec((None, SUBLANES, A_WIDTH), lambda i: (layer, 0, 0)),
                  pl.BlockSpec((N_HEADS * SUBLANES, KEY_PAD), lambda i: (0, 0))],
        out_specs=[row_spec(A_WIDTH), row_spec(Q_WIDTH), new_cache_spec, new_cache_spec],
        scratch_shapes=[pltpu.VMEM((rows, Q_WIDTH), F32)],
        compiler_params=_params(1),
        name="sample_mix",
    )(sinks, u, va32, q, kvn, cache_k, cache_v, wexp, bsexp, bias)


def _merge_ffn_kernel(x_ref, ya_ref, yb_ref, gate_ref, wpa_ref, wpb_ref, wo_ref, wg_ref, wu_ref, wd_ref,
                      ln_ref, out_ref, *, alpha):
    g_a = gate_ref[:, 0:D_MODEL].astype(F32)
    g_b = gate_ref[:, D_MODEL:2 * D_MODEL].astype(F32)
    merged = g_a * _dot(ya_ref[...], wpa_ref[...]) + g_b * _dot(yb_ref[...], wpb_ref[...])
    mix = _dot(merged.astype(BF16), wo_ref[...])
    x1 = _layer_norm(alpha * x_ref[...] + mix, ln_ref[0:1, :], ln_ref[1:2, :])
    x1b = x1.astype(BF16)
    act = (jax.nn.silu(_dot(x1b, wg_ref[...])) * _dot(x1b, wu_ref[...])).astype(BF16)
    ffn = _dot(act, wd_ref[...])
    out_ref[...] = _layer_norm(alpha * x1 + ffn, ln_ref[2:3, :], ln_ref[3:4, :])


def _merge_ffn(x, ya, yb, gates, w_pa, w_pb, w_o, w_gate, w_up, w_down, ln_pack, layer, alpha):
    rows = x.shape[0]
    row_spec = lambda width: pl.BlockSpec((ROW_TILE, width), lambda i: (i, 0))
    weight = lambda k, n: _resident((None, k, n), lambda i: (layer, 0, 0))
    return pl.pallas_call(
        functools.partial(_merge_ffn_kernel, alpha=alpha),
        out_shape=jax.ShapeDtypeStruct((rows, D_MODEL), F32),
        grid=(rows // ROW_TILE,),
        in_specs=[row_spec(D_MODEL), row_spec(A_WIDTH), row_spec(Q_WIDTH), row_spec(2 * D_MODEL),
                  weight(A_WIDTH, D_MODEL), weight(Q_WIDTH, D_MODEL), weight(D_MODEL, D_MODEL),
                  weight(D_MODEL, D_FF), weight(D_MODEL, D_FF), weight(D_FF, D_MODEL),
                  pl.BlockSpec((None, 4, D_MODEL), lambda i: (layer, 0, 0))],
        out_specs=row_spec(D_MODEL),
        compiler_params=_params(1),
        name="merge_ffn",
    )(x, ya, yb, gates, w_pa, w_pb, w_o, w_gate, w_up, w_down, ln_pack)


def kernel(x_prompt, x_sample, cache_swa_k, cache_swa_v, rel_bias, w_in, ln_v_g, ln_v_b, w_s, b_s,
           sinks, w_pa, w_pb, w_o, ln1_g, ln1_b, w_gate, w_up, w_down, ln2_g, ln2_b):
    depth = w_in.shape[0]
    batch, seq, _ = x_prompt.shape
    n_seq, n_new, _ = x_sample.shape
    assert n_new == SUBLANES and seq % (MIX_CHUNKS * CHUNK) == 0 and n_seq % SAMPLE_SEQS == 0
    assert (batch * seq) % ROW_TILE == 0 and (n_seq * n_new) % ROW_TILE == 0
    alpha = (2 * depth) ** 0.25

    def heads_rg(w, axis):
        shape = w.shape
        w = w.reshape(shape[:axis] + (N_KV_HEADS, Q_PER_KV, HEAD_DIM) + shape[axis + 1:])
        return jnp.swapaxes(w, axis, axis + 1).reshape(shape)

    w_in_b = jnp.concatenate([w_in[..., :O_Q], heads_rg(w_in[..., O_Q:O_K], 2), w_in[..., O_K:]],
                             axis=-1).astype(BF16)
    w_pa_b, w_o_b = w_pa.astype(BF16), w_o.astype(BF16)
    w_pb_b = heads_rg(w_pb, 1).astype(BF16)
    w_gate_b, w_up_b, w_down_b = w_gate.astype(BF16), w_up.astype(BF16), w_down.astype(BF16)
    ln_v_g3, ln_v_b3 = ln_v_g[:, None, :], ln_v_b[:, None, :]
    ln_pack = jnp.stack([ln1_g, ln1_b, ln2_g, ln2_b], axis=1)
    b_s_t = jnp.swapaxes(b_s, 1, 2)
    wexp = jnp.repeat(jnp.transpose(w_s[:, :, :n_new, :n_new], (0, 3, 2, 1)), A_WIDTH // A_GROUPS, axis=-1)
    bsexp = jnp.repeat(jnp.swapaxes(b_s[:, :, :n_new], 1, 2), A_WIDTH // A_GROUPS, axis=-1)

    bias_p = _bias_tables(rel_bias, jnp.stack([_masked_buckets(WINDOW, KEY_PAD, False),
                                               _masked_buckets(WINDOW, KEY_PAD, True)]))
    bias_s = _bias_tables(rel_bias, _masked_buckets(n_new, WINDOW + n_new, False)[None])[0]
    bias_s = bias_s.reshape(N_KV_HEADS, Q_PER_KV, n_new, KEY_PAD)
    bias_s = jnp.swapaxes(bias_s, 0, 1).reshape(N_HEADS * n_new, KEY_PAD)

    cache_k = cache_swa_k.reshape(depth, n_seq, WINDOW, KV_WIDTH)
    cache_v = cache_swa_v.reshape(depth, n_seq, WINDOW, KV_WIDTH)

    xp = x_prompt.reshape(batch * seq, D_MODEL)
    xs = x_sample.reshape(n_seq * n_new, D_MODEL)
    kp_l, vp_l, ks_l, vs_l, ga_l = [], [], [], [], []
    for l in range(depth):
        u, va, q, kv, gates = _inproj(xp, w_in_b, ln_v_g3, ln_v_b3, l, False)
        ya, yb = _prompt_mix(sinks[l], u, va, q, kv, w_s, b_s_t, bias_p, l, batch, seq)
        xp = _merge_ffn(xp, ya, yb, gates, w_pa_b, w_pb_b, w_o_b, w_gate_b, w_up_b, w_down_b,
                        ln_pack, l, alpha)
        kv_tail = kv.reshape(batch, seq, 2 * KV_WIDTH)[:, seq - WINDOW:, :]
        kp_l.append(kv_tail[..., :KV_WIDTH].reshape(batch, WINDOW, N_KV_HEADS, HEAD_DIM))
        vp_l.append(kv_tail[..., KV_WIDTH:].reshape(batch, WINDOW, N_KV_HEADS, HEAD_DIM))

        u, _, q, kvn, gates, va32 = _inproj(xs, w_in_b, ln_v_g3, ln_v_b3, l, True)
        ya, yb, k_out, v_out = _sample_mix(sinks[l], u, va32, q, kvn, cache_k, cache_v, wexp, bsexp,
                                           bias_s, l, n_seq)
        xs = _merge_ffn(xs, ya, yb, gates, w_pa_b, w_pb_b, w_o_b, w_gate_b, w_up_b, w_down_b,
                        ln_pack, l, alpha)
        ks_l.append(k_out.reshape(n_seq, WINDOW, N_KV_HEADS, HEAD_DIM))
        vs_l.append(v_out.reshape(n_seq, WINDOW, N_KV_HEADS, HEAD_DIM))
        ga_l.append(va32.reshape(n_seq, n_new, A_WIDTH))

    return (xp.reshape(batch, seq, D_MODEL), xs.reshape(n_seq, n_new, D_MODEL),
            jnp.stack(kp_l), jnp.stack(vp_l), jnp.stack(ks_l), jnp.stack(vs_l), jnp.stack(ga_l))
```

```python
import functools
import math

import jax
import jax.numpy as jnp
from jax import lax
from jax.experimental import pallas as pl
from jax.experimental.pallas import tpu as pltpu

D_MODEL = 1024
CHUNK = 128
A_WIDTH = D_MODEL
A_GROUPS = 8
N_HEADS = 16
HEAD_DIM = 64
N_KV_HEADS = 4
Q_PER_KV = N_HEADS // N_KV_HEADS
WINDOW = 128
N_BUCKETS = 32
MAX_DISTANCE = 128
D_FF = 2816
LN_EPS = 1e-5
NEG_INF = -1e30

KV_WIDTH = N_KV_HEADS * HEAD_DIM
Q_WIDTH = N_HEADS * HEAD_DIM
O_U = 0
O_V = O_U + A_WIDTH
O_Q = O_V + A_WIDTH
O_K = O_Q + Q_WIDTH
O_G = O_K + 2 * KV_WIDTH
IN_WIDTH = O_G + 2 * D_MODEL

LANES = 128
SUBLANES = 8
ROW_TILE = 512
MIX_CHUNKS = 2
SAMPLE_SEQS = LANES // SUBLANES
KEY_PAD = 2 * WINDOW
VMEM_LIMIT = 56 * 1024 * 1024

BF16 = jnp.bfloat16
F32 = jnp.float32


def _layer_norm(x, g, b):
    mu = jnp.mean(x, axis=-1, keepdims=True)
    xc = x - mu
    var = jnp.mean(xc * xc, axis=-1, keepdims=True)
    return xc * lax.rsqrt(var + LN_EPS) * g + b


def _dot(a, b):
    return jnp.dot(a, b, preferred_element_type=F32)


def _dot_nt(a, b):
    return lax.dot_general(a, b, (((1,), (1,)), ((), ())), preferred_element_type=F32)


def _resident(block_shape, index_map):
    return pl.BlockSpec(block_shape, index_map, pipeline_mode=pl.Buffered(1))


def _params(n_axes):
    return pltpu.CompilerParams(dimension_semantics=("arbitrary",) * n_axes,
                                vmem_limit_bytes=VMEM_LIMIT)


def _bias_kernel(rb_ref, bucket_ref, out_ref):
    bk = bucket_ref[...]
    for h in range(N_HEADS):
        acc = jnp.full(bk.shape, NEG_INF, F32)
        for b in range(N_BUCKETS):
            acc = jnp.where(bk == b, rb_ref[b, h], acc)
        out_ref[h] = acc


def _bias_tables(rel_bias, buckets):
    n, t, kp = buckets.shape
    return pl.pallas_call(
        _bias_kernel,
        out_shape=jax.ShapeDtypeStruct((n, N_HEADS, t, kp), F32),
        grid=(n,),
        in_specs=[pl.BlockSpec(memory_space=pltpu.SMEM),
                  pl.BlockSpec((None, t, kp), lambda i: (i, 0, 0))],
        out_specs=pl.BlockSpec((None, N_HEADS, t, kp), lambda i: (i, 0, 0, 0)),
        compiler_params=_params(1),
        name="bias_tables",
    )(rel_bias, buckets)


def _rel_bucket(dist):
    n = jnp.maximum(dist, 0)
    max_exact = N_BUCKETS // 2
    nf = jnp.maximum(n, 1).astype(F32)
    large = max_exact + (jnp.log(nf / max_exact) / math.log(MAX_DISTANCE / max_exact)
                         * (N_BUCKETS - max_exact)).astype(jnp.int32)
    large = jnp.minimum(large, N_BUCKETS - 1)
    return jnp.where(n < max_exact, n, large)


def _masked_buckets(n_q, n_keys, first_block):
    qi = jnp.arange(n_q, dtype=jnp.int32)[:, None]
    kj = jnp.arange(KEY_PAD, dtype=jnp.int32)[None, :]
    dist = qi + WINDOW - kj
    ok = (dist >= 0) & (dist < WINDOW) & (kj < n_keys)
    if first_block:
        ok = ok & (kj >= WINDOW)
    return jnp.where(ok, _rel_bucket(dist), -1)


def _inproj_kernel(*refs, sample):
    if sample:
        x_ref, w_ref, wq_ref, wkvt_ref, g_ref, b_ref, u_ref, va_ref, q_ref, gate_ref, kvt_ref = refs
    else:
        x_ref, w_ref, wq_ref, g_ref, b_ref, u_ref, va_ref, q_ref, gate_ref, kv_ref = refs
    xb = x_ref[...].astype(BF16)

    def proj(c0, c1):
        return _dot(xb, w_ref[:, c0:c1])

    half = A_WIDTH // 2
    for c0 in range(O_U, O_V, half):
        u_ref[:, c0:c0 + half] = jax.nn.gelu(proj(c0, c0 + half)).astype(BF16)
    va = _layer_norm(jax.nn.gelu(proj(O_V, O_Q)), g_ref[...], b_ref[...])
    va_ref[...] = va.astype(va_ref.dtype)
    for c0 in range(0, Q_WIDTH, half):
        q_ref[:, c0:c0 + half] = (_dot(xb, wq_ref[:, c0:c0 + half]) * (HEAD_DIM ** -0.5)).astype(BF16)
    if sample:
        kvt_ref[...] = _dot_nt(wkvt_ref[...], xb)
    else:
        kv_ref[...] = proj(O_K, O_G)
    for c0 in range(0, 2 * D_MODEL, half):
        gate_ref[:, c0:c0 + half] = jax.nn.sigmoid(proj(O_G + c0, O_G + c0 + half)).astype(BF16)


def _inproj(x, w_in, w_q, w_kvt, ln_g, ln_b, layer, sample):
    rows = x.shape[0]
    row_spec = lambda width: pl.BlockSpec((ROW_TILE, width), lambda i: (i, 0))
    vec_spec = pl.BlockSpec((None, 1, A_WIDTH), lambda i: (layer, 0, 0))
    in_specs = [row_spec(D_MODEL),
                _resident((None, D_MODEL, IN_WIDTH), lambda i: (layer, 0, 0)),
                _resident((None, D_MODEL, Q_WIDTH), lambda i: (layer, 0, 0))]
    operands = [x, w_in, w_q]
    if sample:
        in_specs.append(_resident((None, 2 * KV_WIDTH, D_MODEL), lambda i: (layer, 0, 0)))
        operands.append(w_kvt)
    out_shape = [jax.ShapeDtypeStruct((rows, A_WIDTH), BF16),
                 jax.ShapeDtypeStruct((rows, A_WIDTH), F32 if sample else BF16),
                 jax.ShapeDtypeStruct((rows, Q_WIDTH), BF16),
                 jax.ShapeDtypeStruct((rows, 2 * D_MODEL), BF16)]
    out_specs = [row_spec(A_WIDTH), row_spec(A_WIDTH), row_spec(Q_WIDTH), row_spec(2 * D_MODEL)]
    if sample:
        out_shape.append(jax.ShapeDtypeStruct((2 * KV_WIDTH, rows), F32))
        out_specs.append(pl.BlockSpec((2 * KV_WIDTH, ROW_TILE), lambda i: (0, i)))
    else:
        out_shape.append(jax.ShapeDtypeStruct((rows, 2 * KV_WIDTH), F32))
        out_specs.append(row_spec(2 * KV_WIDTH))
    return pl.pallas_call(
        functools.partial(_inproj_kernel, sample=sample),
        out_shape=out_shape,
        grid=(rows // ROW_TILE,),
        in_specs=in_specs + [vec_spec, vec_spec],
        out_specs=out_specs,
        compiler_params=_params(1),
        name="inproj_sample" if sample else "inproj",
    )(*operands, ln_g, ln_b)


def _softmax_sink_pv(s, sink, v):
    m = jnp.maximum(jnp.max(s, axis=-1, keepdims=True), sink)
    p = jnp.exp(s - m)
    denom = jnp.sum(p, axis=-1, keepdims=True) + jnp.exp(sink - m)
    return _dot(p.astype(BF16), v) * (1.0 / denom)


def _prompt_mix_kernel(sinks_ref, u_ref, va_ref, q_ref, kv_ref, kvp_ref, ws_ref, bs_ref, bias_ref,
                       ya_ref, yb_ref):
    first_tile = pl.program_id(1) == 0
    tri = (lax.broadcasted_iota(jnp.int32, (CHUNK, CHUNK), 0)
           >= lax.broadcasted_iota(jnp.int32, (CHUNK, CHUNK), 1))
    low_half = lax.broadcasted_iota(jnp.int32, (1, LANES), 1) < HEAD_DIM
    zero = jnp.zeros((), BF16)

    for c in range(MIX_CHUNKS):
        rows = slice(c * CHUNK, (c + 1) * CHUNK)

        for g in range(A_GROUPS):
            cols = slice(g * LANES, (g + 1) * LANES)
            w = jnp.where(tri, ws_ref[g], 0.0).astype(BF16)
            mixed = _dot(w, va_ref[rows, cols]) + bs_ref[:, g:g + 1]
            ya_ref[rows, cols] = (u_ref[rows, cols].astype(F32) * mixed).astype(BF16)

        prev = kvp_ref[...] if c == 0 else kv_ref[(c - 1) * CHUNK:c * CHUNK, :]
        band = jnp.concatenate([prev, kv_ref[rows, :]], axis=0).astype(BF16)
        table = jnp.where(first_tile, 1, 0) if c == 0 else 0
        for pair in range(N_HEADS // 2):
            r, gp = pair // 2, pair % 2
            qp = q_ref[rows, pair * LANES:(pair + 1) * LANES]
            kn = band[:, gp * LANES:(gp + 1) * LANES]
            vn = band[:, KV_WIDTH + gp * LANES:KV_WIDTH + (gp + 1) * LANES]
            out = None
            for hi in range(2):
                keep = low_half if hi == 0 else jnp.logical_not(low_half)
                head = Q_PER_KV * (2 * gp + hi) + r
                s = _dot_nt(qp, jnp.where(keep, kn, zero)) + bias_ref[table, head]
                o = _softmax_sink_pv(s, sinks_ref[head], jnp.where(keep, vn, zero))
                out = o if out is None else jnp.where(low_half, out, o)
            yb_ref[rows, pair * LANES:(pair + 1) * LANES] = out.astype(BF16)


def _prompt_mix(sinks, u, va, q, kv, w_s, b_s_t, bias, layer, batch, seq):
    tile = MIX_CHUNKS * CHUNK
    n_tiles = seq // tile
    u, va, q = (t.reshape(batch, seq, t.shape[-1]) for t in (u, va, q))
    kv = kv.reshape(batch, seq, 2 * KV_WIDTH)
    tok_spec = lambda width: pl.BlockSpec((None, tile, width), lambda b, i: (b, i, 0))
    prev_spec = pl.BlockSpec((None, CHUNK, 2 * KV_WIDTH),
                             lambda b, i: (b, jnp.maximum(i * MIX_CHUNKS - 1, 0), 0))
    ya, yb = pl.pallas_call(
        _prompt_mix_kernel,
        out_shape=[jax.ShapeDtypeStruct((batch, seq, A_WIDTH), BF16),
                   jax.ShapeDtypeStruct((batch, seq, Q_WIDTH), BF16)],
        grid=(batch, n_tiles),
        in_specs=[pl.BlockSpec(memory_space=pltpu.SMEM),
                  tok_spec(A_WIDTH), tok_spec(A_WIDTH), tok_spec(Q_WIDTH), tok_spec(2 * KV_WIDTH),
                  prev_spec,
                  pl.BlockSpec((None, A_GROUPS, CHUNK, CHUNK), lambda b, i: (layer, 0, 0, 0)),
                  pl.BlockSpec((None, CHUNK, A_GROUPS), lambda b, i: (layer, 0, 0)),
                  pl.BlockSpec((2, N_HEADS, WINDOW, KEY_PAD), lambda b, i: (0, 0, 0, 0))],
        out_specs=[tok_spec(A_WIDTH), tok_spec(Q_WIDTH)],
        compiler_params=_params(2),
        name="prompt_mix",
    )(sinks, u, va, q, kv, kv, w_s, b_s_t, bias)
    return ya.reshape(batch * seq, A_WIDTH), yb.reshape(batch * seq, Q_WIDTH)


def _sample_mix_kernel(*refs, aliased):
    (sinks_ref, u_ref, va_ref, q_ref, kvt_ref, ck_ref, cv_ref, wexp_ref, bsexp_ref, biasc_ref,
     biasn_ref) = refs[:11]
    ya_ref, yb_ref, ko_ref, vo_ref, yb_acc = refs[11 + (2 if aliased else 0):]
    n_new = SUBLANES
    rows = SAMPLE_SEQS * n_new

    va3 = va_ref[...].reshape(SAMPLE_SEQS, n_new, A_WIDTH)
    t_idx = lax.broadcasted_iota(jnp.int32, (n_new, A_WIDTH), 0)
    mixed = jnp.broadcast_to(bsexp_ref[...][None], va3.shape)
    for s in range(n_new):
        w = jnp.where(t_idx >= s, wexp_ref[s], 0.0)
        mixed = mixed + w[None] * va3[:, s:s + 1, :]
    u3 = u_ref[...].astype(F32).reshape(SAMPLE_SEQS, n_new, A_WIDTH)
    ya_ref[...] = (u3 * mixed).reshape(rows, A_WIDTH).astype(BF16)

    group_of_lane = lax.broadcasted_iota(jnp.int32, (1, KV_WIDTH), 1) // HEAD_DIM
    row = lax.broadcasted_iota(jnp.int32, (N_HEADS * n_new, 1), 0)
    head_of_row = Q_PER_KV * ((row // n_new) % N_KV_HEADS) + row // (n_new * N_KV_HEADS)
    sink = jnp.zeros((N_HEADS * n_new, 1), F32)
    for h in range(N_HEADS):
        sink = jnp.where(head_of_row == h, sinks_ref[h], sink)
    lane = lax.broadcasted_iota(jnp.int32, (1, LANES), 1)
    seq_of_lane = lane // n_new
    keep_old = lane < WINDOW - n_new
    q32 = q_ref[...].astype(F32)
    k_new, v_new = kvt_ref[0:KV_WIDTH, :], kvt_ref[KV_WIDTH:2 * KV_WIDTH, :]
    k_new_b, v_new_b = k_new.astype(BF16), v_new.astype(BF16)
    bias_c, bias_n = biasc_ref[...], biasn_ref[...]
    for b in range(SAMPLE_SEQS):
        new = slice(b * n_new, (b + 1) * n_new)
        pieces = []
        for r in range(Q_PER_KV):
            blk = q32[new, r * KV_WIDTH:(r + 1) * KV_WIDTH]
            for g in range(N_KV_HEADS):
                pieces.append(jnp.where(group_of_lane == g, blk, 0.0))
        q_rows = jnp.concatenate(pieces, axis=0).astype(BF16)
        k_old, v_old = ck_ref[b], cv_ref[b]
        k_all = jnp.concatenate([k_old.astype(BF16), k_new_b], axis=1)
        v_all = jnp.concatenate([v_old.astype(BF16), v_new_b], axis=1)
        bias = jnp.concatenate([bias_c, jnp.where(seq_of_lane == b, bias_n, NEG_INF)], axis=1)
        s = _dot(q_rows, k_all) + bias
        m = jnp.maximum(jnp.max(s, axis=-1, keepdims=True), sink)
        p = jnp.exp(s - m)
        denom = jnp.sum(p, axis=-1, keepdims=True) + jnp.exp(sink - m)
        o = _dot_nt(p.astype(BF16), v_all) * (1.0 / denom)
        for r in range(Q_PER_KV):
            acc = jnp.zeros((n_new, KV_WIDTH), F32)
            for g in range(N_KV_HEADS):
                r0 = (r * N_KV_HEADS + g) * n_new
                acc = jnp.where(group_of_lane == g, o[r0:r0 + n_new, :], acc)
            yb_acc[new, r * KV_WIDTH:(r + 1) * KV_WIDTH] = acc
        shift_new = (WINDOW - n_new - b * n_new) % LANES
        ko_ref[b] = jnp.where(keep_old, pltpu.roll(k_old, WINDOW - n_new, 1), pltpu.roll(k_new, shift_new, 1))
        vo_ref[b] = jnp.where(keep_old, pltpu.roll(v_old, WINDOW - n_new, 1), pltpu.roll(v_new, shift_new, 1))
    yb_ref[...] = yb_acc[...].astype(BF16)


def _sample_mix(sinks, u, va32, q, kvt, cache_k, cache_v, wexp, bsexp, bias_c, bias_n, prev, layer):
    n_seq = cache_k.shape[1]
    rows = SAMPLE_SEQS * SUBLANES
    row_spec = lambda width: pl.BlockSpec((rows, width), lambda i: (i, 0))
    cache_spec = pl.BlockSpec((None, SAMPLE_SEQS, KV_WIDTH, WINDOW), lambda i: (layer, i, 0, 0))
    table_spec = pl.BlockSpec((N_HEADS * SUBLANES, LANES), lambda i: (0, 0))
    in_specs = [pl.BlockSpec(memory_space=pltpu.SMEM),
                row_spec(A_WIDTH), row_spec(A_WIDTH), row_spec(Q_WIDTH),
                pl.BlockSpec((2 * KV_WIDTH, rows), lambda i: (0, i)),
                cache_spec, cache_spec,
                pl.BlockSpec((None, SUBLANES, SUBLANES, A_WIDTH), lambda i: (layer, 0, 0, 0)),
                pl.BlockSpec((None, SUBLANES, A_WIDTH), lambda i: (layer, 0, 0)),
                table_spec, table_spec]
    operands = [sinks, u, va32, q, kvt, cache_k, cache_v, wexp, bsexp, bias_c, bias_n]
    aliases = {}
    if prev is not None:
        in_specs += [pl.BlockSpec(memory_space=pl.ANY)] * 2
        aliases = {len(operands): 2, len(operands) + 1: 3}
        operands += list(prev)
    return pl.pallas_call(
        functools.partial(_sample_mix_kernel, aliased=prev is not None),
        out_shape=[jax.ShapeDtypeStruct((n_seq * SUBLANES, A_WIDTH), BF16),
                   jax.ShapeDtypeStruct((n_seq * SUBLANES, Q_WIDTH), BF16),
                   jax.ShapeDtypeStruct(cache_k.shape, F32),
                   jax.ShapeDtypeStruct(cache_v.shape, F32)],
        grid=(n_seq // SAMPLE_SEQS,),
        in_specs=in_specs,
        out_specs=[row_spec(A_WIDTH), row_spec(Q_WIDTH), cache_spec, cache_spec],
        scratch_shapes=[pltpu.VMEM((rows, Q_WIDTH), F32)],
        input_output_aliases=aliases,
        compiler_params=_params(1),
        name="sample_mix",
    )(*operands)


def _merge_ffn_kernel(x_ref, ya_ref, yb_ref, gate_ref, wpa_ref, wpb_ref, wo_ref, wg_ref, wu_ref, wd_ref,
                      ln_ref, out_ref, *, alpha):
    g_a = gate_ref[:, 0:D_MODEL].astype(F32)
    g_b = gate_ref[:, D_MODEL:2 * D_MODEL].astype(F32)
    merged = g_a * _dot(ya_ref[...], wpa_ref[...]) + g_b * _dot(yb_ref[...], wpb_ref[...])
    mix = _dot(merged.astype(BF16), wo_ref[...])
    x1 = _layer_norm(alpha * x_ref[...] + mix, ln_ref[0:1, :], ln_ref[1:2, :])
    x1b = x1.astype(BF16)
    act = (jax.nn.silu(_dot(x1b, wg_ref[...])) * _dot(x1b, wu_ref[...])).astype(BF16)
    ffn = _dot(act, wd_ref[...])
    out_ref[...] = _layer_norm(alpha * x1 + ffn, ln_ref[2:3, :], ln_ref[3:4, :])


def _merge_ffn(x, ya, yb, gates, w_pa, w_pb, w_o, w_gate, w_up, w_down, ln_pack, layer, alpha):
    rows = x.shape[0]
    row_spec = lambda width: pl.BlockSpec((ROW_TILE, width), lambda i: (i, 0))
    weight = lambda k, n: _resident((None, k, n), lambda i: (layer, 0, 0))
    return pl.pallas_call(
        functools.partial(_merge_ffn_kernel, alpha=alpha),
        out_shape=jax.ShapeDtypeStruct((rows, D_MODEL), F32),
        grid=(rows // ROW_TILE,),
        in_specs=[row_spec(D_MODEL), row_spec(A_WIDTH), row_spec(Q_WIDTH), row_spec(2 * D_MODEL),
                  weight(A_WIDTH, D_MODEL), weight(Q_WIDTH, D_MODEL), weight(D_MODEL, D_MODEL),
                  weight(D_MODEL, D_FF), weight(D_MODEL, D_FF), weight(D_FF, D_MODEL),
                  pl.BlockSpec((None, 4, D_MODEL), lambda i: (layer, 0, 0))],
        out_specs=row_spec(D_MODEL),
        compiler_params=_params(1),
        name="merge_ffn",
    )(x, ya, yb, gates, w_pa, w_pb, w_o, w_gate, w_up, w_down, ln_pack)


def kernel(x_prompt, x_sample, cache_swa_k, cache_swa_v, rel_bias, w_in, ln_v_g, ln_v_b, w_s, b_s,
           sinks, w_pa, w_pb, w_o, ln1_g, ln1_b, w_gate, w_up, w_down, ln2_g, ln2_b):
    depth = w_in.shape[0]
    batch, seq, _ = x_prompt.shape
    n_seq, n_new, _ = x_sample.shape
    assert n_new == SUBLANES and seq % (MIX_CHUNKS * CHUNK) == 0 and n_seq % SAMPLE_SEQS == 0
    assert (batch * seq) % ROW_TILE == 0 and (n_seq * n_new) % ROW_TILE == 0
    alpha = (2 * depth) ** 0.25

    def heads_rg(w, axis):
        shape = w.shape
        w = w.reshape(shape[:axis] + (N_KV_HEADS, Q_PER_KV, HEAD_DIM) + shape[axis + 1:])
        return jnp.swapaxes(w, axis, axis + 1).reshape(shape)

    w_in_b = w_in.astype(BF16)
    w_q_b = heads_rg(w_in[..., O_Q:O_K], 2).astype(BF16)
    w_kvt_b = jnp.swapaxes(w_in[..., O_K:O_G], 1, 2).astype(BF16)
    w_pa_b, w_o_b = w_pa.astype(BF16), w_o.astype(BF16)
    w_pb_b = heads_rg(w_pb, 1).astype(BF16)
    w_gate_b, w_up_b, w_down_b = w_gate.astype(BF16), w_up.astype(BF16), w_down.astype(BF16)
    ln_v_g3, ln_v_b3 = ln_v_g[:, None, :], ln_v_b[:, None, :]
    ln_pack = jnp.stack([ln1_g, ln1_b, ln2_g, ln2_b], axis=1)
    b_s_t = jnp.swapaxes(b_s, 1, 2)
    wexp = jnp.repeat(jnp.transpose(w_s[:, :, :n_new, :n_new], (0, 3, 2, 1)), A_WIDTH // A_GROUPS, axis=-1)
    bsexp = jnp.repeat(jnp.swapaxes(b_s[:, :, :n_new], 1, 2), A_WIDTH // A_GROUPS, axis=-1)

    bias_p = _bias_tables(rel_bias, jnp.stack([_masked_buckets(WINDOW, KEY_PAD, False),
                                               _masked_buckets(WINDOW, KEY_PAD, True)]))
    bias_s = _bias_tables(rel_bias, _masked_buckets(n_new, WINDOW + n_new, False)[None])[0]
    bias_s = bias_s.reshape(N_KV_HEADS, Q_PER_KV, n_new, KEY_PAD)
    bias_s = jnp.swapaxes(bias_s, 0, 1).reshape(N_HEADS * n_new, KEY_PAD)
    bias_c = bias_s[:, :WINDOW]
    bias_n = jnp.tile(bias_s[:, WINDOW:WINDOW + n_new], (1, SAMPLE_SEQS))

    cache_k = jnp.transpose(cache_swa_k, (0, 1, 3, 4, 2)).reshape(depth, n_seq, KV_WIDTH, WINDOW)
    cache_v = jnp.transpose(cache_swa_v, (0, 1, 3, 4, 2)).reshape(depth, n_seq, KV_WIDTH, WINDOW)

    xp = x_prompt.reshape(batch * seq, D_MODEL)
    xs = x_sample.reshape(n_seq * n_new, D_MODEL)
    kp_l, vp_l, ga_l, new_cache = [], [], [], None
    for l in range(depth):
        u, va, q, gates, kv = _inproj(xp, w_in_b, w_q_b, None, ln_v_g3, ln_v_b3, l, False)
        ya, yb = _prompt_mix(sinks[l], u, va, q, kv, w_s, b_s_t, bias_p, l, batch, seq)
        xp = _merge_ffn(xp, ya, yb, gates, w_pa_b, w_pb_b, w_o_b, w_gate_b, w_up_b, w_down_b,
                        ln_pack, l, alpha)
        kv_tail = kv.reshape(batch, seq, 2 * KV_WIDTH)[:, seq - WINDOW:, :]
        kp_l.append(kv_tail[..., :KV_WIDTH].reshape(batch, WINDOW, N_KV_HEADS, HEAD_DIM))
        vp_l.append(kv_tail[..., KV_WIDTH:].reshape(batch, WINDOW, N_KV_HEADS, HEAD_DIM))

        u, va32, q, gates, kvt = _inproj(xs, w_in_b, w_q_b, w_kvt_b, ln_v_g3, ln_v_b3, l, True)
        ya, yb, *new_cache = _sample_mix(sinks[l], u, va32, q, kvt, cache_k, cache_v, wexp, bsexp,
                                         bias_c, bias_n, new_cache, l)
        xs = _merge_ffn(xs, ya, yb, gates, w_pa_b, w_pb_b, w_o_b, w_gate_b, w_up_b, w_down_b,
                        ln_pack, l, alpha)
        ga_l.append(va32.reshape(n_seq, n_new, A_WIDTH))

    def window_major(c):
        c = c.reshape(depth, n_seq, N_KV_HEADS, HEAD_DIM, WINDOW)
        return jnp.transpose(c, (0, 1, 4, 2, 3))

    return (xp.reshape(batch, seq, D_MODEL), xs.reshape(n_seq, n_new, D_MODEL),
            jnp.stack(kp_l), jnp.stack(vp_l), window_major(new_cache[0]), window_major(new_cache[1]),
            jnp.stack(ga_l))
```

```python
import functools
import math

import jax
import jax.numpy as jnp
from jax import lax
from jax.experimental import pallas as pl
from jax.experimental.pallas import tpu as pltpu

D_MODEL = 1024
CHUNK = 128
A_WIDTH = D_MODEL
A_GROUPS = 8
N_HEADS = 16
HEAD_DIM = 64
N_KV_HEADS = 4
Q_PER_KV = N_HEADS // N_KV_HEADS
WINDOW = 128
N_BUCKETS = 32
MAX_DISTANCE = 128
D_FF = 2816
LN_EPS = 1e-5
NEG_INF = -1e30

KV_WIDTH = N_KV_HEADS * HEAD_DIM
Q_WIDTH = N_HEADS * HEAD_DIM
O_U = 0
O_V = O_U + A_WIDTH
O_Q = O_V + A_WIDTH
O_K = O_Q + Q_WIDTH
O_G = O_K + 2 * KV_WIDTH
IN_WIDTH = O_G + 2 * D_MODEL

LANES = 128
SUBLANES = 8
ROW_TILE = 512
PIECE = 512
SAMPLE_SEQS = LANES // SUBLANES
KEY_PAD = 2 * WINDOW
VMEM_LIMIT = 56 * 1024 * 1024

BF16 = jnp.bfloat16
F32 = jnp.float32


def _layer_norm(x, g, b):
    mu = jnp.mean(x, axis=-1, keepdims=True)
    xc = x - mu
    var = jnp.mean(xc * xc, axis=-1, keepdims=True)
    return xc * lax.rsqrt(var + LN_EPS) * g + b


def _dot(a, b):
    return jnp.dot(a, b, preferred_element_type=F32)


def _dot_nt(a, b):
    return lax.dot_general(a, b, (((1,), (1,)), ((), ())), preferred_element_type=F32)


def _resident(block_shape, index_map):
    return pl.BlockSpec(block_shape, index_map, pipeline_mode=pl.Buffered(1))


def _params(n_axes):
    return pltpu.CompilerParams(dimension_semantics=("arbitrary",) * n_axes,
                                vmem_limit_bytes=VMEM_LIMIT)


def _bias_kernel(rb_ref, bucket_ref, out_ref):
    bk = bucket_ref[...]
    for h in range(N_HEADS):
        acc = jnp.full(bk.shape, NEG_INF, F32)
        for b in range(N_BUCKETS):
            acc = jnp.where(bk == b, rb_ref[b, h], acc)
        out_ref[h] = acc


def _bias_tables(rel_bias, buckets):
    n, t, kp = buckets.shape
    return pl.pallas_call(
        _bias_kernel,
        out_shape=jax.ShapeDtypeStruct((n, N_HEADS, t, kp), F32),
        grid=(n,),
        in_specs=[pl.BlockSpec(memory_space=pltpu.SMEM),
                  pl.BlockSpec((None, t, kp), lambda i: (i, 0, 0))],
        out_specs=pl.BlockSpec((None, N_HEADS, t, kp), lambda i: (i, 0, 0, 0)),
        compiler_params=_params(1),
        name="bias_tables",
    )(rel_bias, buckets)


def _rel_bucket(dist):
    n = jnp.maximum(dist, 0)
    max_exact = N_BUCKETS // 2
    nf = jnp.maximum(n, 1).astype(F32)
    large = max_exact + (jnp.log(nf / max_exact) / math.log(MAX_DISTANCE / max_exact)
                         * (N_BUCKETS - max_exact)).astype(jnp.int32)
    large = jnp.minimum(large, N_BUCKETS - 1)
    return jnp.where(n < max_exact, n, large)


def _masked_buckets(n_q, n_keys, first_block):
    qi = jnp.arange(n_q, dtype=jnp.int32)[:, None]
    kj = jnp.arange(KEY_PAD, dtype=jnp.int32)[None, :]
    dist = qi + WINDOW - kj
    ok = (dist >= 0) & (dist < WINDOW) & (kj < n_keys)
    if first_block:
        ok = ok & (kj >= WINDOW)
    return jnp.where(ok, _rel_bucket(dist), -1)


def _softmax_sink_pv(s, sink, v):
    m = jnp.maximum(jnp.max(s, axis=-1, keepdims=True), sink)
    p = jnp.exp(s - m)
    denom = jnp.sum(p, axis=-1, keepdims=True) + jnp.exp(sink - m)
    return _dot(p.astype(BF16), v) * (1.0 / denom)


def _prompt_front_kernel(sinks_ref, x_ref, w_ref, wq_ref, g_ref, b_ref, ws_ref, bs_ref, bias_ref,
                         ya_ref, yb_ref, gate_ref, tail_ref, u_s, va_s, q_s, kv_s, hv_s):
    n_chunks = ROW_TILE // CHUNK
    first_tile = pl.program_id(1) == 0

    @pl.when(jnp.logical_and(pl.program_id(0) == 0, first_tile))
    def _():
        kv_s[0:CHUNK, :] = jnp.zeros((CHUNK, 2 * KV_WIDTH), BF16)

    xb = x_ref[...].astype(BF16)
    tri = (lax.broadcasted_iota(jnp.int32, (CHUNK, CHUNK), 0)
           >= lax.broadcasted_iota(jnp.int32, (CHUNK, CHUNK), 1))
    low_half = lax.broadcasted_iota(jnp.int32, (1, LANES), 1) < HEAD_DIM
    zero = jnp.zeros((), BF16)

    def piece_kv(j):
        cols = slice(j * PIECE, (j + 1) * PIECE)
        kv = _dot(xb, w_ref[:, O_K + j * PIECE:O_K + (j + 1) * PIECE])
        kv_s[CHUNK:CHUNK + ROW_TILE, cols] = kv.astype(BF16)
        tail_ref[:, cols] = kv[ROW_TILE - WINDOW:, :]

    def piece_q(j):
        cols = slice(j * PIECE, (j + 1) * PIECE)
        q_s[:, cols] = (_dot(xb, wq_ref[:, cols]) * (HEAD_DIM ** -0.5)).astype(BF16)

    def piece_v(j):
        cols = slice(j * PIECE, (j + 1) * PIECE)
        hv_s[:, cols] = jax.nn.gelu(_dot(xb, w_ref[:, O_V + j * PIECE:O_V + (j + 1) * PIECE]))

    def piece_v_norm():
        va_s[...] = _layer_norm(hv_s[...], g_ref[...], b_ref[...]).astype(BF16)

    def piece_u(j):
        cols = slice(j * PIECE, (j + 1) * PIECE)
        u_s[:, cols] = jax.nn.gelu(_dot(xb, w_ref[:, O_U + j * PIECE:O_U + (j + 1) * PIECE])).astype(BF16)

    def piece_gate(j):
        cols = slice(j * PIECE, (j + 1) * PIECE)
        gate_ref[:, cols] = jax.nn.sigmoid(
            _dot(xb, w_ref[:, O_G + j * PIECE:O_G + (j + 1) * PIECE])).astype(BF16)

    def unit_spatial(c, g):
        rows, cols = slice(c * CHUNK, (c + 1) * CHUNK), slice(g * LANES, (g + 1) * LANES)
        w = jnp.where(tri, ws_ref[g], 0.0).astype(BF16)
        mixed = _dot(w, va_s[rows, cols]) + bs_ref[:, g:g + 1]
        ya_ref[rows, cols] = (u_s[rows, cols].astype(F32) * mixed).astype(BF16)

    def unit_attention(c, pair):
        rows = slice(c * CHUNK, (c + 1) * CHUNK)
        band = slice(c * CHUNK, (c + 2) * CHUNK)
        table = jnp.where(first_tile, 1, 0) if c == 0 else 0
        r, gp = pair // 2, pair % 2
        qp = q_s[rows, pair * LANES:(pair + 1) * LANES]
        kn = kv_s[band, gp * LANES:(gp + 1) * LANES]
        vn = kv_s[band, KV_WIDTH + gp * LANES:KV_WIDTH + (gp + 1) * LANES]
        out = None
        for hi in range(2):
            keep = low_half if hi == 0 else jnp.logical_not(low_half)
            head = Q_PER_KV * (2 * gp + hi) + r
            s = _dot_nt(qp, jnp.where(keep, kn, zero)) + bias_ref[table, head]
            o = _softmax_sink_pv(s, sinks_ref[head], jnp.where(keep, vn, zero))
            out = o if out is None else jnp.where(low_half, out, o)
        yb_ref[rows, pair * LANES:(pair + 1) * LANES] = out.astype(BF16)

    for j in range(2 * KV_WIDTH // PIECE):
        piece_kv(j)
    for j in range(Q_WIDTH // PIECE):
        piece_q(j)
    for j in range(A_WIDTH // PIECE):
        piece_v(j)
    piece_v_norm()
    for j in range(A_WIDTH // PIECE):
        piece_u(j)
    for j in range(2 * D_MODEL // PIECE):
        piece_gate(j)
    for c in range(n_chunks):
        for g in range(A_GROUPS):
            unit_spatial(c, g)
        for pair in range(N_HEADS // 2):
            unit_attention(c, pair)

    kv_s[0:CHUNK, :] = kv_s[ROW_TILE:ROW_TILE + CHUNK, :]


def _prompt_front(sinks, x, w_in, w_q, ln_g, ln_b, w_s, b_s_t, bias, layer, batch, seq):
    n_tiles = seq // ROW_TILE
    tok_spec = lambda width: pl.BlockSpec((ROW_TILE, width), lambda b, i: (b * n_tiles + i, 0))
    vec_spec = pl.BlockSpec((None, 1, A_WIDTH), lambda b, i: (layer, 0, 0))
    rows = batch * seq
    return pl.pallas_call(
        _prompt_front_kernel,
        out_shape=[jax.ShapeDtypeStruct((rows, A_WIDTH), BF16),
                   jax.ShapeDtypeStruct((rows, Q_WIDTH), BF16),
                   jax.ShapeDtypeStruct((rows, 2 * D_MODEL), BF16),
                   jax.ShapeDtypeStruct((batch, WINDOW, 2 * KV_WIDTH), F32)],
        grid=(batch, n_tiles),
        in_specs=[pl.BlockSpec(memory_space=pltpu.SMEM),
                  tok_spec(D_MODEL),
                  _resident((None, D_MODEL, IN_WIDTH), lambda b, i: (layer, 0, 0)),
                  _resident((None, D_MODEL, Q_WIDTH), lambda b, i: (layer, 0, 0)),
                  vec_spec, vec_spec,
                  _resident((None, A_GROUPS, CHUNK, CHUNK), lambda b, i: (layer, 0, 0, 0)),
                  _resident((None, CHUNK, A_GROUPS), lambda b, i: (layer, 0, 0)),
                  _resident((2, N_HEADS, WINDOW, KEY_PAD), lambda b, i: (0, 0, 0, 0))],
        out_specs=[tok_spec(A_WIDTH), tok_spec(Q_WIDTH), tok_spec(2 * D_MODEL),
                   pl.BlockSpec((None, WINDOW, 2 * KV_WIDTH), lambda b, i: (b, 0, 0))],
        scratch_shapes=[pltpu.VMEM((ROW_TILE, A_WIDTH), BF16),
                        pltpu.VMEM((ROW_TILE, A_WIDTH), BF16),
                        pltpu.VMEM((ROW_TILE, Q_WIDTH), BF16),
                        pltpu.VMEM((CHUNK + ROW_TILE, 2 * KV_WIDTH), BF16),
                        pltpu.VMEM((ROW_TILE, A_WIDTH), F32)],
        compiler_params=_params(2),
        name="prompt_front",
    )(sinks, x, w_in, w_q, ln_g, ln_b, w_s, b_s_t, bias)


def _inproj_sample_kernel(x_ref, w_ref, wq_ref, wkvt_ref, g_ref, b_ref, u_ref, va_ref, q_ref, gate_ref,
                          kvt_ref):
    xb = x_ref[...].astype(BF16)

    def proj(c0, c1):
        return _dot(xb, w_ref[:, c0:c1])

    half = A_WIDTH // 2
    for c0 in range(O_U, O_V, half):
        u_ref[:, c0:c0 + half] = jax.nn.gelu(proj(c0, c0 + half)).astype(BF16)
    va_ref[...] = _layer_norm(jax.nn.gelu(proj(O_V, O_Q)), g_ref[...], b_ref[...])
    for c0 in range(0, Q_WIDTH, half):
        q_ref[:, c0:c0 + half] = (_dot(xb, wq_ref[:, c0:c0 + half]) * (HEAD_DIM ** -0.5)).astype(BF16)
    kvt_ref[...] = _dot_nt(wkvt_ref[...], xb)
    for c0 in range(0, 2 * D_MODEL, half):
        gate_ref[:, c0:c0 + half] = jax.nn.sigmoid(proj(O_G + c0, O_G + c0 + half)).astype(BF16)


def _inproj_sample(x, w_in, w_q, w_kvt, ln_g, ln_b, layer):
    rows = x.shape[0]
    row_spec = lambda width: pl.BlockSpec((ROW_TILE, width), lambda i: (i, 0))
    vec_spec = pl.BlockSpec((None, 1, A_WIDTH), lambda i: (layer, 0, 0))
    return pl.pallas_call(
        _inproj_sample_kernel,
        out_shape=[jax.ShapeDtypeStruct((rows, A_WIDTH), BF16),
                   jax.ShapeDtypeStruct((rows, A_WIDTH), F32),
                   jax.ShapeDtypeStruct((rows, Q_WIDTH), BF16),
                   jax.ShapeDtypeStruct((rows, 2 * D_MODEL), BF16),
                   jax.ShapeDtypeStruct((2 * KV_WIDTH, rows), F32)],
        grid=(rows // ROW_TILE,),
        in_specs=[row_spec(D_MODEL),
                  _resident((None, D_MODEL, IN_WIDTH), lambda i: (layer, 0, 0)),
                  _resident((None, D_MODEL, Q_WIDTH), lambda i: (layer, 0, 0)),
                  _resident((None, 2 * KV_WIDTH, D_MODEL), lambda i: (layer, 0, 0)),
                  vec_spec, vec_spec],
        out_specs=[row_spec(A_WIDTH), row_spec(A_WIDTH), row_spec(Q_WIDTH), row_spec(2 * D_MODEL),
                   pl.BlockSpec((2 * KV_WIDTH, ROW_TILE), lambda i: (0, i))],
        compiler_params=_params(1),
        name="inproj_sample",
    )(x, w_in, w_q, w_kvt, ln_g, ln_b)


def _sample_mix_kernel(*refs, aliased):
    (sinks_ref, u_ref, va_ref, q_ref, kvt_ref, ck_ref, cv_ref, wexp_ref, bsexp_ref, biasc_ref,
     biasn_ref) = refs[:11]
    ya_ref, yb_ref, ko_ref, vo_ref, yb_acc = refs[11 + (2 if aliased else 0):]
    n_new = SUBLANES
    rows = SAMPLE_SEQS * n_new

    va3 = va_ref[...].reshape(SAMPLE_SEQS, n_new, A_WIDTH)
    t_idx = lax.broadcasted_iota(jnp.int32, (n_new, A_WIDTH), 0)
    mixed = jnp.broadcast_to(bsexp_ref[...][None], va3.shape)
    for s in range(n_new):
        w = jnp.where(t_idx >= s, wexp_ref[s], 0.0)
        mixed = mixed + w[None] * va3[:, s:s + 1, :]
    u3 = u_ref[...].astype(F32).reshape(SAMPLE_SEQS, n_new, A_WIDTH)
    ya_ref[...] = (u3 * mixed).reshape(rows, A_WIDTH).astype(BF16)

    group_of_lane = lax.broadcasted_iota(jnp.int32, (1, KV_WIDTH), 1) // HEAD_DIM
    row = lax.broadcasted_iota(jnp.int32, (N_HEADS * n_new, 1), 0)
    head_of_row = Q_PER_KV * ((row // n_new) % N_KV_HEADS) + row // (n_new * N_KV_HEADS)
    sink = jnp.zeros((N_HEADS * n_new, 1), F32)
    for h in range(N_HEADS):
        sink = jnp.where(head_of_row == h, sinks_ref[h], sink)
    lane = lax.broadcasted_iota(jnp.int32, (1, LANES), 1)
    seq_of_lane = lane // n_new
    keep_old = lane < WINDOW - n_new
    q32 = q_ref[...].astype(F32)
    k_new, v_new = kvt_ref[0:KV_WIDTH, :], kvt_ref[KV_WIDTH:2 * KV_WIDTH, :]
    k_new_b, v_new_b = k_new.astype(BF16), v_new.astype(BF16)
    bias_c, bias_n = biasc_ref[...], biasn_ref[...]
    for b in range(SAMPLE_SEQS):
        new = slice(b * n_new, (b + 1) * n_new)
        pieces = []
        for r in range(Q_PER_KV):
            blk = q32[new, r * KV_WIDTH:(r + 1) * KV_WIDTH]
            for g in range(N_KV_HEADS):
                pieces.append(jnp.where(group_of_lane == g, blk, 0.0))
        q_rows = jnp.concatenate(pieces, axis=0).astype(BF16)
        k_old, v_old = ck_ref[b], cv_ref[b]
        k_all = jnp.concatenate([k_old.astype(BF16), k_new_b], axis=1)
        v_all = jnp.concatenate([v_old.astype(BF16), v_new_b], axis=1)
        bias = jnp.concatenate([bias_c, jnp.where(seq_of_lane == b, bias_n, NEG_INF)], axis=1)
        s = _dot(q_rows, k_all) + bias
        m = jnp.maximum(jnp.max(s, axis=-1, keepdims=True), sink)
        p = jnp.exp(s - m)
        denom = jnp.sum(p, axis=-1, keepdims=True) + jnp.exp(sink - m)
        o = _dot_nt(p.astype(BF16), v_all) * (1.0 / denom)
        for r in range(Q_PER_KV):
            acc = jnp.zeros((n_new, KV_WIDTH), F32)
            for g in range(N_KV_HEADS):
                r0 = (r * N_KV_HEADS + g) * n_new
                acc = jnp.where(group_of_lane == g, o[r0:r0 + n_new, :], acc)
            yb_acc[new, r * KV_WIDTH:(r + 1) * KV_WIDTH] = acc
        shift_new = (WINDOW - n_new - b * n_new) % LANES
        ko_ref[b] = jnp.where(keep_old, pltpu.roll(k_old, WINDOW - n_new, 1), pltpu.roll(k_new, shift_new, 1))
        vo_ref[b] = jnp.where(keep_old, pltpu.roll(v_old, WINDOW - n_new, 1), pltpu.roll(v_new, shift_new, 1))
    yb_ref[...] = yb_acc[...].astype(BF16)


def _sample_mix(sinks, u, va32, q, kvt, cache_k, cache_v, wexp, bsexp, bias_c, bias_n, prev, layer):
    n_seq = cache_k.shape[1]
    rows = SAMPLE_SEQS * SUBLANES
    row_spec = lambda width: pl.BlockSpec((rows, width), lambda i: (i, 0))
    cache_spec = pl.BlockSpec((None, SAMPLE_SEQS, KV_WIDTH, WINDOW), lambda i: (layer, i, 0, 0))
    table_spec = pl.BlockSpec((N_HEADS * SUBLANES, LANES), lambda i: (0, 0))
    in_specs = [pl.BlockSpec(memory_space=pltpu.SMEM),
                row_spec(A_WIDTH), row_spec(A_WIDTH), row_spec(Q_WIDTH),
                pl.BlockSpec((2 * KV_WIDTH, rows), lambda i: (0, i)),
                cache_spec, cache_spec,
                pl.BlockSpec((None, SUBLANES, SUBLANES, A_WIDTH), lambda i: (layer, 0, 0, 0)),
                pl.BlockSpec((None, SUBLANES, A_WIDTH), lambda i: (layer, 0, 0)),
                table_spec, table_spec]
    operands = [sinks, u, va32, q, kvt, cache_k, cache_v, wexp, bsexp, bias_c, bias_n]
    aliases = {}
    if prev is not None:
        in_specs += [pl.BlockSpec(memory_space=pl.ANY)] * 2
        aliases = {len(operands): 2, len(operands) + 1: 3}
        operands += list(prev)
    return pl.pallas_call(
        functools.partial(_sample_mix_kernel, aliased=prev is not None),
        out_shape=[jax.ShapeDtypeStruct((n_seq * SUBLANES, A_WIDTH), BF16),
                   jax.ShapeDtypeStruct((n_seq * SUBLANES, Q_WIDTH), BF16),
                   jax.ShapeDtypeStruct(cache_k.shape, F32),
                   jax.ShapeDtypeStruct(cache_v.shape, F32)],
        grid=(n_seq // SAMPLE_SEQS,),
        in_specs=in_specs,
        out_specs=[row_spec(A_WIDTH), row_spec(Q_WIDTH), cache_spec, cache_spec],
        scratch_shapes=[pltpu.VMEM((rows, Q_WIDTH), F32)],
        input_output_aliases=aliases,
        compiler_params=_params(1),
        name="sample_mix",
    )(*operands)


def _merge_ffn_kernel(x_ref, ya_ref, yb_ref, gate_ref, wpa_ref, wpb_ref, wo_ref, wg_ref, wu_ref, wd_ref,
                      ln_ref, out_ref, *, alpha):
    g_a = gate_ref[:, 0:D_MODEL].astype(F32)
    g_b = gate_ref[:, D_MODEL:2 * D_MODEL].astype(F32)
    merged = g_a * _dot(ya_ref[...], wpa_ref[...]) + g_b * _dot(yb_ref[...], wpb_ref[...])
    mix = _dot(merged.astype(BF16), wo_ref[...])
    x1 = _layer_norm(alpha * x_ref[...] + mix, ln_ref[0:1, :], ln_ref[1:2, :])
    x1b = x1.astype(BF16)
    act = (jax.nn.silu(_dot(x1b, wg_ref[...])) * _dot(x1b, wu_ref[...])).astype(BF16)
    ffn = _dot(act, wd_ref[...])
    out_ref[...] = _layer_norm(alpha * x1 + ffn, ln_ref[2:3, :], ln_ref[3:4, :])


def _merge_ffn(x, ya, yb, gates, w_pa, w_pb, w_o, w_gate, w_up, w_down, ln_pack, layer, alpha):
    rows = x.shape[0]
    row_spec = lambda width: pl.BlockSpec((ROW_TILE, width), lambda i: (i, 0))
    weight = lambda k, n: _resident((None, k, n), lambda i: (layer, 0, 0))
    return pl.pallas_call(
        functools.partial(_merge_ffn_kernel, alpha=alpha),
        out_shape=jax.ShapeDtypeStruct((rows, D_MODEL), F32),
        grid=(rows // ROW_TILE,),
        in_specs=[row_spec(D_MODEL), row_spec(A_WIDTH), row_spec(Q_WIDTH), row_spec(2 * D_MODEL),
                  weight(A_WIDTH, D_MODEL), weight(Q_WIDTH, D_MODEL), weight(D_MODEL, D_MODEL),
                  weight(D_MODEL, D_FF), weight(D_MODEL, D_FF), weight(D_FF, D_MODEL),
                  pl.BlockSpec((None, 4, D_MODEL), lambda i: (layer, 0, 0))],
        out_specs=row_spec(D_MODEL),
        compiler_params=_params(1),
        name="merge_ffn",
    )(x, ya, yb, gates, w_pa, w_pb, w_o, w_gate, w_up, w_down, ln_pack)


def kernel(x_prompt, x_sample, cache_swa_k, cache_swa_v, rel_bias, w_in, ln_v_g, ln_v_b, w_s, b_s,
           sinks, w_pa, w_pb, w_o, ln1_g, ln1_b, w_gate, w_up, w_down, ln2_g, ln2_b):
    depth = w_in.shape[0]
    batch, seq, _ = x_prompt.shape
    n_seq, n_new, _ = x_sample.shape
    assert n_new == SUBLANES and seq % ROW_TILE == 0 and n_seq % SAMPLE_SEQS == 0
    assert (n_seq * n_new) % ROW_TILE == 0
    alpha = (2 * depth) ** 0.25

    def heads_rg(w, axis):
        shape = w.shape
        w = w.reshape(shape[:axis] + (N_KV_HEADS, Q_PER_KV, HEAD_DIM) + shape[axis + 1:])
        return jnp.swapaxes(w, axis, axis + 1).reshape(shape)

    w_in_b = lax.optimization_barrier(w_in.astype(BF16))
    w_q_b = heads_rg(w_in_b[..., O_Q:O_K], 2)
    w_kvt_b = jnp.swapaxes(w_in_b[..., O_K:O_G], 1, 2)
    w_pa_b, w_o_b = w_pa.astype(BF16), w_o.astype(BF16)
    w_pb_b = heads_rg(w_pb, 1).astype(BF16)
    w_gate_b, w_up_b, w_down_b = w_gate.astype(BF16), w_up.astype(BF16), w_down.astype(BF16)
    ln_v_g3, ln_v_b3 = ln_v_g[:, None, :], ln_v_b[:, None, :]
    ln_pack = jnp.stack([ln1_g, ln1_b, ln2_g, ln2_b], axis=1)
    b_s_t = jnp.swapaxes(b_s, 1, 2)
    wexp = jnp.repeat(jnp.transpose(w_s[:, :, :n_new, :n_new], (0, 3, 2, 1)), A_WIDTH // A_GROUPS, axis=-1)
    bsexp = jnp.repeat(jnp.swapaxes(b_s[:, :, :n_new], 1, 2), A_WIDTH // A_GROUPS, axis=-1)

    bias_p = _bias_tables(rel_bias, jnp.stack([_masked_buckets(WINDOW, KEY_PAD, False),
                                               _masked_buckets(WINDOW, KEY_PAD, True)]))
    bias_s = _bias_tables(rel_bias, _masked_buckets(n_new, WINDOW + n_new, False)[None])[0]
    bias_s = bias_s.reshape(N_KV_HEADS, Q_PER_KV, n_new, KEY_PAD)
    bias_s = jnp.swapaxes(bias_s, 0, 1).reshape(N_HEADS * n_new, KEY_PAD)
    bias_c = bias_s[:, :WINDOW]
    bias_n = jnp.tile(bias_s[:, WINDOW:WINDOW + n_new], (1, SAMPLE_SEQS))

    cache_k = jnp.transpose(cache_swa_k, (0, 1, 3, 4, 2)).reshape(depth, n_seq, KV_WIDTH, WINDOW)
    cache_v = jnp.transpose(cache_swa_v, (0, 1, 3, 4, 2)).reshape(depth, n_seq, KV_WIDTH, WINDOW)

    xp = x_prompt.reshape(batch * seq, D_MODEL)
    xs = x_sample.reshape(n_seq * n_new, D_MODEL)
    kp_l, vp_l, ga_l, new_cache = [], [], [], None
    for l in range(depth):
        ya, yb, gates, kv_tail = _prompt_front(sinks[l], xp, w_in_b, w_q_b, ln_v_g3, ln_v_b3, w_s, b_s_t,
                                               bias_p, l, batch, seq)
        xp = _merge_ffn(xp, ya, yb, gates, w_pa_b, w_pb_b, w_o_b, w_gate_b, w_up_b, w_down_b,
                        ln_pack, l, alpha)
        kp_l.append(kv_tail[..., :KV_WIDTH].reshape(batch, WINDOW, N_KV_HEADS, HEAD_DIM))
        vp_l.append(kv_tail[..., KV_WIDTH:].reshape(batch, WINDOW, N_KV_HEADS, HEAD_DIM))

        u, va32, q, gates, kvt = _inproj_sample(xs, w_in_b, w_q_b, w_kvt_b, ln_v_g3, ln_v_b3, l)
        ya, yb, *new_cache = _sample_mix(sinks[l], u, va32, q, kvt, cache_k, cache_v, wexp, bsexp,
                                         bias_c, bias_n, new_cache, l)
        xs = _merge_ffn(xs, ya, yb, gates, w_pa_b, w_pb_b, w_o_b, w_gate_b, w_up_b, w_down_b,
                        ln_pack, l, alpha)
        ga_l.append(va32.reshape(n_seq, n_new, A_WIDTH))

    def window_major(c):
        c = c.reshape(depth, n_seq, N_KV_HEADS, HEAD_DIM, WINDOW)
        return jnp.transpose(c, (0, 1, 4, 2, 3))

    return (xp.reshape(batch, seq, D_MODEL), xs.reshape(n_seq, n_new, D_MODEL),
            jnp.stack(kp_l), jnp.stack(vp_l), window_major(new_cache[0]), window_major(new_cache[1]),
            jnp.stack(ga_l))
```

```python
import functools
import math

import jax
import jax.numpy as jnp
from jax import lax
from jax.experimental import pallas as pl
from jax.experimental.pallas import tpu as pltpu

D_MODEL = 1024
CHUNK = 128
A_WIDTH = D_MODEL
A_GROUPS = 8
N_HEADS = 16
HEAD_DIM = 64
N_KV_HEADS = 4
Q_PER_KV = N_HEADS // N_KV_HEADS
WINDOW = 128
N_BUCKETS = 32
MAX_DISTANCE = 128
D_FF = 2816
LN_EPS = 1e-5
NEG_INF = -1e30

KV_WIDTH = N_KV_HEADS * HEAD_DIM
Q_WIDTH = N_HEADS * HEAD_DIM
O_U = 0
O_V = O_U + A_WIDTH
O_Q = O_V + A_WIDTH
O_K = O_Q + Q_WIDTH
O_G = O_K + 2 * KV_WIDTH
IN_WIDTH = O_G + 2 * D_MODEL

LANES = 128
SUBLANES = 8
BF16_ROWS = 16
ROW_TILE = 512
PIECE = 512
SAMPLE_SEQS = LANES // SUBLANES
KEY_PAD = 2 * WINDOW
VMEM_LIMIT = 56 * 1024 * 1024

BF16 = jnp.bfloat16
F32 = jnp.float32


def _layer_norm(x, g, b):
    mu = jnp.mean(x, axis=-1, keepdims=True)
    xc = x - mu
    var = jnp.mean(xc * xc, axis=-1, keepdims=True)
    return xc * lax.rsqrt(var + LN_EPS) * g + b


def _dot(a, b):
    return jnp.dot(a, b, preferred_element_type=F32)


def _dot_nt(a, b):
    return lax.dot_general(a, b, (((1,), (1,)), ((), ())), preferred_element_type=F32)


def _resident(block_shape, index_map):
    return pl.BlockSpec(block_shape, index_map, pipeline_mode=pl.Buffered(1))


def _params(n_axes):
    return pltpu.CompilerParams(dimension_semantics=("arbitrary",) * n_axes,
                                vmem_limit_bytes=VMEM_LIMIT)


def _bias_kernel(rb_ref, bucket_ref, out_ref):
    bk = bucket_ref[...]
    for h in range(N_HEADS):
        acc = jnp.full(bk.shape, NEG_INF, F32)
        for b in range(N_BUCKETS):
            acc = jnp.where(bk == b, rb_ref[b, h], acc)
        out_ref[h] = acc


def _bias_tables(rel_bias, buckets):
    n, t, kp = buckets.shape
    return pl.pallas_call(
        _bias_kernel,
        out_shape=jax.ShapeDtypeStruct((n, N_HEADS, t, kp), F32),
        grid=(n,),
        in_specs=[pl.BlockSpec(memory_space=pltpu.SMEM),
                  pl.BlockSpec((None, t, kp), lambda i: (i, 0, 0))],
        out_specs=pl.BlockSpec((None, N_HEADS, t, kp), lambda i: (i, 0, 0, 0)),
        compiler_params=_params(1),
        name="bias_tables",
    )(rel_bias, buckets)


def _rel_bucket(dist):
    n = jnp.maximum(dist, 0)
    max_exact = N_BUCKETS // 2
    nf = jnp.maximum(n, 1).astype(F32)
    large = max_exact + (jnp.log(nf / max_exact) / math.log(MAX_DISTANCE / max_exact)
                         * (N_BUCKETS - max_exact)).astype(jnp.int32)
    large = jnp.minimum(large, N_BUCKETS - 1)
    return jnp.where(n < max_exact, n, large)


def _masked_buckets(n_q, n_keys, first_block):
    qi = jnp.arange(n_q, dtype=jnp.int32)[:, None]
    kj = jnp.arange(KEY_PAD, dtype=jnp.int32)[None, :]
    dist = qi + WINDOW - kj
    ok = (dist >= 0) & (dist < WINDOW) & (kj < n_keys)
    if first_block:
        ok = ok & (kj >= WINDOW)
    return jnp.where(ok, _rel_bucket(dist), -1)


def _prompt_front_kernel(sinks_ref, x_ref, w_ref, wq_ref, g_ref, b_ref, ws_ref, bs_ref, bias_ref,
                         ya_ref, yb_ref, gate_ref, tail_ref, u_s, va_s, q_s, kv_s, hv_s, bias_s):
    n_chunks = ROW_TILE // CHUNK
    first_tile = pl.program_id(1) == 0

    lane = lax.broadcasted_iota(jnp.int32, (1, LANES), 1)

    @pl.when(jnp.logical_and(pl.program_id(0) == 0, first_tile))
    def _():
        kv_s[0:CHUNK, :] = jnp.zeros((CHUNK, 2 * KV_WIDTH), BF16)
        for table in range(2):
            for h in range(N_HEADS):
                bias_s[table, h, :, 0:LANES] = jnp.where(lane == 0, sinks_ref[h], bias_ref[table, h, :, 0:LANES])
                bias_s[table, h, :, LANES:] = bias_ref[table, h, :, LANES:]

    xb = x_ref[...].astype(BF16)
    tri = (lax.broadcasted_iota(jnp.int32, (CHUNK, CHUNK), 0)
           >= lax.broadcasted_iota(jnp.int32, (CHUNK, CHUNK), 1))
    low_half = lane < HEAD_DIM
    zero = jnp.zeros((), BF16)

    def piece_kv(j):
        cols = slice(j * PIECE, (j + 1) * PIECE)
        kv = _dot(xb, w_ref[:, O_K + j * PIECE:O_K + (j + 1) * PIECE])
        kv_s[CHUNK:CHUNK + ROW_TILE, cols] = kv.astype(BF16)
        tail_ref[:, cols] = kv[ROW_TILE - WINDOW:, :]

    def piece_q(j):
        cols = slice(j * PIECE, (j + 1) * PIECE)
        q_s[:, cols] = (_dot(xb, wq_ref[:, cols]) * (HEAD_DIM ** -0.5)).astype(BF16)

    def piece_v(j):
        cols = slice(j * PIECE, (j + 1) * PIECE)
        hv_s[:, cols] = jax.nn.gelu(_dot(xb, w_ref[:, O_V + j * PIECE:O_V + (j + 1) * PIECE]))

    def piece_v_norm():
        va_s[...] = _layer_norm(hv_s[...], g_ref[...], b_ref[...]).astype(BF16)

    def piece_u(j):
        cols = slice(j * PIECE, (j + 1) * PIECE)
        u_s[:, cols] = jax.nn.gelu(_dot(xb, w_ref[:, O_U + j * PIECE:O_U + (j + 1) * PIECE])).astype(BF16)

    def piece_gate(j):
        cols = slice(j * PIECE, (j + 1) * PIECE)
        gate_ref[:, cols] = jax.nn.sigmoid(
            _dot(xb, w_ref[:, O_G + j * PIECE:O_G + (j + 1) * PIECE])).astype(BF16)

    def unit_spatial(c, g):
        rows, cols = slice(c * CHUNK, (c + 1) * CHUNK), slice(g * LANES, (g + 1) * LANES)
        w = jnp.where(tri, ws_ref[g], 0.0).astype(BF16)
        mixed = _dot(w, va_s[rows, cols]) + bs_ref[:, g:g + 1]
        ya_ref[rows, cols] = (u_s[rows, cols].astype(F32) * mixed).astype(BF16)

    band_ops = {}

    def band_operands(c, gp, hi):
        if (c, gp, hi) not in band_ops:
            band = slice(c * CHUNK, (c + 2) * CHUNK)
            keep = low_half if hi == 0 else jnp.logical_not(low_half)
            kn = jnp.where(keep, kv_s[band, gp * LANES:(gp + 1) * LANES], zero)
            vn = jnp.where(keep, kv_s[band, KV_WIDTH + gp * LANES:KV_WIDTH + (gp + 1) * LANES], zero)
            not_sink = lax.broadcasted_iota(jnp.int32, (BF16_ROWS, 1), 0) > 0
            kn = jnp.concatenate([jnp.where(not_sink, kn[:BF16_ROWS], zero), kn[BF16_ROWS:]], axis=0)
            vn = jnp.concatenate([jnp.where(not_sink, vn[:BF16_ROWS], zero), vn[BF16_ROWS:]], axis=0)
            band_ops[(c, gp, hi)] = (kn, vn)
        return band_ops[(c, gp, hi)]

    def unit_attention(c, pair):
        rows = slice(c * CHUNK, (c + 1) * CHUNK)
        table = jnp.where(first_tile, 1, 0) if c == 0 else 0
        r, gp = pair // 2, pair % 2
        qp = q_s[rows, pair * LANES:(pair + 1) * LANES]
        out = None
        for hi in range(2):
            kn, vn = band_operands(c, gp, hi)
            head = Q_PER_KV * (2 * gp + hi) + r
            s = _dot_nt(qp, kn) + bias_s[table, head]
            p = jnp.exp(s - jnp.max(s, axis=-1, keepdims=True))
            o = _dot(p.astype(BF16), vn) * (1.0 / jnp.sum(p, axis=-1, keepdims=True))
            out = o if out is None else jnp.where(low_half, out, o)
        yb_ref[rows, pair * LANES:(pair + 1) * LANES] = out.astype(BF16)

    for j in range(2 * KV_WIDTH // PIECE):
        piece_kv(j)
    for j in range(Q_WIDTH // PIECE):
        piece_q(j)
    for j in range(A_WIDTH // PIECE):
        piece_v(j)
    piece_v_norm()
    for j in range(A_WIDTH // PIECE):
        piece_u(j)
    for j in range(2 * D_MODEL // PIECE):
        piece_gate(j)
    for c in range(n_chunks):
        for g in range(A_GROUPS):
            unit_spatial(c, g)
        for pair in range(N_HEADS // 2):
            unit_attention(c, pair)

    kv_s[0:CHUNK, :] = kv_s[ROW_TILE:ROW_TILE + CHUNK, :]


def _prompt_front(sinks, x, w_in, w_q, ln_g, ln_b, w_s, b_s_t, bias, layer, batch, seq):
    n_tiles = seq // ROW_TILE
    tok_spec = lambda width: pl.BlockSpec((ROW_TILE, width), lambda b, i: (b * n_tiles + i, 0))
    vec_spec = pl.BlockSpec((None, 1, A_WIDTH), lambda b, i: (layer, 0, 0))
    rows = batch * seq
    return pl.pallas_call(
        _prompt_front_kernel,
        out_shape=[jax.ShapeDtypeStruct((rows, A_WIDTH), BF16),
                   jax.ShapeDtypeStruct((rows, Q_WIDTH), BF16),
                   jax.ShapeDtypeStruct((rows, 2 * D_MODEL), BF16),
                   jax.ShapeDtypeStruct((batch, WINDOW, 2 * KV_WIDTH), F32)],
        grid=(batch, n_tiles),
        in_specs=[pl.BlockSpec(memory_space=pltpu.SMEM),
                  tok_spec(D_MODEL),
                  _resident((None, D_MODEL, IN_WIDTH), lambda b, i: (layer, 0, 0)),
                  _resident((None, D_MODEL, Q_WIDTH), lambda b, i: (layer, 0, 0)),
                  vec_spec, vec_spec,
                  _resident((None, A_GROUPS, CHUNK, CHUNK), lambda b, i: (layer, 0, 0, 0)),
                  _resident((None, CHUNK, A_GROUPS), lambda b, i: (layer, 0, 0)),
                  _resident((2, N_HEADS, WINDOW, KEY_PAD), lambda b, i: (0, 0, 0, 0))],
        out_specs=[tok_spec(A_WIDTH), tok_spec(Q_WIDTH), tok_spec(2 * D_MODEL),
                   pl.BlockSpec((None, WINDOW, 2 * KV_WIDTH), lambda b, i: (b, 0, 0))],
        scratch_shapes=[pltpu.VMEM((ROW_TILE, A_WIDTH), BF16),
                        pltpu.VMEM((ROW_TILE, A_WIDTH), BF16),
                        pltpu.VMEM((ROW_TILE, Q_WIDTH), BF16),
                        pltpu.VMEM((CHUNK + ROW_TILE, 2 * KV_WIDTH), BF16),
                        pltpu.VMEM((ROW_TILE, A_WIDTH), F32),
                        pltpu.VMEM((2, N_HEADS, WINDOW, KEY_PAD), F32)],
        compiler_params=_params(2),
        name="prompt_front",
    )(sinks, x, w_in, w_q, ln_g, ln_b, w_s, b_s_t, bias)


def _inproj_sample_kernel(x_ref, w_ref, wq_ref, g_ref, b_ref, u_ref, va_ref, q_ref, gate_ref, kvt_ref):
    xb = x_ref[...].astype(BF16)

    def proj(c0, c1):
        return _dot(xb, w_ref[:, c0:c1])

    half = A_WIDTH // 2
    for c0 in range(O_U, O_V, half):
        u_ref[:, c0:c0 + half] = jax.nn.gelu(proj(c0, c0 + half)).astype(BF16)
    va_ref[...] = _layer_norm(jax.nn.gelu(proj(O_V, O_Q)), g_ref[...], b_ref[...])
    for c0 in range(0, Q_WIDTH, half):
        q_ref[:, c0:c0 + half] = (_dot(xb, wq_ref[:, c0:c0 + half]) * (HEAD_DIM ** -0.5)).astype(BF16)
    kvt_ref[...] = proj(O_K, O_G).T
    for c0 in range(0, 2 * D_MODEL, half):
        gate_ref[:, c0:c0 + half] = jax.nn.sigmoid(proj(O_G + c0, O_G + c0 + half)).astype(BF16)


def _inproj_sample(x, w_in, w_q, ln_g, ln_b, layer):
    rows = x.shape[0]
    row_spec = lambda width: pl.BlockSpec((ROW_TILE, width), lambda i: (i, 0))
    vec_spec = pl.BlockSpec((None, 1, A_WIDTH), lambda i: (layer, 0, 0))
    return pl.pallas_call(
        _inproj_sample_kernel,
        out_shape=[jax.ShapeDtypeStruct((rows, A_WIDTH), BF16),
                   jax.ShapeDtypeStruct((rows, A_WIDTH), F32),
                   jax.ShapeDtypeStruct((rows, Q_WIDTH), BF16),
                   jax.ShapeDtypeStruct((rows, 2 * D_MODEL), BF16),
                   jax.ShapeDtypeStruct((2 * KV_WIDTH, rows), F32)],
        grid=(rows // ROW_TILE,),
        in_specs=[row_spec(D_MODEL),
                  _resident((None, D_MODEL, IN_WIDTH), lambda i: (layer, 0, 0)),
                  _resident((None, D_MODEL, Q_WIDTH), lambda i: (layer, 0, 0)),
                  vec_spec, vec_spec],
        out_specs=[row_spec(A_WIDTH), row_spec(A_WIDTH), row_spec(Q_WIDTH), row_spec(2 * D_MODEL),
                   pl.BlockSpec((2 * KV_WIDTH, ROW_TILE), lambda i: (0, i))],
        compiler_params=_params(1),
        name="inproj_sample",
    )(x, w_in, w_q, ln_g, ln_b)


def _sample_mix_kernel(*refs, aliased):
    (sinks_ref, u_ref, va_ref, q_ref, kvt_ref, ck_ref, cv_ref, wexp_ref, bsexp_ref, biasc_ref,
     biasn_ref) = refs[:11]
    ya_ref, yb_ref, ko_ref, vo_ref, yb_acc = refs[11 + (2 if aliased else 0):]
    n_new = SUBLANES
    rows = SAMPLE_SEQS * n_new

    va3 = va_ref[...].reshape(SAMPLE_SEQS, n_new, A_WIDTH)
    t_idx = lax.broadcasted_iota(jnp.int32, (n_new, A_WIDTH), 0)
    mixed = jnp.broadcast_to(bsexp_ref[...][None], va3.shape)
    for s in range(n_new):
        w = jnp.where(t_idx >= s, wexp_ref[s], 0.0)
        mixed = mixed + w[None] * va3[:, s:s + 1, :]
    u3 = u_ref[...].astype(F32).reshape(SAMPLE_SEQS, n_new, A_WIDTH)
    ya_ref[...] = (u3 * mixed).reshape(rows, A_WIDTH).astype(BF16)

    group_of_lane = lax.broadcasted_iota(jnp.int32, (1, KV_WIDTH), 1) // HEAD_DIM
    row = lax.broadcasted_iota(jnp.int32, (N_HEADS * n_new, 1), 0)
    head_of_row = Q_PER_KV * ((row // n_new) % N_KV_HEADS) + row // (n_new * N_KV_HEADS)
    sink = jnp.zeros((N_HEADS * n_new, 1), F32)
    for h in range(N_HEADS):
        sink = jnp.where(head_of_row == h, sinks_ref[h], sink)
    lane = lax.broadcasted_iota(jnp.int32, (1, LANES), 1)
    seq_of_lane = lane // n_new
    keep_old = lane < WINDOW - n_new
    q32 = q_ref[...].astype(F32)
    k_new, v_new = kvt_ref[0:KV_WIDTH, :], kvt_ref[KV_WIDTH:2 * KV_WIDTH, :]
    k_new_b, v_new_b = k_new.astype(BF16), v_new.astype(BF16)
    bias_c, bias_n = biasc_ref[...], biasn_ref[...]
    for b in range(SAMPLE_SEQS):
        new = slice(b * n_new, (b + 1) * n_new)
        pieces = []
        for r in range(Q_PER_KV):
            blk = q32[new, r * KV_WIDTH:(r + 1) * KV_WIDTH]
            for g in range(N_KV_HEADS):
                pieces.append(jnp.where(group_of_lane == g, blk, 0.0))
        q_rows = jnp.concatenate(pieces, axis=0).astype(BF16)
        k_old, v_old = ck_ref[b], cv_ref[b]
        k_all = jnp.concatenate([k_old.astype(BF16), k_new_b], axis=1)
        v_all = jnp.concatenate([v_old.astype(BF16), v_new_b], axis=1)
        bias = jnp.concatenate([bias_c, jnp.where(seq_of_lane == b, bias_n, NEG_INF)], axis=1)
        s = _dot(q_rows, k_all) + bias
        m = jnp.maximum(jnp.max(s, axis=-1, keepdims=True), sink)
        p = jnp.exp(s - m)
        denom = jnp.sum(p, axis=-1, keepdims=True) + jnp.exp(sink - m)
        o = _dot_nt(p.astype(BF16), v_all) * (1.0 / denom)
        for r in range(Q_PER_KV):
            acc = jnp.zeros((n_new, KV_WIDTH), F32)
            for g in range(N_KV_HEADS):
                r0 = (r * N_KV_HEADS + g) * n_new
                acc = jnp.where(group_of_lane == g, o[r0:r0 + n_new, :], acc)
            yb_acc[new, r * KV_WIDTH:(r + 1) * KV_WIDTH] = acc
        shift_new = (WINDOW - n_new - b * n_new) % LANES
        ko_ref[b] = jnp.where(keep_old, pltpu.roll(k_old, WINDOW - n_new, 1), pltpu.roll(k_new, shift_new, 1))
        vo_ref[b] = jnp.where(keep_old, pltpu.roll(v_old, WINDOW - n_new, 1), pltpu.roll(v_new, shift_new, 1))
    yb_ref[...] = yb_acc[...].astype(BF16)


def _sample_mix(sinks, u, va32, q, kvt, cache_k, cache_v, wexp, bsexp, bias_c, bias_n, prev, layer):
    n_seq = cache_k.shape[1]
    rows = SAMPLE_SEQS * SUBLANES
    row_spec = lambda width: pl.BlockSpec((rows, width), lambda i: (i, 0))
    cache_spec = pl.BlockSpec((None, SAMPLE_SEQS, KV_WIDTH, WINDOW), lambda i: (layer, i, 0, 0))
    table_spec = pl.BlockSpec((N_HEADS * SUBLANES, LANES), lambda i: (0, 0))
    in_specs = [pl.BlockSpec(memory_space=pltpu.SMEM),
                row_spec(A_WIDTH), row_spec(A_WIDTH), row_spec(Q_WIDTH),
                pl.BlockSpec((2 * KV_WIDTH, rows), lambda i: (0, i)),
                cache_spec, cache_spec,
                pl.BlockSpec((None, SUBLANES, SUBLANES, A_WIDTH), lambda i: (layer, 0, 0, 0)),
                pl.BlockSpec((None, SUBLANES, A_WIDTH), lambda i: (layer, 0, 0)),
                table_spec, table_spec]
    operands = [sinks, u, va32, q, kvt, cache_k, cache_v, wexp, bsexp, bias_c, bias_n]
    aliases = {}
    if prev is not None:
        in_specs += [pl.BlockSpec(memory_space=pl.ANY)] * 2
        aliases = {len(operands): 2, len(operands) + 1: 3}
        operands += list(prev)
    return pl.pallas_call(
        functools.partial(_sample_mix_kernel, aliased=prev is not None),
        out_shape=[jax.ShapeDtypeStruct((n_seq * SUBLANES, A_WIDTH), BF16),
                   jax.ShapeDtypeStruct((n_seq * SUBLANES, Q_WIDTH), BF16),
                   jax.ShapeDtypeStruct(cache_k.shape, F32),
                   jax.ShapeDtypeStruct(cache_v.shape, F32)],
        grid=(n_seq // SAMPLE_SEQS,),
        in_specs=in_specs,
        out_specs=[row_spec(A_WIDTH), row_spec(Q_WIDTH), cache_spec, cache_spec],
        scratch_shapes=[pltpu.VMEM((rows, Q_WIDTH), F32)],
        input_output_aliases=aliases,
        compiler_params=_params(1),
        name="sample_mix",
    )(*operands)


def _merge_ffn_kernel(x_ref, ya_ref, yb_ref, gate_ref, wpa_ref, wpb_ref, wo_ref, wg_ref, wu_ref, wd_ref,
                      ln_ref, out_ref, *, alpha):
    g_a = gate_ref[:, 0:D_MODEL].astype(F32)
    g_b = gate_ref[:, D_MODEL:2 * D_MODEL].astype(F32)
    merged = g_a * _dot(ya_ref[...], wpa_ref[...]) + g_b * _dot(yb_ref[...], wpb_ref[...])
    mix = _dot(merged.astype(BF16), wo_ref[...])
    x1 = _layer_norm(alpha * x_ref[...] + mix, ln_ref[0:1, :], ln_ref[1:2, :])
    x1b = x1.astype(BF16)
    act = (jax.nn.silu(_dot(x1b, wg_ref[...])) * _dot(x1b, wu_ref[...])).astype(BF16)
    ffn = _dot(act, wd_ref[...])
    out_ref[...] = _layer_norm(alpha * x1 + ffn, ln_ref[2:3, :], ln_ref[3:4, :])


def _merge_ffn(x, ya, yb, gates, w_pa, w_pb, w_o, w_gate, w_up, w_down, ln_pack, layer, alpha):
    rows = x.shape[0]
    row_spec = lambda width: pl.BlockSpec((ROW_TILE, width), lambda i: (i, 0))
    weight = lambda k, n: _resident((None, k, n), lambda i: (layer, 0, 0))
    return pl.pallas_call(
        functools.partial(_merge_ffn_kernel, alpha=alpha),
        out_shape=jax.ShapeDtypeStruct((rows, D_MODEL), F32),
        grid=(rows // ROW_TILE,),
        in_specs=[row_spec(D_MODEL), row_spec(A_WIDTH), row_spec(Q_WIDTH), row_spec(2 * D_MODEL),
                  weight(A_WIDTH, D_MODEL), weight(Q_WIDTH, D_MODEL), weight(D_MODEL, D_MODEL),
                  weight(D_MODEL, D_FF), weight(D_MODEL, D_FF), weight(D_FF, D_MODEL),
                  pl.BlockSpec((None, 4, D_MODEL), lambda i: (layer, 0, 0))],
        out_specs=row_spec(D_MODEL),
        compiler_params=_params(1),
        name="merge_ffn",
    )(x, ya, yb, gates, w_pa, w_pb, w_o, w_gate, w_up, w_down, ln_pack)


def kernel(x_prompt, x_sample, cache_swa_k, cache_swa_v, rel_bias, w_in, ln_v_g, ln_v_b, w_s, b_s,
           sinks, w_pa, w_pb, w_o, ln1_g, ln1_b, w_gate, w_up, w_down, ln2_g, ln2_b):
    depth = w_in.shape[0]
    batch, seq, _ = x_prompt.shape
    n_seq, n_new, _ = x_sample.shape
    assert n_new == SUBLANES and seq % ROW_TILE == 0 and n_seq % SAMPLE_SEQS == 0
    assert (n_seq * n_new) % ROW_TILE == 0
    alpha = (2 * depth) ** 0.25

    def heads_rg(w, axis):
        shape = w.shape
        w = w.reshape(shape[:axis] + (N_KV_HEADS, Q_PER_KV, HEAD_DIM) + shape[axis + 1:])
        return jnp.swapaxes(w, axis, axis + 1).reshape(shape)

    w_in_b = w_in.astype(BF16)
    w_q_b = heads_rg(w_in[..., O_Q:O_K], 2).astype(BF16)
    w_pa_b, w_o_b = w_pa.astype(BF16), w_o.astype(BF16)
    w_pb_b = heads_rg(w_pb, 1).astype(BF16)
    w_gate_b, w_up_b, w_down_b = w_gate.astype(BF16), w_up.astype(BF16), w_down.astype(BF16)
    ln_v_g3, ln_v_b3 = ln_v_g[:, None, :], ln_v_b[:, None, :]
    ln_pack = jnp.stack([ln1_g, ln1_b, ln2_g, ln2_b], axis=1)
    b_s_t = jnp.swapaxes(b_s, 1, 2)
    wexp = jnp.repeat(jnp.transpose(w_s[:, :, :n_new, :n_new], (0, 3, 2, 1)), A_WIDTH // A_GROUPS, axis=-1)
    bsexp = jnp.repeat(jnp.swapaxes(b_s[:, :, :n_new], 1, 2), A_WIDTH // A_GROUPS, axis=-1)

    bias_p = _bias_tables(rel_bias, jnp.stack([_masked_buckets(WINDOW, KEY_PAD, False),
                                               _masked_buckets(WINDOW, KEY_PAD, True)]))
    bias_s = _bias_tables(rel_bias, _masked_buckets(n_new, WINDOW + n_new, False)[None])[0]
    bias_s = bias_s.reshape(N_KV_HEADS, Q_PER_KV, n_new, KEY_PAD)
    bias_s = jnp.swapaxes(bias_s, 0, 1).reshape(N_HEADS * n_new, KEY_PAD)
    bias_c = bias_s[:, :WINDOW]
    bias_n = jnp.tile(bias_s[:, WINDOW:WINDOW + n_new], (1, SAMPLE_SEQS))

    cache_k = jnp.transpose(cache_swa_k, (0, 1, 3, 4, 2)).reshape(depth, n_seq, KV_WIDTH, WINDOW)
    cache_v = jnp.transpose(cache_swa_v, (0, 1, 3, 4, 2)).reshape(depth, n_seq, KV_WIDTH, WINDOW)

    xp = x_prompt.reshape(batch * seq, D_MODEL)
    xs = x_sample.reshape(n_seq * n_new, D_MODEL)
    kp_l, vp_l, ga_l, new_cache = [], [], [], None
    for l in range(depth):
        ya, yb, gates, kv_tail = _prompt_front(sinks[l], xp, w_in_b, w_q_b, ln_v_g3, ln_v_b3, w_s, b_s_t,
                                               bias_p, l, batch, seq)
        xp = _merge_ffn(xp, ya, yb, gates, w_pa_b, w_pb_b, w_o_b, w_gate_b, w_up_b, w_down_b,
                        ln_pack, l, alpha)
        kp_l.append(kv_tail[..., :KV_WIDTH].reshape(batch, WINDOW, N_KV_HEADS, HEAD_DIM))
        vp_l.append(kv_tail[..., KV_WIDTH:].reshape(batch, WINDOW, N_KV_HEADS, HEAD_DIM))

        u, va32, q, gates, kvt = _inproj_sample(xs, w_in_b, w_q_b, ln_v_g3, ln_v_b3, l)
        ya, yb, *new_cache = _sample_mix(sinks[l], u, va32, q, kvt, cache_k, cache_v, wexp, bsexp,
                                         bias_c, bias_n, new_cache, l)
        xs = _merge_ffn(xs, ya, yb, gates, w_pa_b, w_pb_b, w_o_b, w_gate_b, w_up_b, w_down_b,
                        ln_pack, l, alpha)
        ga_l.append(va32.reshape(n_seq, n_new, A_WIDTH))

    def window_major(c):
        c = c.reshape(depth, n_seq, N_KV_HEADS, HEAD_DIM, WINDOW)
        return jnp.transpose(c, (0, 1, 4, 2, 3))

    return (xp.reshape(batch, seq, D_MODEL), xs.reshape(n_seq, n_new, D_MODEL),
            jnp.stack(kp_l), jnp.stack(vp_l), window_major(new_cache[0]), window_major(new_cache[1]),
            jnp.stack(ga_l))
```

```python
import functools
import math

import jax
import jax.numpy as jnp
from jax import lax
from jax.experimental import pallas as pl
from jax.experimental.pallas import tpu as pltpu

D_MODEL = 1024
CHUNK = 128
A_WIDTH = D_MODEL
A_GROUPS = 8
N_HEADS = 16
HEAD_DIM = 64
N_KV_HEADS = 4
Q_PER_KV = N_HEADS // N_KV_HEADS
WINDOW = 128
N_BUCKETS = 32
MAX_DISTANCE = 128
D_FF = 2816
LN_EPS = 1e-5
NEG_INF = -1e30
LOG2_E = math.log2(math.e)

KV_WIDTH = N_KV_HEADS * HEAD_DIM
Q_WIDTH = N_HEADS * HEAD_DIM
O_U = 0
O_V = O_U + A_WIDTH
O_Q = O_V + A_WIDTH
O_K = O_Q + Q_WIDTH
O_G = O_K + 2 * KV_WIDTH
IN_WIDTH = O_G + 2 * D_MODEL

LANES = 128
SUBLANES = 8
BF16_ROWS = 16
ROW_TILE = 512
PIECE = 512
SAMPLE_SEQS = LANES // SUBLANES
KEY_PAD = 2 * WINDOW
VMEM_LIMIT = 56 * 1024 * 1024

BF16 = jnp.bfloat16
F32 = jnp.float32


def _layer_norm(x, g, b):
    mu = jnp.mean(x, axis=-1, keepdims=True)
    xc = x - mu
    var = jnp.mean(xc * xc, axis=-1, keepdims=True)
    return xc * lax.rsqrt(var + LN_EPS) * g + b


def _dot(a, b):
    return jnp.dot(a, b, preferred_element_type=F32)


def _dot_nt(a, b):
    return lax.dot_general(a, b, (((1,), (1,)), ((), ())), preferred_element_type=F32)


def _resident(block_shape, index_map):
    return pl.BlockSpec(block_shape, index_map, pipeline_mode=pl.Buffered(1))


def _params(n_axes):
    return pltpu.CompilerParams(dimension_semantics=("arbitrary",) * n_axes,
                                vmem_limit_bytes=VMEM_LIMIT)


def _bias_kernel(rb_ref, bucket_ref, out_ref):
    bk = bucket_ref[...]
    for h in range(N_HEADS):
        acc = jnp.full(bk.shape, NEG_INF, F32)
        for b in range(N_BUCKETS):
            acc = jnp.where(bk == b, rb_ref[b, h], acc)
        out_ref[h] = acc


def _bias_tables(rel_bias, buckets):
    n, t, kp = buckets.shape
    return pl.pallas_call(
        _bias_kernel,
        out_shape=jax.ShapeDtypeStruct((n, N_HEADS, t, kp), F32),
        grid=(n,),
        in_specs=[pl.BlockSpec(memory_space=pltpu.SMEM),
                  pl.BlockSpec((None, t, kp), lambda i: (i, 0, 0))],
        out_specs=pl.BlockSpec((None, N_HEADS, t, kp), lambda i: (i, 0, 0, 0)),
        compiler_params=_params(1),
        name="bias_tables",
    )(rel_bias, buckets)


def _rel_bucket(dist):
    n = jnp.maximum(dist, 0)
    max_exact = N_BUCKETS // 2
    nf = jnp.maximum(n, 1).astype(F32)
    large = max_exact + (jnp.log(nf / max_exact) / math.log(MAX_DISTANCE / max_exact)
                         * (N_BUCKETS - max_exact)).astype(jnp.int32)
    large = jnp.minimum(large, N_BUCKETS - 1)
    return jnp.where(n < max_exact, n, large)


def _masked_buckets(n_q, n_keys, first_block):
    qi = jnp.arange(n_q, dtype=jnp.int32)[:, None]
    kj = jnp.arange(KEY_PAD, dtype=jnp.int32)[None, :]
    dist = qi + WINDOW - kj
    ok = (dist >= 0) & (dist < WINDOW) & (kj < n_keys)
    if first_block:
        ok = ok & (kj >= WINDOW)
    return jnp.where(ok, _rel_bucket(dist), -1)


def _prompt_front_kernel(sinks_ref, x_ref, w_ref, wq_ref, g_ref, b_ref, ws_ref, bs_ref, bias_ref,
                         ya_ref, yb_ref, gate_ref, tail_ref, u_s, va_s, q_s, kv_s, hv_s, bias_s):
    n_chunks = ROW_TILE // CHUNK
    first_tile = pl.program_id(1) == 0

    lane = lax.broadcasted_iota(jnp.int32, (1, LANES), 1)

    @pl.when(jnp.logical_and(pl.program_id(0) == 0, first_tile))
    def _():
        kv_s[0:CHUNK, :] = jnp.zeros((CHUNK, 2 * KV_WIDTH), BF16)
        for table in range(2):
            for h in range(N_HEADS):
                bias_s[table, h, :, 0:LANES] = LOG2_E * jnp.where(lane == 0, sinks_ref[h],
                                                                  bias_ref[table, h, :, 0:LANES])
                bias_s[table, h, :, LANES:] = LOG2_E * bias_ref[table, h, :, LANES:]

    xb = x_ref[...].astype(BF16)
    tri = (lax.broadcasted_iota(jnp.int32, (CHUNK, CHUNK), 0)
           >= lax.broadcasted_iota(jnp.int32, (CHUNK, CHUNK), 1))
    low_half = lane < HEAD_DIM
    zero = jnp.zeros((), BF16)

    def piece_kv(j):
        cols = slice(j * PIECE, (j + 1) * PIECE)
        kv = _dot(xb, w_ref[:, O_K + j * PIECE:O_K + (j + 1) * PIECE])
        kv_s[CHUNK:CHUNK + ROW_TILE, cols] = kv.astype(BF16)
        tail_ref[:, cols] = kv[ROW_TILE - WINDOW:, :]

    def piece_q(j):
        cols = slice(j * PIECE, (j + 1) * PIECE)
        q_s[:, cols] = (_dot(xb, wq_ref[:, cols]) * (LOG2_E * HEAD_DIM ** -0.5)).astype(BF16)

    def piece_v(j):
        cols = slice(j * PIECE, (j + 1) * PIECE)
        hv_s[:, cols] = jax.nn.gelu(_dot(xb, w_ref[:, O_V + j * PIECE:O_V + (j + 1) * PIECE]))

    def piece_v_norm():
        va_s[...] = _layer_norm(hv_s[...], g_ref[...], b_ref[...]).astype(BF16)

    def piece_u(j):
        cols = slice(j * PIECE, (j + 1) * PIECE)
        u_s[:, cols] = jax.nn.gelu(_dot(xb, w_ref[:, O_U + j * PIECE:O_U + (j + 1) * PIECE])).astype(BF16)

    def piece_gate(j):
        cols = slice(j * PIECE, (j + 1) * PIECE)
        gate_ref[:, cols] = jax.nn.sigmoid(
            _dot(xb, w_ref[:, O_G + j * PIECE:O_G + (j + 1) * PIECE])).astype(BF16)

    def unit_spatial(c, g):
        rows, cols = slice(c * CHUNK, (c + 1) * CHUNK), slice(g * LANES, (g + 1) * LANES)
        w = jnp.where(tri, ws_ref[g], 0.0).astype(BF16)
        mixed = _dot(w, va_s[rows, cols]) + bs_ref[:, g:g + 1]
        ya_ref[rows, cols] = (u_s[rows, cols].astype(F32) * mixed).astype(BF16)

    def band_operands(c, gp, hi):
        band = slice(c * CHUNK, (c + 2) * CHUNK)
        keep = low_half if hi == 0 else jnp.logical_not(low_half)
        kn = jnp.where(keep, kv_s[band, gp * LANES:(gp + 1) * LANES], zero)
        vn = jnp.where(keep, kv_s[band, KV_WIDTH + gp * LANES:KV_WIDTH + (gp + 1) * LANES], zero)
        not_sink = lax.broadcasted_iota(jnp.int32, (BF16_ROWS, 1), 0) > 0
        kn = jnp.concatenate([jnp.where(not_sink, kn[:BF16_ROWS], zero), kn[BF16_ROWS:]], axis=0)
        vn = jnp.concatenate([jnp.where(not_sink, vn[:BF16_ROWS], zero), vn[BF16_ROWS:]], axis=0)
        return kn, vn

    def unit_attention(c, gp):
        rows = slice(c * CHUNK, (c + 1) * CHUNK)
        table = jnp.where(first_tile, 1, 0) if c == 0 else 0
        pair_cols = [slice((r * 2 + gp) * LANES, (r * 2 + gp + 1) * LANES) for r in range(Q_PER_KV)]
        q4 = jnp.concatenate([q_s[rows, cols] for cols in pair_cols], axis=0)
        out = None
        for hi in range(2):
            kn, vn = band_operands(c, gp, hi)
            h0 = Q_PER_KV * (2 * gp + hi)
            bias = bias_s[table, h0:h0 + Q_PER_KV].reshape(Q_PER_KV * CHUNK, KEY_PAD)
            s = _dot_nt(q4, kn) + bias
            p = jnp.exp2(s - jnp.max(s, axis=-1, keepdims=True))
            o = _dot(p.astype(BF16), vn) * (1.0 / jnp.sum(p, axis=-1, keepdims=True))
            out = o if out is None else jnp.where(low_half, out, o)
        for r, cols in enumerate(pair_cols):
            yb_ref[rows, cols] = out[r * CHUNK:(r + 1) * CHUNK].astype(BF16)

    for j in range(2 * KV_WIDTH // PIECE):
        piece_kv(j)
    for j in range(Q_WIDTH // PIECE):
        piece_q(j)
    for j in range(A_WIDTH // PIECE):
        piece_v(j)
    piece_v_norm()
    for j in range(A_WIDTH // PIECE):
        piece_u(j)
    for j in range(2 * D_MODEL // PIECE):
        piece_gate(j)
    for c in range(n_chunks):
        for g in range(A_GROUPS):
            unit_spatial(c, g)
        for gp in range(N_KV_HEADS // 2):
            unit_attention(c, gp)

    kv_s[0:CHUNK, :] = kv_s[ROW_TILE:ROW_TILE + CHUNK, :]


def _prompt_front(sinks, x, w_in, w_q, ln_g, ln_b, w_s, b_s_t, bias, layer, batch, seq):
    n_tiles = seq // ROW_TILE
    tok_spec = lambda width: pl.BlockSpec((ROW_TILE, width), lambda b, i: (b * n_tiles + i, 0))
    vec_spec = pl.BlockSpec((None, 1, A_WIDTH), lambda b, i: (layer, 0, 0))
    rows = batch * seq
    return pl.pallas_call(
        _prompt_front_kernel,
        out_shape=[jax.ShapeDtypeStruct((rows, A_WIDTH), BF16),
                   jax.ShapeDtypeStruct((rows, Q_WIDTH), BF16),
                   jax.ShapeDtypeStruct((rows, 2 * D_MODEL), BF16),
                   jax.ShapeDtypeStruct((batch, WINDOW, 2 * KV_WIDTH), F32)],
        grid=(batch, n_tiles),
        in_specs=[pl.BlockSpec(memory_space=pltpu.SMEM),
                  tok_spec(D_MODEL),
                  _resident((None, D_MODEL, IN_WIDTH), lambda b, i: (layer, 0, 0)),
                  _resident((None, D_MODEL, Q_WIDTH), lambda b, i: (layer, 0, 0)),
                  vec_spec, vec_spec,
                  _resident((None, A_GROUPS, CHUNK, CHUNK), lambda b, i: (layer, 0, 0, 0)),
                  _resident((None, CHUNK, A_GROUPS), lambda b, i: (layer, 0, 0)),
                  _resident((2, N_HEADS, WINDOW, KEY_PAD), lambda b, i: (0, 0, 0, 0))],
        out_specs=[tok_spec(A_WIDTH), tok_spec(Q_WIDTH), tok_spec(2 * D_MODEL),
                   pl.BlockSpec((None, WINDOW, 2 * KV_WIDTH), lambda b, i: (b, 0, 0))],
        scratch_shapes=[pltpu.VMEM((ROW_TILE, A_WIDTH), BF16),
                        pltpu.VMEM((ROW_TILE, A_WIDTH), BF16),
                        pltpu.VMEM((ROW_TILE, Q_WIDTH), BF16),
                        pltpu.VMEM((CHUNK + ROW_TILE, 2 * KV_WIDTH), BF16),
                        pltpu.VMEM((ROW_TILE, A_WIDTH), F32),
                        pltpu.VMEM((2, N_HEADS, WINDOW, KEY_PAD), F32)],
        compiler_params=_params(2),
        name="prompt_front",
    )(sinks, x, w_in, w_q, ln_g, ln_b, w_s, b_s_t, bias)


def _inproj_sample_kernel(x_ref, w_ref, wq_ref, g_ref, b_ref, u_ref, va_ref, q_ref, gate_ref, kvt_ref):
    xb = x_ref[...].astype(BF16)

    def proj(c0, c1):
        return _dot(xb, w_ref[:, c0:c1])

    half = A_WIDTH // 2
    for c0 in range(O_U, O_V, half):
        u_ref[:, c0:c0 + half] = jax.nn.gelu(proj(c0, c0 + half)).astype(BF16)
    va_ref[...] = _layer_norm(jax.nn.gelu(proj(O_V, O_Q)), g_ref[...], b_ref[...])
    for c0 in range(0, Q_WIDTH, half):
        q_ref[:, c0:c0 + half] = (_dot(xb, wq_ref[:, c0:c0 + half]) * (HEAD_DIM ** -0.5)).astype(BF16)
    kvt_ref[...] = proj(O_K, O_G).T
    for c0 in range(0, 2 * D_MODEL, half):
        gate_ref[:, c0:c0 + half] = jax.nn.sigmoid(proj(O_G + c0, O_G + c0 + half)).astype(BF16)


def _inproj_sample(x, w_in, w_q, ln_g, ln_b, layer):
    rows = x.shape[0]
    row_spec = lambda width: pl.BlockSpec((ROW_TILE, width), lambda i: (i, 0))
    vec_spec = pl.BlockSpec((None, 1, A_WIDTH), lambda i: (layer, 0, 0))
    return pl.pallas_call(
        _inproj_sample_kernel,
        out_shape=[jax.ShapeDtypeStruct((rows, A_WIDTH), BF16),
                   jax.ShapeDtypeStruct((rows, A_WIDTH), F32),
                   jax.ShapeDtypeStruct((rows, Q_WIDTH), BF16),
                   jax.ShapeDtypeStruct((rows, 2 * D_MODEL), BF16),
                   jax.ShapeDtypeStruct((2 * KV_WIDTH, rows), F32)],
        grid=(rows // ROW_TILE,),
        in_specs=[row_spec(D_MODEL),
                  _resident((None, D_MODEL, IN_WIDTH), lambda i: (layer, 0, 0)),
                  _resident((None, D_MODEL, Q_WIDTH), lambda i: (layer, 0, 0)),
                  vec_spec, vec_spec],
        out_specs=[row_spec(A_WIDTH), row_spec(A_WIDTH), row_spec(Q_WIDTH), row_spec(2 * D_MODEL),
                   pl.BlockSpec((2 * KV_WIDTH, ROW_TILE), lambda i: (0, i))],
        compiler_params=_params(1),
        name="inproj_sample",
    )(x, w_in, w_q, ln_g, ln_b)


def _sample_mix_kernel(*refs, aliased):
    (sinks_ref, u_ref, va_ref, q_ref, kvt_ref, ck_ref, cv_ref, wexp_ref, bsexp_ref, biasc_ref,
     biasn_ref) = refs[:11]
    ya_ref, yb_ref, ko_ref, vo_ref, yb_acc = refs[11 + (2 if aliased else 0):]
    n_new = SUBLANES
    rows = SAMPLE_SEQS * n_new

    va3 = va_ref[...].reshape(SAMPLE_SEQS, n_new, A_WIDTH)
    t_idx = lax.broadcasted_iota(jnp.int32, (n_new, A_WIDTH), 0)
    mixed = jnp.broadcast_to(bsexp_ref[...][None], va3.shape)
    for s in range(n_new):
        w = jnp.where(t_idx >= s, wexp_ref[s], 0.0)
        mixed = mixed + w[None] * va3[:, s:s + 1, :]
    u3 = u_ref[...].astype(F32).reshape(SAMPLE_SEQS, n_new, A_WIDTH)
    ya_ref[...] = (u3 * mixed).reshape(rows, A_WIDTH).astype(BF16)

    group_of_lane = lax.broadcasted_iota(jnp.int32, (1, KV_WIDTH), 1) // HEAD_DIM
    row = lax.broadcasted_iota(jnp.int32, (N_HEADS * n_new, 1), 0)
    head_of_row = Q_PER_KV * ((row // n_new) % N_KV_HEADS) + row // (n_new * N_KV_HEADS)
    sink = jnp.zeros((N_HEADS * n_new, 1), F32)
    for h in range(N_HEADS):
        sink = jnp.where(head_of_row == h, sinks_ref[h], sink)
    lane = lax.broadcasted_iota(jnp.int32, (1, LANES), 1)
    seq_of_lane = lane // n_new
    keep_old = lane < WINDOW - n_new
    q32 = q_ref[...].astype(F32)
    k_new, v_new = kvt_ref[0:KV_WIDTH, :], kvt_ref[KV_WIDTH:2 * KV_WIDTH, :]
    k_new_b, v_new_b = k_new.astype(BF16), v_new.astype(BF16)
    bias_c, bias_n = biasc_ref[...], biasn_ref[...]
    for b in range(SAMPLE_SEQS):
        new = slice(b * n_new, (b + 1) * n_new)
        pieces = []
        for r in range(Q_PER_KV):
            blk = q32[new, r * KV_WIDTH:(r + 1) * KV_WIDTH]
            for g in range(N_KV_HEADS):
                pieces.append(jnp.where(group_of_lane == g, blk, 0.0))
        q_rows = jnp.concatenate(pieces, axis=0).astype(BF16)
        k_old, v_old = ck_ref[b], cv_ref[b]
        k_all = jnp.concatenate([k_old.astype(BF16), k_new_b], axis=1)
        v_all = jnp.concatenate([v_old.astype(BF16), v_new_b], axis=1)
        bias = jnp.concatenate([bias_c, jnp.where(seq_of_lane == b, bias_n, NEG_INF)], axis=1)
        s = _dot(q_rows, k_all) + bias
        m = jnp.maximum(jnp.max(s, axis=-1, keepdims=True), sink)
        p = jnp.exp(s - m)
        denom = jnp.sum(p, axis=-1, keepdims=True) + jnp.exp(sink - m)
        o = _dot_nt(p.astype(BF16), v_all) * (1.0 / denom)
        for r in range(Q_PER_KV):
            acc = jnp.zeros((n_new, KV_WIDTH), F32)
            for g in range(N_KV_HEADS):
                r0 = (r * N_KV_HEADS + g) * n_new
                acc = jnp.where(group_of_lane == g, o[r0:r0 + n_new, :], acc)
            yb_acc[new, r * KV_WIDTH:(r + 1) * KV_WIDTH] = acc
        shift_new = (WINDOW - n_new - b * n_new) % LANES
        ko_ref[b] = jnp.where(keep_old, pltpu.roll(k_old, WINDOW - n_new, 1), pltpu.roll(k_new, shift_new, 1))
        vo_ref[b] = jnp.where(keep_old, pltpu.roll(v_old, WINDOW - n_new, 1), pltpu.roll(v_new, shift_new, 1))
    yb_ref[...] = yb_acc[...].astype(BF16)


def _sample_mix(sinks, u, va32, q, kvt, cache_k, cache_v, wexp, bsexp, bias_c, bias_n, prev, layer):
    n_seq = cache_k.shape[1]
    rows = SAMPLE_SEQS * SUBLANES
    row_spec = lambda width: pl.BlockSpec((rows, width), lambda i: (i, 0))
    cache_spec = pl.BlockSpec((None, SAMPLE_SEQS, KV_WIDTH, WINDOW), lambda i: (layer, i, 0, 0))
    table_spec = pl.BlockSpec((N_HEADS * SUBLANES, LANES), lambda i: (0, 0))
    in_specs = [pl.BlockSpec(memory_space=pltpu.SMEM),
                row_spec(A_WIDTH), row_spec(A_WIDTH), row_spec(Q_WIDTH),
                pl.BlockSpec((2 * KV_WIDTH, rows), lambda i: (0, i)),
                cache_spec, cache_spec,
                pl.BlockSpec((None, SUBLANES, SUBLANES, A_WIDTH), lambda i: (layer, 0, 0, 0)),
                pl.BlockSpec((None, SUBLANES, A_WIDTH), lambda i: (layer, 0, 0)),
                table_spec, table_spec]
    operands = [sinks, u, va32, q, kvt, cache_k, cache_v, wexp, bsexp, bias_c, bias_n]
    aliases = {}
    if prev is not None:
        in_specs += [pl.BlockSpec(memory_space=pl.ANY)] * 2
        aliases = {len(operands): 2, len(operands) + 1: 3}
        operands += list(prev)
    return pl.pallas_call(
        functools.partial(_sample_mix_kernel, aliased=prev is not None),
        out_shape=[jax.ShapeDtypeStruct((n_seq * SUBLANES, A_WIDTH), BF16),
                   jax.ShapeDtypeStruct((n_seq * SUBLANES, Q_WIDTH), BF16),
                   jax.ShapeDtypeStruct(cache_k.shape, F32),
                   jax.ShapeDtypeStruct(cache_v.shape, F32)],
        grid=(n_seq // SAMPLE_SEQS,),
        in_specs=in_specs,
        out_specs=[row_spec(A_WIDTH), row_spec(Q_WIDTH), cache_spec, cache_spec],
        scratch_shapes=[pltpu.VMEM((rows, Q_WIDTH), F32)],
        input_output_aliases=aliases,
        compiler_params=_params(1),
        name="sample_mix",
    )(*operands)


def _merge_ffn_kernel(x_ref, ya_ref, yb_ref, gate_ref, wpa_ref, wpb_ref, wo_ref, wg_ref, wu_ref, wd_ref,
                      ln_ref, out_ref, *, alpha):
    g_a = gate_ref[:, 0:D_MODEL].astype(F32)
    g_b = gate_ref[:, D_MODEL:2 * D_MODEL].astype(F32)
    merged = g_a * _dot(ya_ref[...], wpa_ref[...]) + g_b * _dot(yb_ref[...], wpb_ref[...])
    mix = _dot(merged.astype(BF16), wo_ref[...])
    x1 = _layer_norm(alpha * x_ref[...] + mix, ln_ref[0:1, :], ln_ref[1:2, :])
    x1b = x1.astype(BF16)
    act = (jax.nn.silu(_dot(x1b, wg_ref[...])) * _dot(x1b, wu_ref[...])).astype(BF16)
    ffn = _dot(act, wd_ref[...])
    out_ref[...] = _layer_norm(alpha * x1 + ffn, ln_ref[2:3, :], ln_ref[3:4, :])


def _merge_ffn(x, ya, yb, gates, w_pa, w_pb, w_o, w_gate, w_up, w_down, ln_pack, layer, alpha):
    rows = x.shape[0]
    row_spec = lambda width: pl.BlockSpec((ROW_TILE, width), lambda i: (i, 0))
    weight = lambda k, n: _resident((None, k, n), lambda i: (layer, 0, 0))
    return pl.pallas_call(
        functools.partial(_merge_ffn_kernel, alpha=alpha),
        out_shape=jax.ShapeDtypeStruct((rows, D_MODEL), F32),
        grid=(rows // ROW_TILE,),
        in_specs=[row_spec(D_MODEL), row_spec(A_WIDTH), row_spec(Q_WIDTH), row_spec(2 * D_MODEL),
                  weight(A_WIDTH, D_MODEL), weight(Q_WIDTH, D_MODEL), weight(D_MODEL, D_MODEL),
                  weight(D_MODEL, D_FF), weight(D_MODEL, D_FF), weight(D_FF, D_MODEL),
                  pl.BlockSpec((None, 4, D_MODEL), lambda i: (layer, 0, 0))],
        out_specs=row_spec(D_MODEL),
        compiler_params=_params(1),
        name="merge_ffn",
    )(x, ya, yb, gates, w_pa, w_pb, w_o, w_gate, w_up, w_down, ln_pack)


def kernel(x_prompt, x_sample, cache_swa_k, cache_swa_v, rel_bias, w_in, ln_v_g, ln_v_b, w_s, b_s,
           sinks, w_pa, w_pb, w_o, ln1_g, ln1_b, w_gate, w_up, w_down, ln2_g, ln2_b):
    depth = w_in.shape[0]
    batch, seq, _ = x_prompt.shape
    n_seq, n_new, _ = x_sample.shape
    assert n_new == SUBLANES and seq % ROW_TILE == 0 and n_seq % SAMPLE_SEQS == 0
    assert (n_seq * n_new) % ROW_TILE == 0
    alpha = (2 * depth) ** 0.25

    def heads_rg(w, axis):
        shape = w.shape
        w = w.reshape(shape[:axis] + (N_KV_HEADS, Q_PER_KV, HEAD_DIM) + shape[axis + 1:])
        return jnp.swapaxes(w, axis, axis + 1).reshape(shape)

    w_in_b = w_in.astype(BF16)
    w_q_b = heads_rg(w_in[..., O_Q:O_K], 2).astype(BF16)
    w_pa_b, w_o_b = w_pa.astype(BF16), w_o.astype(BF16)
    w_pb_b = heads_rg(w_pb, 1).astype(BF16)
    w_gate_b, w_up_b, w_down_b = w_gate.astype(BF16), w_up.astype(BF16), w_down.astype(BF16)
    ln_v_g3, ln_v_b3 = ln_v_g[:, None, :], ln_v_b[:, None, :]
    ln_pack = jnp.stack([ln1_g, ln1_b, ln2_g, ln2_b], axis=1)
    b_s_t = jnp.swapaxes(b_s, 1, 2)
    wexp = jnp.repeat(jnp.transpose(w_s[:, :, :n_new, :n_new], (0, 3, 2, 1)), A_WIDTH // A_GROUPS, axis=-1)
    bsexp = jnp.repeat(jnp.swapaxes(b_s[:, :, :n_new], 1, 2), A_WIDTH // A_GROUPS, axis=-1)

    bias_p = _bias_tables(rel_bias, jnp.stack([_masked_buckets(WINDOW, KEY_PAD, False),
                                               _masked_buckets(WINDOW, KEY_PAD, True)]))
    bias_s = _bias_tables(rel_bias, _masked_buckets(n_new, WINDOW + n_new, False)[None])[0]
    bias_s = bias_s.reshape(N_KV_HEADS, Q_PER_KV, n_new, KEY_PAD)
    bias_s = jnp.swapaxes(bias_s, 0, 1).reshape(N_HEADS * n_new, KEY_PAD)
    bias_c = bias_s[:, :WINDOW]
    bias_n = jnp.tile(bias_s[:, WINDOW:WINDOW + n_new], (1, SAMPLE_SEQS))

    cache_k = jnp.transpose(cache_swa_k, (0, 1, 3, 4, 2)).reshape(depth, n_seq, KV_WIDTH, WINDOW)
    cache_v = jnp.transpose(cache_swa_v, (0, 1, 3, 4, 2)).reshape(depth, n_seq, KV_WIDTH, WINDOW)

    xp = x_prompt.reshape(batch * seq, D_MODEL)
    xs = x_sample.reshape(n_seq * n_new, D_MODEL)
    kp_l, vp_l, ga_l, new_cache = [], [], [], None
    for l in range(depth):
        ya, yb, gates, kv_tail = _prompt_front(sinks[l], xp, w_in_b, w_q_b, ln_v_g3, ln_v_b3, w_s, b_s_t,
                                               bias_p, l, batch, seq)
        xp = _merge_ffn(xp, ya, yb, gates, w_pa_b, w_pb_b, w_o_b, w_gate_b, w_up_b, w_down_b,
                        ln_pack, l, alpha)
        kp_l.append(kv_tail[..., :KV_WIDTH].reshape(batch, WINDOW, N_KV_HEADS, HEAD_DIM))
        vp_l.append(kv_tail[..., KV_WIDTH:].reshape(batch, WINDOW, N_KV_HEADS, HEAD_DIM))

        u, va32, q, gates, kvt = _inproj_sample(xs, w_in_b, w_q_b, ln_v_g3, ln_v_b3, l)
        ya, yb, *new_cache = _sample_mix(sinks[l], u, va32, q, kvt, cache_k, cache_v, wexp, bsexp,
                                         bias_c, bias_n, new_cache, l)
        xs = _merge_ffn(xs, ya, yb, gates, w_pa_b, w_pb_b, w_o_b, w_gate_b, w_up_b, w_down_b,
                        ln_pack, l, alpha)
        ga_l.append(va32.reshape(n_seq, n_new, A_WIDTH))

    def window_major(c):
        c = c.reshape(depth, n_seq, N_KV_HEADS, HEAD_DIM, WINDOW)
        return jnp.transpose(c, (0, 1, 4, 2, 3))

    return (xp.reshape(batch, seq, D_MODEL), xs.reshape(n_seq, n_new, D_MODEL),
            jnp.stack(kp_l), jnp.stack(vp_l), window_major(new_cache[0]), window_major(new_cache[1]),
            jnp.stack(ga_l))
```

```python
import functools
import math

import jax
import jax.numpy as jnp
from jax import lax
from jax.experimental import pallas as pl
from jax.experimental.pallas import tpu as pltpu

D_MODEL = 1024
CHUNK = 128
A_WIDTH = D_MODEL
A_GROUPS = 8
N_HEADS = 16
HEAD_DIM = 64
N_KV_HEADS = 4
Q_PER_KV = N_HEADS // N_KV_HEADS
WINDOW = 128
N_BUCKETS = 32
MAX_DISTANCE = 128
D_FF = 2816
LN_EPS = 1e-5
NEG_INF = -1e30
LOG2_E = math.log2(math.e)

KV_WIDTH = N_KV_HEADS * HEAD_DIM
Q_WIDTH = N_HEADS * HEAD_DIM
O_U = 0
O_V = O_U + A_WIDTH
O_Q = O_V + A_WIDTH
O_K = O_Q + Q_WIDTH
O_G = O_K + 2 * KV_WIDTH
IN_WIDTH = O_G + 2 * D_MODEL

LANES = 128
SUBLANES = 8
BF16_ROWS = 16
ROW_TILE = 512
PIECE = 512
SAMPLE_SEQS = LANES // SUBLANES
KEY_PAD = 2 * WINDOW
VMEM_LIMIT = 56 * 1024 * 1024

BF16 = jnp.bfloat16
F32 = jnp.float32


def _layer_norm(x, g, b):
    mu = jnp.mean(x, axis=-1, keepdims=True)
    xc = x - mu
    var = jnp.mean(xc * xc, axis=-1, keepdims=True)
    return xc * lax.rsqrt(var + LN_EPS) * g + b


def _gelu(x):
    return x * jax.nn.sigmoid((2.0 * math.sqrt(2.0 / math.pi)) * (x + 0.044715 * (x * x * x)))


def _dot(a, b):
    return jnp.dot(a, b, preferred_element_type=F32)


def _dot_nt(a, b):
    return lax.dot_general(a, b, (((1,), (1,)), ((), ())), preferred_element_type=F32)


def _resident(block_shape, index_map):
    return pl.BlockSpec(block_shape, index_map, pipeline_mode=pl.Buffered(1))


def _params(n_axes):
    return pltpu.CompilerParams(dimension_semantics=("arbitrary",) * n_axes,
                                vmem_limit_bytes=VMEM_LIMIT)


def _bias_kernel(rb_ref, bucket_ref, out_ref):
    bk = bucket_ref[...]
    for h in range(N_HEADS):
        acc = jnp.full(bk.shape, NEG_INF, F32)
        for b in range(N_BUCKETS):
            acc = jnp.where(bk == b, rb_ref[b, h], acc)
        out_ref[h] = acc


def _bias_tables(rel_bias, buckets):
    n, t, kp = buckets.shape
    return pl.pallas_call(
        _bias_kernel,
        out_shape=jax.ShapeDtypeStruct((n, N_HEADS, t, kp), F32),
        grid=(n,),
        in_specs=[pl.BlockSpec(memory_space=pltpu.SMEM),
                  pl.BlockSpec((None, t, kp), lambda i: (i, 0, 0))],
        out_specs=pl.BlockSpec((None, N_HEADS, t, kp), lambda i: (i, 0, 0, 0)),
        compiler_params=_params(1),
        name="bias_tables",
    )(rel_bias, buckets)


def _rel_bucket(dist):
    n = jnp.maximum(dist, 0)
    max_exact = N_BUCKETS // 2
    nf = jnp.maximum(n, 1).astype(F32)
    large = max_exact + (jnp.log(nf / max_exact) / math.log(MAX_DISTANCE / max_exact)
                         * (N_BUCKETS - max_exact)).astype(jnp.int32)
    large = jnp.minimum(large, N_BUCKETS - 1)
    return jnp.where(n < max_exact, n, large)


def _masked_buckets(n_q, n_keys, first_block):
    qi = jnp.arange(n_q, dtype=jnp.int32)[:, None]
    kj = jnp.arange(KEY_PAD, dtype=jnp.int32)[None, :]
    dist = qi + WINDOW - kj
    ok = (dist >= 0) & (dist < WINDOW) & (kj < n_keys)
    if first_block:
        ok = ok & (kj >= WINDOW)
    return jnp.where(ok, _rel_bucket(dist), -1)


def _prompt_front_kernel(sinks_ref, x_ref, w_ref, wq_ref, g_ref, b_ref, ws_ref, bs_ref, bias_ref,
                         ya_ref, yb_ref, gate_ref, tail_ref, u_s, va_s, q_s, kv_s, hv_s, bias_s):
    n_chunks = ROW_TILE // CHUNK
    first_tile = pl.program_id(1) == 0

    lane = lax.broadcasted_iota(jnp.int32, (1, LANES), 1)

    @pl.when(jnp.logical_and(pl.program_id(0) == 0, first_tile))
    def _():
        kv_s[0:CHUNK, :] = jnp.zeros((CHUNK, 2 * KV_WIDTH), BF16)
        for table in range(2):
            for h in range(N_HEADS):
                bias_s[table, h, :, 0:LANES] = LOG2_E * jnp.where(lane == 0, sinks_ref[h],
                                                                  bias_ref[table, h, :, 0:LANES])
                bias_s[table, h, :, LANES:] = LOG2_E * bias_ref[table, h, :, LANES:]

    xb = x_ref[...].astype(BF16)
    tri = (lax.broadcasted_iota(jnp.int32, (CHUNK, CHUNK), 0)
           >= lax.broadcasted_iota(jnp.int32, (CHUNK, CHUNK), 1))
    low_half = lane < HEAD_DIM
    zero = jnp.zeros((), BF16)

    def piece_kv(j):
        cols = slice(j * PIECE, (j + 1) * PIECE)
        kv = _dot(xb, w_ref[:, O_K + j * PIECE:O_K + (j + 1) * PIECE])
        kv_s[CHUNK:CHUNK + ROW_TILE, cols] = kv.astype(BF16)
        tail_ref[:, cols] = kv[ROW_TILE - WINDOW:, :]

    def piece_q(j):
        cols = slice(j * PIECE, (j + 1) * PIECE)
        q_s[:, cols] = (_dot(xb, wq_ref[:, cols]) * (LOG2_E * HEAD_DIM ** -0.5)).astype(BF16)

    def piece_v(j):
        cols = slice(j * PIECE, (j + 1) * PIECE)
        hv_s[:, cols] = _gelu(_dot(xb, w_ref[:, O_V + j * PIECE:O_V + (j + 1) * PIECE]))

    def piece_v_norm():
        va_s[...] = _layer_norm(hv_s[...], g_ref[...], b_ref[...]).astype(BF16)

    def piece_u(j):
        cols = slice(j * PIECE, (j + 1) * PIECE)
        u_s[:, cols] = _gelu(_dot(xb, w_ref[:, O_U + j * PIECE:O_U + (j + 1) * PIECE])).astype(BF16)

    def piece_gate(j):
        cols = slice(j * PIECE, (j + 1) * PIECE)
        gate_ref[:, cols] = jax.nn.sigmoid(
            _dot(xb, w_ref[:, O_G + j * PIECE:O_G + (j + 1) * PIECE])).astype(BF16)

    def unit_spatial(c, g):
        cols = slice(g * LANES, (g + 1) * LANES)
        chunk_rows = [slice((c + k) * CHUNK, (c + k + 1) * CHUNK) for k in range(2)]
        w = jnp.where(tri, ws_ref[g], 0.0).astype(BF16)
        mixed = _dot(w, jnp.concatenate([va_s[rows, cols] for rows in chunk_rows], axis=1))
        for k, rows in enumerate(chunk_rows):
            gated = u_s[rows, cols].astype(F32) * (mixed[:, k * LANES:(k + 1) * LANES] + bs_ref[:, g:g + 1])
            ya_ref[rows, cols] = gated.astype(BF16)

    def band_operands(c, gp, hi):
        band = slice(c * CHUNK, (c + 2) * CHUNK)
        keep = low_half if hi == 0 else jnp.logical_not(low_half)
        kn = jnp.where(keep, kv_s[band, gp * LANES:(gp + 1) * LANES], zero)
        vn = jnp.where(keep, kv_s[band, KV_WIDTH + gp * LANES:KV_WIDTH + (gp + 1) * LANES], zero)
        not_sink = lax.broadcasted_iota(jnp.int32, (BF16_ROWS, 1), 0) > 0
        kn = jnp.concatenate([jnp.where(not_sink, kn[:BF16_ROWS], zero), kn[BF16_ROWS:]], axis=0)
        vn = jnp.concatenate([jnp.where(not_sink, vn[:BF16_ROWS], zero), vn[BF16_ROWS:]], axis=0)
        return kn, vn

    def unit_attention(c, gp):
        rows = slice(c * CHUNK, (c + 1) * CHUNK)
        table = jnp.where(first_tile, 1, 0) if c == 0 else 0
        pair_cols = [slice((r * 2 + gp) * LANES, (r * 2 + gp + 1) * LANES) for r in range(Q_PER_KV)]
        q4 = jnp.concatenate([q_s[rows, cols] for cols in pair_cols], axis=0)
        out = None
        for hi in range(2):
            kn, vn = band_operands(c, gp, hi)
            h0 = Q_PER_KV * (2 * gp + hi)
            bias = bias_s[table, h0:h0 + Q_PER_KV].reshape(Q_PER_KV * CHUNK, KEY_PAD)
            s = _dot_nt(q4, kn) + bias
            p = jnp.exp2(s - jnp.max(s, axis=-1, keepdims=True))
            o = _dot(p.astype(BF16), vn) * (1.0 / jnp.sum(p, axis=-1, keepdims=True))
            out = o if out is None else jnp.where(low_half, out, o)
        for r, cols in enumerate(pair_cols):
            yb_ref[rows, cols] = out[r * CHUNK:(r + 1) * CHUNK].astype(BF16)

    for j in range(2 * KV_WIDTH // PIECE):
        piece_kv(j)
    for j in range(Q_WIDTH // PIECE):
        piece_q(j)
    for j in range(A_WIDTH // PIECE):
        piece_v(j)
    piece_v_norm()
    for j in range(A_WIDTH // PIECE):
        piece_u(j)
    for j in range(2 * D_MODEL // PIECE):
        piece_gate(j)
    for c in range(0, n_chunks, 2):
        for g in range(A_GROUPS):
            unit_spatial(c, g)
    for c in range(n_chunks):
        for gp in range(N_KV_HEADS // 2):
            unit_attention(c, gp)

    kv_s[0:CHUNK, :] = kv_s[ROW_TILE:ROW_TILE + CHUNK, :]


def _prompt_front(sinks, x, w_in, w_q, ln_g, ln_b, w_s, b_s_t, bias, layer, batch, seq):
    n_tiles = seq // ROW_TILE
    tok_spec = lambda width: pl.BlockSpec((ROW_TILE, width), lambda b, i: (b * n_tiles + i, 0))
    vec_spec = pl.BlockSpec((None, 1, A_WIDTH), lambda b, i: (layer, 0, 0))
    rows = batch * seq
    return pl.pallas_call(
        _prompt_front_kernel,
        out_shape=[jax.ShapeDtypeStruct((rows, A_WIDTH), BF16),
                   jax.ShapeDtypeStruct((rows, Q_WIDTH), BF16),
                   jax.ShapeDtypeStruct((rows, 2 * D_MODEL), BF16),
                   jax.ShapeDtypeStruct((batch, WINDOW, 2 * KV_WIDTH), F32)],
        grid=(batch, n_tiles),
        in_specs=[pl.BlockSpec(memory_space=pltpu.SMEM),
                  tok_spec(D_MODEL),
                  _resident((None, D_MODEL, IN_WIDTH), lambda b, i: (layer, 0, 0)),
                  _resident((None, D_MODEL, Q_WIDTH), lambda b, i: (layer, 0, 0)),
                  vec_spec, vec_spec,
                  _resident((None, A_GROUPS, CHUNK, CHUNK), lambda b, i: (layer, 0, 0, 0)),
                  _resident((None, CHUNK, A_GROUPS), lambda b, i: (layer, 0, 0)),
                  _resident((2, N_HEADS, WINDOW, KEY_PAD), lambda b, i: (0, 0, 0, 0))],
        out_specs=[tok_spec(A_WIDTH), tok_spec(Q_WIDTH), tok_spec(2 * D_MODEL),
                   pl.BlockSpec((None, WINDOW, 2 * KV_WIDTH), lambda b, i: (b, 0, 0))],
        scratch_shapes=[pltpu.VMEM((ROW_TILE, A_WIDTH), BF16),
                        pltpu.VMEM((ROW_TILE, A_WIDTH), BF16),
                        pltpu.VMEM((ROW_TILE, Q_WIDTH), BF16),
                        pltpu.VMEM((CHUNK + ROW_TILE, 2 * KV_WIDTH), BF16),
                        pltpu.VMEM((ROW_TILE, A_WIDTH), F32),
                        pltpu.VMEM((2, N_HEADS, WINDOW, KEY_PAD), F32)],
        compiler_params=_params(2),
        name="prompt_front",
    )(sinks, x, w_in, w_q, ln_g, ln_b, w_s, b_s_t, bias)


def _inproj_sample_kernel(x_ref, w_ref, wq_ref, g_ref, b_ref, u_ref, va_ref, q_ref, gate_ref, kvt_ref):
    xb = x_ref[...].astype(BF16)

    def proj(c0, c1):
        return _dot(xb, w_ref[:, c0:c1])

    half = A_WIDTH // 2
    for c0 in range(O_U, O_V, half):
        u_ref[:, c0:c0 + half] = _gelu(proj(c0, c0 + half)).astype(BF16)
    va_ref[...] = _layer_norm(_gelu(proj(O_V, O_Q)), g_ref[...], b_ref[...])
    for c0 in range(0, Q_WIDTH, half):
        q_ref[:, c0:c0 + half] = (_dot(xb, wq_ref[:, c0:c0 + half]) * (HEAD_DIM ** -0.5)).astype(BF16)
    kvt_ref[...] = proj(O_K, O_G).T
    for c0 in range(0, 2 * D_MODEL, half):
        gate_ref[:, c0:c0 + half] = jax.nn.sigmoid(proj(O_G + c0, O_G + c0 + half)).astype(BF16)


def _inproj_sample(x, w_in, w_q, ln_g, ln_b, layer):
    rows = x.shape[0]
    row_spec = lambda width: pl.BlockSpec((ROW_TILE, width), lambda i: (i, 0))
    vec_spec = pl.BlockSpec((None, 1, A_WIDTH), lambda i: (layer, 0, 0))
    return pl.pallas_call(
        _inproj_sample_kernel,
        out_shape=[jax.ShapeDtypeStruct((rows, A_WIDTH), BF16),
                   jax.ShapeDtypeStruct((rows, A_WIDTH), F32),
                   jax.ShapeDtypeStruct((rows, Q_WIDTH), BF16),
                   jax.ShapeDtypeStruct((rows, 2 * D_MODEL), BF16),
                   jax.ShapeDtypeStruct((2 * KV_WIDTH, rows), F32)],
        grid=(rows // ROW_TILE,),
        in_specs=[row_spec(D_MODEL),
                  _resident((None, D_MODEL, IN_WIDTH), lambda i: (layer, 0, 0)),
                  _resident((None, D_MODEL, Q_WIDTH), lambda i: (layer, 0, 0)),
                  vec_spec, vec_spec],
        out_specs=[row_spec(A_WIDTH), row_spec(A_WIDTH), row_spec(Q_WIDTH), row_spec(2 * D_MODEL),
                   pl.BlockSpec((2 * KV_WIDTH, ROW_TILE), lambda i: (0, i))],
        compiler_params=_params(1),
        name="inproj_sample",
    )(x, w_in, w_q, ln_g, ln_b)


def _sample_mix_kernel(sinks_ref, u_ref, va_ref, q_ref, kvt_ref, ck_ref, cv_ref, wexp_ref, bsexp_ref,
                       biasc_ref, biasn_ref, prev_k_ref, prev_v_ref, ya_ref, yb_ref, ko_ref, vo_ref, yb_acc):
    del prev_k_ref, prev_v_ref
    n_new = SUBLANES
    rows = SAMPLE_SEQS * n_new

    va3 = va_ref[...].reshape(SAMPLE_SEQS, n_new, A_WIDTH)
    t_idx = lax.broadcasted_iota(jnp.int32, (n_new, A_WIDTH), 0)
    mixed = jnp.broadcast_to(bsexp_ref[...][None], va3.shape)
    for s in range(n_new):
        w = jnp.where(t_idx >= s, wexp_ref[s], 0.0)
        mixed = mixed + w[None] * va3[:, s:s + 1, :]
    u3 = u_ref[...].astype(F32).reshape(SAMPLE_SEQS, n_new, A_WIDTH)
    ya_ref[...] = (u3 * mixed).reshape(rows, A_WIDTH).astype(BF16)

    group_of_lane = lax.broadcasted_iota(jnp.int32, (1, KV_WIDTH), 1) // HEAD_DIM
    row = lax.broadcasted_iota(jnp.int32, (N_HEADS * n_new, 1), 0)
    head_of_row = Q_PER_KV * ((row // n_new) % N_KV_HEADS) + row // (n_new * N_KV_HEADS)
    sink = jnp.zeros((N_HEADS * n_new, 1), F32)
    for h in range(N_HEADS):
        sink = jnp.where(head_of_row == h, sinks_ref[h], sink)
    lane = lax.broadcasted_iota(jnp.int32, (1, LANES), 1)
    seq_of_lane = lane // n_new
    keep_old = lane < WINDOW - n_new
    q32 = q_ref[...].astype(F32)
    k_new, v_new = kvt_ref[0:KV_WIDTH, :], kvt_ref[KV_WIDTH:2 * KV_WIDTH, :]
    k_new_b, v_new_b = k_new.astype(BF16), v_new.astype(BF16)
    bias_c, bias_n = biasc_ref[...], biasn_ref[...]
    for b in range(SAMPLE_SEQS):
        new = slice(b * n_new, (b + 1) * n_new)
        pieces = []
        for r in range(Q_PER_KV):
            blk = q32[new, r * KV_WIDTH:(r + 1) * KV_WIDTH]
            for g in range(N_KV_HEADS):
                pieces.append(jnp.where(group_of_lane == g, blk, 0.0))
        q_rows = jnp.concatenate(pieces, axis=0).astype(BF16)
        k_old, v_old = ck_ref[b], cv_ref[b]
        k_all = jnp.concatenate([k_old.astype(BF16), k_new_b], axis=1)
        v_all = jnp.concatenate([v_old.astype(BF16), v_new_b], axis=1)
        bias = jnp.concatenate([bias_c, jnp.where(seq_of_lane == b, bias_n, NEG_INF)], axis=1)
        s = _dot(q_rows, k_all) + bias
        m = jnp.maximum(jnp.max(s, axis=-1, keepdims=True), sink)
        p = jnp.exp(s - m)
        denom = jnp.sum(p, axis=-1, keepdims=True) + jnp.exp(sink - m)
        o = _dot_nt(p.astype(BF16), v_all) * (1.0 / denom)
        for r in range(Q_PER_KV):
            acc = jnp.zeros((n_new, KV_WIDTH), F32)
            for g in range(N_KV_HEADS):
                r0 = (r * N_KV_HEADS + g) * n_new
                acc = jnp.where(group_of_lane == g, o[r0:r0 + n_new, :], acc)
            yb_acc[new, r * KV_WIDTH:(r + 1) * KV_WIDTH] = acc
        shift_new = (WINDOW - n_new - b * n_new) % LANES
        ko_ref[b] = jnp.where(keep_old, pltpu.roll(k_old, WINDOW - n_new, 1), pltpu.roll(k_new, shift_new, 1))
        vo_ref[b] = jnp.where(keep_old, pltpu.roll(v_old, WINDOW - n_new, 1), pltpu.roll(v_new, shift_new, 1))
    yb_ref[...] = yb_acc[...].astype(BF16)


def _sample_mix(sinks, u, va32, q, kvt, cache_k, cache_v, wexp, bsexp, bias_c, bias_n, prev, layer):
    n_seq = cache_k.shape[1]
    rows = SAMPLE_SEQS * SUBLANES
    row_spec = lambda width: pl.BlockSpec((rows, width), lambda i: (i, 0))
    cache_spec = pl.BlockSpec((None, SAMPLE_SEQS, KV_WIDTH, WINDOW), lambda i: (layer, i, 0, 0))
    table_spec = pl.BlockSpec((N_HEADS * SUBLANES, LANES), lambda i: (0, 0))
    in_specs = [pl.BlockSpec(memory_space=pltpu.SMEM),
                row_spec(A_WIDTH), row_spec(A_WIDTH), row_spec(Q_WIDTH),
                pl.BlockSpec((2 * KV_WIDTH, rows), lambda i: (0, i)),
                cache_spec, cache_spec,
                pl.BlockSpec((None, SUBLANES, SUBLANES, A_WIDTH), lambda i: (layer, 0, 0, 0)),
                pl.BlockSpec((None, SUBLANES, A_WIDTH), lambda i: (layer, 0, 0)),
                table_spec, table_spec,
                pl.BlockSpec(memory_space=pl.ANY), pl.BlockSpec(memory_space=pl.ANY)]
    operands = [sinks, u, va32, q, kvt, cache_k, cache_v, wexp, bsexp, bias_c, bias_n, *prev]
    return pl.pallas_call(
        _sample_mix_kernel,
        out_shape=[jax.ShapeDtypeStruct((n_seq * SUBLANES, A_WIDTH), BF16),
                   jax.ShapeDtypeStruct((n_seq * SUBLANES, Q_WIDTH), BF16),
                   jax.ShapeDtypeStruct(cache_k.shape, F32),
                   jax.ShapeDtypeStruct(cache_v.shape, F32)],
        grid=(n_seq // SAMPLE_SEQS,),
        in_specs=in_specs,
        out_specs=[row_spec(A_WIDTH), row_spec(Q_WIDTH), cache_spec, cache_spec],
        scratch_shapes=[pltpu.VMEM((rows, Q_WIDTH), F32)],
        input_output_aliases={len(operands) - 2: 2, len(operands) - 1: 3},
        compiler_params=_params(1),
        name="sample_mix",
    )(*operands)


def _merge_ffn_kernel(x_ref, ya_ref, yb_ref, gate_ref, wpa_ref, wpb_ref, wo_ref, wg_ref, wu_ref, wd_ref,
                      ln_ref, out_ref, *, alpha):
    g_a = gate_ref[:, 0:D_MODEL].astype(F32)
    g_b = gate_ref[:, D_MODEL:2 * D_MODEL].astype(F32)
    merged = g_a * _dot(ya_ref[...], wpa_ref[...]) + g_b * _dot(yb_ref[...], wpb_ref[...])
    mix = _dot(merged.astype(BF16), wo_ref[...])
    x1 = _layer_norm(alpha * x_ref[...] + mix, ln_ref[0:1, :], ln_ref[1:2, :])
    x1b = x1.astype(BF16)
    act = (jax.nn.silu(_dot(x1b, wg_ref[...])) * _dot(x1b, wu_ref[...])).astype(BF16)
    ffn = _dot(act, wd_ref[...])
    out_ref[...] = _layer_norm(alpha * x1 + ffn, ln_ref[2:3, :], ln_ref[3:4, :])


def _merge_ffn(x, ya, yb, gates, w_pa, w_pb, w_o, w_gate, w_up, w_down, ln_pack, layer, alpha):
    rows = x.shape[0]
    row_spec = lambda width: pl.BlockSpec((ROW_TILE, width), lambda i: (i, 0))
    weight = lambda k, n: _resident((None, k, n), lambda i: (layer, 0, 0))
    return pl.pallas_call(
        functools.partial(_merge_ffn_kernel, alpha=alpha),
        out_shape=jax.ShapeDtypeStruct((rows, D_MODEL), F32),
        grid=(rows // ROW_TILE,),
        in_specs=[row_spec(D_MODEL), row_spec(A_WIDTH), row_spec(Q_WIDTH), row_spec(2 * D_MODEL),
                  weight(A_WIDTH, D_MODEL), weight(Q_WIDTH, D_MODEL), weight(D_MODEL, D_MODEL),
                  weight(D_MODEL, D_FF), weight(D_MODEL, D_FF), weight(D_FF, D_MODEL),
                  pl.BlockSpec((None, 4, D_MODEL), lambda i: (layer, 0, 0))],
        out_specs=row_spec(D_MODEL),
        compiler_params=_params(1),
        name="merge_ffn",
    )(x, ya, yb, gates, w_pa, w_pb, w_o, w_gate, w_up, w_down, ln_pack)


def kernel(x_prompt, x_sample, cache_swa_k, cache_swa_v, rel_bias, w_in, ln_v_g, ln_v_b, w_s, b_s,
           sinks, w_pa, w_pb, w_o, ln1_g, ln1_b, w_gate, w_up, w_down, ln2_g, ln2_b):
    depth = w_in.shape[0]
    batch, seq, _ = x_prompt.shape
    n_seq, n_new, _ = x_sample.shape
    assert n_new == SUBLANES and seq % ROW_TILE == 0 and n_seq % SAMPLE_SEQS == 0
    assert (n_seq * n_new) % ROW_TILE == 0 and (ROW_TILE // CHUNK) % 2 == 0
    alpha = (2 * depth) ** 0.25

    def heads_rg(w, axis):
        shape = w.shape
        w = w.reshape(shape[:axis] + (N_KV_HEADS, Q_PER_KV, HEAD_DIM) + shape[axis + 1:])
        return jnp.swapaxes(w, axis, axis + 1).reshape(shape)

    w_in_b = w_in.astype(BF16)
    w_q_b = heads_rg(w_in[..., O_Q:O_K], 2).astype(BF16)
    w_pa_b, w_o_b = w_pa.astype(BF16), w_o.astype(BF16)
    w_pb_b = heads_rg(w_pb, 1).astype(BF16)
    w_gate_b, w_up_b, w_down_b = w_gate.astype(BF16), w_up.astype(BF16), w_down.astype(BF16)
    ln_v_g3, ln_v_b3 = ln_v_g[:, None, :], ln_v_b[:, None, :]
    ln_pack = jnp.stack([ln1_g, ln1_b, ln2_g, ln2_b], axis=1)
    b_s_t = jnp.swapaxes(b_s, 1, 2)
    wexp = jnp.repeat(jnp.transpose(w_s[:, :, :n_new, :n_new], (0, 3, 2, 1)), A_WIDTH // A_GROUPS, axis=-1)
    bsexp = jnp.repeat(jnp.swapaxes(b_s[:, :, :n_new], 1, 2), A_WIDTH // A_GROUPS, axis=-1)

    bias_p = _bias_tables(rel_bias, jnp.stack([_masked_buckets(WINDOW, KEY_PAD, False),
                                               _masked_buckets(WINDOW, KEY_PAD, True)]))
    bias_s = _bias_tables(rel_bias, _masked_buckets(n_new, WINDOW + n_new, False)[None])[0]
    bias_s = bias_s.reshape(N_KV_HEADS, Q_PER_KV, n_new, KEY_PAD)
    bias_s = jnp.swapaxes(bias_s, 0, 1).reshape(N_HEADS * n_new, KEY_PAD)
    bias_c = bias_s[:, :WINDOW]
    bias_n = jnp.tile(bias_s[:, WINDOW:WINDOW + n_new], (1, SAMPLE_SEQS))

    cache_k = jnp.transpose(cache_swa_k, (0, 1, 3, 4, 2)).reshape(depth, n_seq, KV_WIDTH, WINDOW)
    cache_v = jnp.transpose(cache_swa_v, (0, 1, 3, 4, 2)).reshape(depth, n_seq, KV_WIDTH, WINDOW)

    xp = x_prompt.reshape(batch * seq, D_MODEL)
    xs = x_sample.reshape(n_seq * n_new, D_MODEL)
    kp_l, vp_l, ga_l = [], [], []
    new_cache = [jnp.zeros(cache_k.shape, F32), jnp.zeros(cache_v.shape, F32)]
    for l in range(depth):
        ya, yb, gates, kv_tail = _prompt_front(sinks[l], xp, w_in_b, w_q_b, ln_v_g3, ln_v_b3, w_s, b_s_t,
                                               bias_p, l, batch, seq)
        xp = _merge_ffn(xp, ya, yb, gates, w_pa_b, w_pb_b, w_o_b, w_gate_b, w_up_b, w_down_b,
                        ln_pack, l, alpha)
        kp_l.append(kv_tail[..., :KV_WIDTH].reshape(batch, WINDOW, N_KV_HEADS, HEAD_DIM))
        vp_l.append(kv_tail[..., KV_WIDTH:].reshape(batch, WINDOW, N_KV_HEADS, HEAD_DIM))

        u, va32, q, gates, kvt = _inproj_sample(xs, w_in_b, w_q_b, ln_v_g3, ln_v_b3, l)
        ya, yb, *new_cache = _sample_mix(sinks[l], u, va32, q, kvt, cache_k, cache_v, wexp, bsexp,
                                         bias_c, bias_n, new_cache, l)
        xs = _merge_ffn(xs, ya, yb, gates, w_pa_b, w_pb_b, w_o_b, w_gate_b, w_up_b, w_down_b,
                        ln_pack, l, alpha)
        ga_l.append(va32.reshape(n_seq, n_new, A_WIDTH))

    def window_major(c):
        c = c.reshape(depth, n_seq, N_KV_HEADS, HEAD_DIM, WINDOW)
        return jnp.transpose(c, (0, 1, 4, 2, 3))

    return (xp.reshape(batch, seq, D_MODEL), xs.reshape(n_seq, n_new, D_MODEL),
            jnp.stack(kp_l), jnp.stack(vp_l), window_major(new_cache[0]), window_major(new_cache[1]),
            jnp.stack(ga_l))
```

```python
import functools
import math

import jax
import jax.numpy as jnp
from jax import lax
from jax.experimental import pallas as pl
from jax.experimental.pallas import tpu as pltpu

D_MODEL = 1024
CHUNK = 128
A_WIDTH = D_MODEL
A_GROUPS = 8
N_HEADS = 16
HEAD_DIM = 64
N_KV_HEADS = 4
Q_PER_KV = N_HEADS // N_KV_HEADS
WINDOW = 128
N_BUCKETS = 32
MAX_DISTANCE = 128
D_FF = 2816
LN_EPS = 1e-5
NEG_INF = -1e30
LOG2_E = math.log2(math.e)

KV_WIDTH = N_KV_HEADS * HEAD_DIM
Q_WIDTH = N_HEADS * HEAD_DIM
O_U = 0
O_V = O_U + A_WIDTH
O_Q = O_V + A_WIDTH
O_K = O_Q + Q_WIDTH
O_G = O_K + 2 * KV_WIDTH
IN_WIDTH = O_G + 2 * D_MODEL

LANES = 128
SUBLANES = 8
BF16_ROWS = 16
ROW_TILE = 512
PIECE = 512
SAMPLE_SEQS = LANES // SUBLANES
KEY_PAD = 2 * WINDOW
VMEM_LIMIT = 56 * 1024 * 1024

BF16 = jnp.bfloat16
F32 = jnp.float32


def _layer_norm(x, g, b):
    mu = jnp.mean(x, axis=-1, keepdims=True)
    xc = x - mu
    var = jnp.mean(xc * xc, axis=-1, keepdims=True)
    return xc * lax.rsqrt(var + LN_EPS) * g + b


def _gelu(x):
    return jax.nn.gelu(x, approximate=True)


def _dot(a, b):
    return jnp.dot(a, b, preferred_element_type=F32)


def _dot_nt(a, b):
    return lax.dot_general(a, b, (((1,), (1,)), ((), ())), preferred_element_type=F32)


def _resident(block_shape, index_map):
    return pl.BlockSpec(block_shape, index_map, pipeline_mode=pl.Buffered(1))


def _params(n_axes):
    return pltpu.CompilerParams(dimension_semantics=("arbitrary",) * n_axes,
                                vmem_limit_bytes=VMEM_LIMIT)


def _bias_kernel(rb_ref, bucket_ref, out_ref):
    bk = bucket_ref[...]
    for h in range(N_HEADS):
        acc = jnp.full(bk.shape, NEG_INF, F32)
        for b in range(N_BUCKETS):
            acc = jnp.where(bk == b, rb_ref[b, h], acc)
        out_ref[h] = acc


def _bias_tables(rel_bias, buckets):
    n, t, kp = buckets.shape
    return pl.pallas_call(
        _bias_kernel,
        out_shape=jax.ShapeDtypeStruct((n, N_HEADS, t, kp), F32),
        grid=(n,),
        in_specs=[pl.BlockSpec(memory_space=pltpu.SMEM),
                  pl.BlockSpec((None, t, kp), lambda i: (i, 0, 0))],
        out_specs=pl.BlockSpec((None, N_HEADS, t, kp), lambda i: (i, 0, 0, 0)),
        compiler_params=_params(1),
        name="bias_tables",
    )(rel_bias, buckets)


def _rel_bucket(dist):
    n = jnp.maximum(dist, 0)
    max_exact = N_BUCKETS // 2
    nf = jnp.maximum(n, 1).astype(F32)
    large = max_exact + (jnp.log(nf / max_exact) / math.log(MAX_DISTANCE / max_exact)
                         * (N_BUCKETS - max_exact)).astype(jnp.int32)
    large = jnp.minimum(large, N_BUCKETS - 1)
    return jnp.where(n < max_exact, n, large)


def _masked_buckets(n_q, n_keys, first_block):
    qi = jnp.arange(n_q, dtype=jnp.int32)[:, None]
    kj = jnp.arange(KEY_PAD, dtype=jnp.int32)[None, :]
    dist = qi + WINDOW - kj
    ok = (dist >= 0) & (dist < WINDOW) & (kj < n_keys)
    if first_block:
        ok = ok & (kj >= WINDOW)
    return jnp.where(ok, _rel_bucket(dist), -1)


def _prompt_front_kernel(sinks_ref, x_ref, w_ref, wq_ref, g_ref, b_ref, ws_ref, bs_ref, bias_ref,
                         ya_ref, yb_ref, gate_ref, tail_ref, u_s, va_s, q_s, kv_s, hv_s, bias_s):
    n_chunks = ROW_TILE // CHUNK
    first_tile = pl.program_id(1) == 0

    lane = lax.broadcasted_iota(jnp.int32, (1, LANES), 1)

    @pl.when(jnp.logical_and(pl.program_id(0) == 0, first_tile))
    def _():
        kv_s[0:CHUNK, :] = jnp.zeros((CHUNK, 2 * KV_WIDTH), BF16)
        for table in range(2):
            for h in range(N_HEADS):
                bias_s[table, h, :, 0:LANES] = LOG2_E * jnp.where(lane == 0, sinks_ref[h],
                                                                  bias_ref[table, h, :, 0:LANES])
                bias_s[table, h, :, LANES:] = LOG2_E * bias_ref[table, h, :, LANES:]

    xb = x_ref[...].astype(BF16)
    tri = (lax.broadcasted_iota(jnp.int32, (CHUNK, CHUNK), 0)
           >= lax.broadcasted_iota(jnp.int32, (CHUNK, CHUNK), 1))
    low_half = lane < HEAD_DIM
    zero = jnp.zeros((), BF16)

    def piece_kv(j):
        cols = slice(j * PIECE, (j + 1) * PIECE)
        kv = _dot(xb, w_ref[:, O_K + j * PIECE:O_K + (j + 1) * PIECE])
        kv_s[CHUNK:CHUNK + ROW_TILE, cols] = kv.astype(BF16)
        tail_ref[:, cols] = kv[ROW_TILE - WINDOW:, :]

    def piece_q(j):
        cols = slice(j * PIECE, (j + 1) * PIECE)
        q_s[:, cols] = (_dot(xb, wq_ref[:, cols]) * (LOG2_E * HEAD_DIM ** -0.5)).astype(BF16)

    def piece_v(j):
        cols = slice(j * PIECE, (j + 1) * PIECE)
        hv_s[:, cols] = _gelu(_dot(xb, w_ref[:, O_V + j * PIECE:O_V + (j + 1) * PIECE]))

    def piece_v_norm():
        va_s[...] = _layer_norm(hv_s[...], g_ref[...], b_ref[...]).astype(BF16)

    def piece_u(j):
        cols = slice(j * PIECE, (j + 1) * PIECE)
        u_s[:, cols] = _gelu(_dot(xb, w_ref[:, O_U + j * PIECE:O_U + (j + 1) * PIECE])).astype(BF16)

    def piece_gate(j):
        cols = slice(j * PIECE, (j + 1) * PIECE)
        gate_ref[:, cols] = jax.nn.sigmoid(
            _dot(xb, w_ref[:, O_G + j * PIECE:O_G + (j + 1) * PIECE])).astype(BF16)

    def unit_spatial(c, g):
        rows, cols = slice(c * CHUNK, (c + 1) * CHUNK), slice(g * LANES, (g + 1) * LANES)
        w = jnp.where(tri, ws_ref[g], 0.0).astype(BF16)
        mixed = _dot(w, va_s[rows, cols]) + bs_ref[:, g:g + 1]
        ya_ref[rows, cols] = (u_s[rows, cols].astype(F32) * mixed).astype(BF16)

    def band_operands(c, gp, hi):
        band = slice(c * CHUNK, (c + 2) * CHUNK)
        keep = low_half if hi == 0 else jnp.logical_not(low_half)
        kn = jnp.where(keep, kv_s[band, gp * LANES:(gp + 1) * LANES], zero)
        vn = jnp.where(keep, kv_s[band, KV_WIDTH + gp * LANES:KV_WIDTH + (gp + 1) * LANES], zero)
        not_sink = lax.broadcasted_iota(jnp.int32, (BF16_ROWS, 1), 0) > 0
        kn = jnp.concatenate([jnp.where(not_sink, kn[:BF16_ROWS], zero), kn[BF16_ROWS:]], axis=0)
        vn = jnp.concatenate([jnp.where(not_sink, vn[:BF16_ROWS], zero), vn[BF16_ROWS:]], axis=0)
        return kn, vn

    def unit_attention(c, gp):
        rows = slice(c * CHUNK, (c + 1) * CHUNK)
        table = jnp.where(first_tile, 1, 0) if c == 0 else 0
        pair_cols = [slice((r * 2 + gp) * LANES, (r * 2 + gp + 1) * LANES) for r in range(Q_PER_KV)]
        q4 = jnp.concatenate([q_s[rows, cols] for cols in pair_cols], axis=0)
        out = None
        for hi in range(2):
            kn, vn = band_operands(c, gp, hi)
            h0 = Q_PER_KV * (2 * gp + hi)
            bias = bias_s[table, h0:h0 + Q_PER_KV].reshape(Q_PER_KV * CHUNK, KEY_PAD)
            s = _dot_nt(q4, kn) + bias
            p = jnp.exp2(s - jnp.max(s, axis=-1, keepdims=True))
            o = _dot(p.astype(BF16), vn) * (1.0 / jnp.sum(p, axis=-1, keepdims=True))
            out = o if out is None else jnp.where(low_half, out, o)
        for r, cols in enumerate(pair_cols):
            yb_ref[rows, cols] = out[r * CHUNK:(r + 1) * CHUNK].astype(BF16)

    for j in range(2 * KV_WIDTH // PIECE):
        piece_kv(j)
    for j in range(Q_WIDTH // PIECE):
        piece_q(j)
    for j in range(A_WIDTH // PIECE):
        piece_v(j)
    piece_v_norm()
    for j in range(A_WIDTH // PIECE):
        piece_u(j)
    for j in range(2 * D_MODEL // PIECE):
        piece_gate(j)
    for c in range(n_chunks):
        for g in range(A_GROUPS):
            unit_spatial(c, g)
        for gp in range(N_KV_HEADS // 2):
            unit_attention(c, gp)

    kv_s[0:CHUNK, :] = kv_s[ROW_TILE:ROW_TILE + CHUNK, :]


def _prompt_front(sinks, x, w_in, w_q, ln_g, ln_b, w_s, b_s_t, bias, layer, batch, seq):
    n_tiles = seq // ROW_TILE
    tok_spec = lambda width: pl.BlockSpec((ROW_TILE, width), lambda b, i: (b * n_tiles + i, 0))
    vec_spec = pl.BlockSpec((None, 1, A_WIDTH), lambda b, i: (layer, 0, 0))
    rows = batch * seq
    return pl.pallas_call(
        _prompt_front_kernel,
        out_shape=[jax.ShapeDtypeStruct((rows, A_WIDTH), BF16),
                   jax.ShapeDtypeStruct((rows, Q_WIDTH), BF16),
                   jax.ShapeDtypeStruct((rows, 2 * D_MODEL), BF16),
                   jax.ShapeDtypeStruct((batch, WINDOW, 2 * KV_WIDTH), F32)],
        grid=(batch, n_tiles),
        in_specs=[pl.BlockSpec(memory_space=pltpu.SMEM),
                  tok_spec(D_MODEL),
                  _resident((None, D_MODEL, IN_WIDTH), lambda b, i: (layer, 0, 0)),
                  _resident((None, D_MODEL, Q_WIDTH), lambda b, i: (layer, 0, 0)),
                  vec_spec, vec_spec,
                  _resident((None, A_GROUPS, CHUNK, CHUNK), lambda b, i: (layer, 0, 0, 0)),
                  _resident((None, CHUNK, A_GROUPS), lambda b, i: (layer, 0, 0)),
                  _resident((2, N_HEADS, WINDOW, KEY_PAD), lambda b, i: (0, 0, 0, 0))],
        out_specs=[tok_spec(A_WIDTH), tok_spec(Q_WIDTH), tok_spec(2 * D_MODEL),
                   pl.BlockSpec((None, WINDOW, 2 * KV_WIDTH), lambda b, i: (b, 0, 0))],
        scratch_shapes=[pltpu.VMEM((ROW_TILE, A_WIDTH), BF16),
                        pltpu.VMEM((ROW_TILE, A_WIDTH), BF16),
                        pltpu.VMEM((ROW_TILE, Q_WIDTH), BF16),
                        pltpu.VMEM((CHUNK + ROW_TILE, 2 * KV_WIDTH), BF16),
                        pltpu.VMEM((ROW_TILE, A_WIDTH), F32),
                        pltpu.VMEM((2, N_HEADS, WINDOW, KEY_PAD), F32)],
        compiler_params=_params(2),
        name="prompt_front",
    )(sinks, x, w_in, w_q, ln_g, ln_b, w_s, b_s_t, bias)


def _inproj_sample_kernel(x_ref, w_ref, wq_ref, g_ref, b_ref, u_ref, va_ref, q_ref, gate_ref, kvt_ref):
    xb = x_ref[...].astype(BF16)

    def proj(c0, c1):
        return _dot(xb, w_ref[:, c0:c1])

    half = A_WIDTH // 2
    for c0 in range(O_U, O_V, half):
        u_ref[:, c0:c0 + half] = _gelu(proj(c0, c0 + half)).astype(BF16)
    va_ref[...] = _layer_norm(_gelu(proj(O_V, O_Q)), g_ref[...], b_ref[...])
    for c0 in range(0, Q_WIDTH, half):
        q_ref[:, c0:c0 + half] = (_dot(xb, wq_ref[:, c0:c0 + half]) * (HEAD_DIM ** -0.5)).astype(BF16)
    kvt_ref[...] = proj(O_K, O_G).T
    for c0 in range(0, 2 * D_MODEL, half):
        gate_ref[:, c0:c0 + half] = jax.nn.sigmoid(proj(O_G + c0, O_G + c0 + half)).astype(BF16)


def _inproj_sample(x, first_tile, rows, w_in, w_q, ln_g, ln_b, layer):
    row_spec = lambda width: pl.BlockSpec((ROW_TILE, width), lambda i: (i, 0))
    vec_spec = pl.BlockSpec((None, 1, A_WIDTH), lambda i: (layer, 0, 0))
    return pl.pallas_call(
        _inproj_sample_kernel,
        out_shape=[jax.ShapeDtypeStruct((rows, A_WIDTH), BF16),
                   jax.ShapeDtypeStruct((rows, A_WIDTH), F32),
                   jax.ShapeDtypeStruct((rows, Q_WIDTH), BF16),
                   jax.ShapeDtypeStruct((rows, 2 * D_MODEL), BF16),
                   jax.ShapeDtypeStruct((2 * KV_WIDTH, rows), F32)],
        grid=(rows // ROW_TILE,),
        in_specs=[pl.BlockSpec((ROW_TILE, D_MODEL), lambda i: (first_tile + i, 0)),
                  _resident((None, D_MODEL, IN_WIDTH), lambda i: (layer, 0, 0)),
                  _resident((None, D_MODEL, Q_WIDTH), lambda i: (layer, 0, 0)),
                  vec_spec, vec_spec],
        out_specs=[row_spec(A_WIDTH), row_spec(A_WIDTH), row_spec(Q_WIDTH), row_spec(2 * D_MODEL),
                   pl.BlockSpec((2 * KV_WIDTH, ROW_TILE), lambda i: (0, i))],
        compiler_params=_params(1),
        name="inproj_sample",
    )(x, w_in, w_q, ln_g, ln_b)


def _sample_mix_kernel(sinks_ref, u_ref, va_ref, q_ref, kvt_ref, ck_ref, cv_ref, wexp_ref, bsexp_ref,
                       biasc_ref, biasn_ref, prev_k_ref, prev_v_ref, ya_ref, yb_ref, ko_ref, vo_ref, yb_acc):
    del prev_k_ref, prev_v_ref
    n_new = SUBLANES
    rows = SAMPLE_SEQS * n_new

    va3 = va_ref[...].reshape(SAMPLE_SEQS, n_new, A_WIDTH)
    t_idx = lax.broadcasted_iota(jnp.int32, (n_new, A_WIDTH), 0)
    mixed = jnp.broadcast_to(bsexp_ref[...][None], va3.shape)
    for s in range(n_new):
        w = jnp.where(t_idx >= s, wexp_ref[s], 0.0)
        mixed = mixed + w[None] * va3[:, s:s + 1, :]
    u3 = u_ref[...].astype(F32).reshape(SAMPLE_SEQS, n_new, A_WIDTH)
    ya_ref[...] = (u3 * mixed).reshape(rows, A_WIDTH).astype(BF16)

    group_of_lane = lax.broadcasted_iota(jnp.int32, (1, KV_WIDTH), 1) // HEAD_DIM
    row = lax.broadcasted_iota(jnp.int32, (N_HEADS * n_new, 1), 0)
    head_of_row = Q_PER_KV * ((row // n_new) % N_KV_HEADS) + row // (n_new * N_KV_HEADS)
    sink = jnp.zeros((N_HEADS * n_new, 1), F32)
    for h in range(N_HEADS):
        sink = jnp.where(head_of_row == h, sinks_ref[h], sink)
    lane = lax.broadcasted_iota(jnp.int32, (1, LANES), 1)
    seq_of_lane = lane // n_new
    keep_old = lane < WINDOW - n_new
    q32 = q_ref[...].astype(F32)
    k_new, v_new = kvt_ref[0:KV_WIDTH, :], kvt_ref[KV_WIDTH:2 * KV_WIDTH, :]
    k_new_b, v_new_b = k_new.astype(BF16), v_new.astype(BF16)
    bias_c, bias_n = biasc_ref[...], biasn_ref[...]
    for b in range(SAMPLE_SEQS):
        new = slice(b * n_new, (b + 1) * n_new)
        pieces = []
        for r in range(Q_PER_KV):
            blk = q32[new, r * KV_WIDTH:(r + 1) * KV_WIDTH]
            for g in range(N_KV_HEADS):
                pieces.append(jnp.where(group_of_lane == g, blk, 0.0))
        q_rows = jnp.concatenate(pieces, axis=0).astype(BF16)
        k_old, v_old = ck_ref[b], cv_ref[b]
        k_all = jnp.concatenate([k_old.astype(BF16), k_new_b], axis=1)
        v_all = jnp.concatenate([v_old.astype(BF16), v_new_b], axis=1)
        bias = jnp.concatenate([bias_c, jnp.where(seq_of_lane == b, bias_n, NEG_INF)], axis=1)
        s = _dot(q_rows, k_all) + bias
        m = jnp.maximum(jnp.max(s, axis=-1, keepdims=True), sink)
        p = jnp.exp(s - m)
        denom = jnp.sum(p, axis=-1, keepdims=True) + jnp.exp(sink - m)
        o = _dot_nt(p.astype(BF16), v_all) * (1.0 / denom)
        for r in range(Q_PER_KV):
            acc = jnp.zeros((n_new, KV_WIDTH), F32)
            for g in range(N_KV_HEADS):
                r0 = (r * N_KV_HEADS + g) * n_new
                acc = jnp.where(group_of_lane == g, o[r0:r0 + n_new, :], acc)
            yb_acc[new, r * KV_WIDTH:(r + 1) * KV_WIDTH] = acc
        shift_new = (WINDOW - n_new - b * n_new) % LANES
        ko_ref[b] = jnp.where(keep_old, pltpu.roll(k_old, WINDOW - n_new, 1), pltpu.roll(k_new, shift_new, 1))
        vo_ref[b] = jnp.where(keep_old, pltpu.roll(v_old, WINDOW - n_new, 1), pltpu.roll(v_new, shift_new, 1))
    yb_ref[...] = yb_acc[...].astype(BF16)


def _sample_mix(sinks, u, va32, q, kvt, cache_k, cache_v, wexp, bsexp, bias_c, bias_n, prev, layer):
    n_seq = cache_k.shape[1]
    rows = SAMPLE_SEQS * SUBLANES
    row_spec = lambda width: pl.BlockSpec((rows, width), lambda i: (i, 0))
    cache_spec = pl.BlockSpec((None, SAMPLE_SEQS, KV_WIDTH, WINDOW), lambda i: (layer, i, 0, 0))
    table_spec = pl.BlockSpec((N_HEADS * SUBLANES, LANES), lambda i: (0, 0))
    in_specs = [pl.BlockSpec(memory_space=pltpu.SMEM),
                row_spec(A_WIDTH), row_spec(A_WIDTH), row_spec(Q_WIDTH),
                pl.BlockSpec((2 * KV_WIDTH, rows), lambda i: (0, i)),
                cache_spec, cache_spec,
                pl.BlockSpec((None, SUBLANES, SUBLANES, A_WIDTH), lambda i: (layer, 0, 0, 0)),
                pl.BlockSpec((None, SUBLANES, A_WIDTH), lambda i: (layer, 0, 0)),
                table_spec, table_spec,
                pl.BlockSpec(memory_space=pl.ANY), pl.BlockSpec(memory_space=pl.ANY)]
    operands = [sinks, u, va32, q, kvt, cache_k, cache_v, wexp, bsexp, bias_c, bias_n, *prev]
    return pl.pallas_call(
        _sample_mix_kernel,
        out_shape=[jax.ShapeDtypeStruct((n_seq * SUBLANES, A_WIDTH), BF16),
                   jax.ShapeDtypeStruct((n_seq * SUBLANES, Q_WIDTH), BF16),
                   jax.ShapeDtypeStruct(cache_k.shape, F32),
                   jax.ShapeDtypeStruct(cache_v.shape, F32)],
        grid=(n_seq // SAMPLE_SEQS,),
        in_specs=in_specs,
        out_specs=[row_spec(A_WIDTH), row_spec(Q_WIDTH), cache_spec, cache_spec],
        scratch_shapes=[pltpu.VMEM((rows, Q_WIDTH), F32)],
        input_output_aliases={len(operands) - 2: 2, len(operands) - 1: 3},
        compiler_params=_params(1),
        name="sample_mix",
    )(*operands)


def _merge_ffn_kernel(*refs, alpha, n_prompt_tiles, split_x, split_out):
    refs = list(refs)
    x_refs = [refs.pop(0) for _ in range(2 if split_x else 1)]
    (ya_ref, yb_ref, gate_ref, yas_ref, ybs_ref, gates_ref,
     wpa_ref, wpb_ref, wo_ref, wg_ref, wu_ref, wd_ref, ln_ref, *out_refs) = refs
    is_sample = pl.program_id(0) >= n_prompt_tiles

    def pick(prompt_ref, sample_ref):
        return jnp.where(is_sample, sample_ref[...], prompt_ref[...])

    x = pick(*x_refs) if split_x else x_refs[0][...]
    gates = pick(gate_ref, gates_ref)
    g_a = gates[:, 0:D_MODEL].astype(F32)
    g_b = gates[:, D_MODEL:2 * D_MODEL].astype(F32)
    merged = g_a * _dot(pick(ya_ref, yas_ref), wpa_ref[...]) + g_b * _dot(pick(yb_ref, ybs_ref), wpb_ref[...])
    mix = _dot(merged.astype(BF16), wo_ref[...])
    x1 = _layer_norm(alpha * x + mix, ln_ref[0:1, :], ln_ref[1:2, :])
    x1b = x1.astype(BF16)
    act = (jax.nn.silu(_dot(x1b, wg_ref[...])) * _dot(x1b, wu_ref[...])).astype(BF16)
    ffn = _dot(act, wd_ref[...])
    y = _layer_norm(alpha * x1 + ffn, ln_ref[2:3, :], ln_ref[3:4, :])
    if split_out:
        @pl.when(jnp.logical_not(is_sample))
        def _():
            out_refs[0][...] = y

        @pl.when(is_sample)
        def _():
            out_refs[1][...] = y
    else:
        out_refs[0][...] = y


def _merge_ffn(x, prompt, sample, w_pa, w_pb, w_o, w_gate, w_up, w_down, ln_pack, layer, alpha, split_out):
    split_x = isinstance(x, tuple)
    n_p, n_s = prompt[0].shape[0] // ROW_TILE, sample[0].shape[0] // ROW_TILE
    all_rows = lambda width: pl.BlockSpec((ROW_TILE, width), lambda i: (i, 0))
    p_rows = lambda width: pl.BlockSpec((ROW_TILE, width), lambda i: (jnp.minimum(i, n_p - 1), 0))
    s_rows = lambda width: pl.BlockSpec((ROW_TILE, width), lambda i: (jnp.maximum(i - n_p, 0), 0),
                                        pipeline_mode=pl.Buffered(1))
    weight = lambda k, n: _resident((None, k, n), lambda i: (layer, 0, 0))
    widths = (A_WIDTH, Q_WIDTH, 2 * D_MODEL)
    x_specs = [p_rows(D_MODEL), s_rows(D_MODEL)] if split_x else [all_rows(D_MODEL)]
    if split_out:
        out_shape = [jax.ShapeDtypeStruct((n_p * ROW_TILE, D_MODEL), F32),
                     jax.ShapeDtypeStruct((n_s * ROW_TILE, D_MODEL), F32)]
        out_specs = [p_rows(D_MODEL), pl.BlockSpec((ROW_TILE, D_MODEL), lambda i: (jnp.maximum(i - n_p, 0), 0))]
    else:
        out_shape = jax.ShapeDtypeStruct(((n_p + n_s) * ROW_TILE, D_MODEL), F32)
        out_specs = all_rows(D_MODEL)
    return pl.pallas_call(
        functools.partial(_merge_ffn_kernel, alpha=alpha, n_prompt_tiles=n_p, split_x=split_x,
                          split_out=split_out),
        out_shape=out_shape,
        grid=(n_p + n_s,),
        in_specs=x_specs + [p_rows(w) for w in widths] + [s_rows(w) for w in widths] + [
            weight(A_WIDTH, D_MODEL), weight(Q_WIDTH, D_MODEL), weight(D_MODEL, D_MODEL),
            weight(D_MODEL, D_FF), weight(D_MODEL, D_FF), weight(D_FF, D_MODEL),
            pl.BlockSpec((None, 4, D_MODEL), lambda i: (layer, 0, 0))],
        out_specs=out_specs,
        compiler_params=_params(1),
        name="merge_ffn",
    )(*(x if split_x else (x,)), *prompt, *sample, w_pa, w_pb, w_o, w_gate, w_up, w_down, ln_pack)


def kernel(x_prompt, x_sample, cache_swa_k, cache_swa_v, rel_bias, w_in, ln_v_g, ln_v_b, w_s, b_s,
           sinks, w_pa, w_pb, w_o, ln1_g, ln1_b, w_gate, w_up, w_down, ln2_g, ln2_b):
    depth = w_in.shape[0]
    batch, seq, _ = x_prompt.shape
    n_seq, n_new, _ = x_sample.shape
    assert n_new == SUBLANES and seq % ROW_TILE == 0 and n_seq % SAMPLE_SEQS == 0
    assert (n_seq * n_new) % ROW_TILE == 0
    alpha = (2 * depth) ** 0.25

    def heads_rg(w, axis):
        shape = w.shape
        w = w.reshape(shape[:axis] + (N_KV_HEADS, Q_PER_KV, HEAD_DIM) + shape[axis + 1:])
        return jnp.swapaxes(w, axis, axis + 1).reshape(shape)

    w_in_b = w_in.astype(BF16)
    w_q_b = heads_rg(w_in[..., O_Q:O_K], 2).astype(BF16)
    w_pa_b, w_o_b = w_pa.astype(BF16), w_o.astype(BF16)
    w_pb_b = heads_rg(w_pb, 1).astype(BF16)
    w_gate_b, w_up_b, w_down_b = w_gate.astype(BF16), w_up.astype(BF16), w_down.astype(BF16)
    ln_v_g3, ln_v_b3 = ln_v_g[:, None, :], ln_v_b[:, None, :]
    ln_pack = jnp.stack([ln1_g, ln1_b, ln2_g, ln2_b], axis=1)
    b_s_t = jnp.swapaxes(b_s, 1, 2)
    wexp = jnp.repeat(jnp.transpose(w_s[:, :, :n_new, :n_new], (0, 3, 2, 1)), A_WIDTH // A_GROUPS, axis=-1)
    bsexp = jnp.repeat(jnp.swapaxes(b_s[:, :, :n_new], 1, 2), A_WIDTH // A_GROUPS, axis=-1)

    bias_p = _bias_tables(rel_bias, jnp.stack([_masked_buckets(WINDOW, KEY_PAD, False),
                                               _masked_buckets(WINDOW, KEY_PAD, True)]))
    bias_s = _bias_tables(rel_bias, _masked_buckets(n_new, WINDOW + n_new, False)[None])[0]
    bias_s = bias_s.reshape(N_KV_HEADS, Q_PER_KV, n_new, KEY_PAD)
    bias_s = jnp.swapaxes(bias_s, 0, 1).reshape(N_HEADS * n_new, KEY_PAD)
    bias_c = bias_s[:, :WINDOW]
    bias_n = jnp.tile(bias_s[:, WINDOW:WINDOW + n_new], (1, SAMPLE_SEQS))

    cache_k = jnp.transpose(cache_swa_k, (0, 1, 3, 4, 2)).reshape(depth, n_seq, KV_WIDTH, WINDOW)
    cache_v = jnp.transpose(cache_swa_v, (0, 1, 3, 4, 2)).reshape(depth, n_seq, KV_WIDTH, WINDOW)

    x = (x_prompt.reshape(batch * seq, D_MODEL), x_sample.reshape(n_seq * n_new, D_MODEL))
    n_prompt_tiles, sample_rows = batch * seq // ROW_TILE, n_seq * n_new
    kp_l, vp_l, ga_l = [], [], []
    new_cache = [jnp.zeros(cache_k.shape, F32), jnp.zeros(cache_v.shape, F32)]
    for l in range(depth):
        x_p, x_s, s_tile = (x[0], x[1], 0) if l == 0 else (x, x, n_prompt_tiles)
        ya, yb, gates, kv_tail = _prompt_front(sinks[l], x_p, w_in_b, w_q_b, ln_v_g3, ln_v_b3, w_s, b_s_t,
                                               bias_p, l, batch, seq)
        kp_l.append(kv_tail[..., :KV_WIDTH].reshape(batch, WINDOW, N_KV_HEADS, HEAD_DIM))
        vp_l.append(kv_tail[..., KV_WIDTH:].reshape(batch, WINDOW, N_KV_HEADS, HEAD_DIM))

        u, va32, q, gates_s, kvt = _inproj_sample(x_s, s_tile, sample_rows, w_in_b, w_q_b, ln_v_g3, ln_v_b3, l)
        ya_s, yb_s, *new_cache = _sample_mix(sinks[l], u, va32, q, kvt, cache_k, cache_v, wexp, bsexp,
                                             bias_c, bias_n, new_cache, l)
        ga_l.append(va32.reshape(n_seq, n_new, A_WIDTH))

        x = _merge_ffn(x, (ya, yb, gates), (ya_s, yb_s, gates_s), w_pa_b, w_pb_b, w_o_b, w_gate_b, w_up_b,
                       w_down_b, ln_pack, l, alpha, split_out=l == depth - 1)
    xp, xs = x

    def window_major(c):
        c = c.reshape(depth, n_seq, N_KV_HEADS, HEAD_DIM, WINDOW)
        return jnp.transpose(c, (0, 1, 4, 2, 3))

    return (xp.reshape(batch, seq, D_MODEL), xs.reshape(n_seq, n_new, D_MODEL),
            jnp.stack(kp_l), jnp.stack(vp_l), window_major(new_cache[0]), window_major(new_cache[1]),
            jnp.stack(ga_l))
```

```python
import functools
import math

import jax
import jax.numpy as jnp
from jax import lax
from jax.experimental import pallas as pl
from jax.experimental.pallas import tpu as pltpu

D_MODEL = 1024
CHUNK = 128
A_WIDTH = D_MODEL
A_GROUPS = 8
N_HEADS = 16
HEAD_DIM = 64
N_KV_HEADS = 4
Q_PER_KV = N_HEADS // N_KV_HEADS
WINDOW = 128
N_BUCKETS = 32
MAX_DISTANCE = 128
D_FF = 2816
LN_EPS = 1e-5
NEG_INF = -1e30
LOG2_E = math.log2(math.e)

KV_WIDTH = N_KV_HEADS * HEAD_DIM
Q_WIDTH = N_HEADS * HEAD_DIM
O_U = 0
O_V = O_U + A_WIDTH
O_Q = O_V + A_WIDTH
O_K = O_Q + Q_WIDTH
O_G = O_K + 2 * KV_WIDTH
IN_WIDTH = O_G + 2 * D_MODEL

LANES = 128
SUBLANES = 8
BF16_ROWS = 16
ROW_TILE = 512
PIECE = 512
SAMPLE_SEQS = LANES // SUBLANES
KEY_PAD = 2 * WINDOW
VMEM_LIMIT = 56 * 1024 * 1024

BF16 = jnp.bfloat16
F32 = jnp.float32


def _layer_norm(x, g, b):
    mu = jnp.mean(x, axis=-1, keepdims=True)
    xc = x - mu
    var = jnp.mean(xc * xc, axis=-1, keepdims=True)
    return xc * lax.rsqrt(var + LN_EPS) * g + b


def _gelu(x):
    return jax.nn.gelu(x, approximate=True)


def _sigmoid(x):
    return 0.5 * jnp.tanh(0.5 * x) + 0.5


def _dot(a, b):
    return jnp.dot(a, b, preferred_element_type=F32)


def _dot_nt(a, b):
    return lax.dot_general(a, b, (((1,), (1,)), ((), ())), preferred_element_type=F32)


def _resident(block_shape, index_map):
    return pl.BlockSpec(block_shape, index_map, pipeline_mode=pl.Buffered(1))


def _params(n_axes):
    return pltpu.CompilerParams(dimension_semantics=("arbitrary",) * n_axes,
                                vmem_limit_bytes=VMEM_LIMIT)


def _bias_kernel(rb_ref, bucket_ref, out_ref):
    bk = bucket_ref[...]
    for h in range(N_HEADS):
        acc = jnp.full(bk.shape, NEG_INF, F32)
        for b in range(N_BUCKETS):
            acc = jnp.where(bk == b, rb_ref[b, h], acc)
        out_ref[h] = acc


def _bias_tables(rel_bias, buckets):
    n, t, kp = buckets.shape
    return pl.pallas_call(
        _bias_kernel,
        out_shape=jax.ShapeDtypeStruct((n, N_HEADS, t, kp), F32),
        grid=(n,),
        in_specs=[pl.BlockSpec(memory_space=pltpu.SMEM),
                  pl.BlockSpec((None, t, kp), lambda i: (i, 0, 0))],
        out_specs=pl.BlockSpec((None, N_HEADS, t, kp), lambda i: (i, 0, 0, 0)),
        compiler_params=_params(1),
        name="bias_tables",
    )(rel_bias, buckets)


def _rel_bucket(dist):
    n = jnp.maximum(dist, 0)
    max_exact = N_BUCKETS // 2
    nf = jnp.maximum(n, 1).astype(F32)
    large = max_exact + (jnp.log(nf / max_exact) / math.log(MAX_DISTANCE / max_exact)
                         * (N_BUCKETS - max_exact)).astype(jnp.int32)
    large = jnp.minimum(large, N_BUCKETS - 1)
    return jnp.where(n < max_exact, n, large)


def _masked_buckets(n_q, n_keys, first_block):
    qi = jnp.arange(n_q, dtype=jnp.int32)[:, None]
    kj = jnp.arange(KEY_PAD, dtype=jnp.int32)[None, :]
    dist = qi + WINDOW - kj
    ok = (dist >= 0) & (dist < WINDOW) & (kj < n_keys)
    if first_block:
        ok = ok & (kj >= WINDOW)
    return jnp.where(ok, _rel_bucket(dist), -1)


def _prompt_front_kernel(sinks_ref, x_ref, w_ref, wq_ref, g_ref, b_ref, ws_ref, bs_ref, bias_ref,
                         ya_ref, yb_ref, gate_ref, tail_ref, u_s, va_s, q_s, kv_s, hv_s, bias_s):
    n_chunks = ROW_TILE // CHUNK
    first_tile = pl.program_id(1) == 0

    lane = lax.broadcasted_iota(jnp.int32, (1, LANES), 1)

    @pl.when(jnp.logical_and(pl.program_id(0) == 0, first_tile))
    def _():
        kv_s[0:CHUNK, :] = jnp.zeros((CHUNK, 2 * KV_WIDTH), BF16)
        for table in range(2):
            for h in range(N_HEADS):
                bias_s[table, h, :, 0:LANES] = LOG2_E * jnp.where(lane == 0, sinks_ref[h],
                                                                  bias_ref[table, h, :, 0:LANES])
                bias_s[table, h, :, LANES:] = LOG2_E * bias_ref[table, h, :, LANES:]

    xb = x_ref[...].astype(BF16)
    tri = (lax.broadcasted_iota(jnp.int32, (CHUNK, CHUNK), 0)
           >= lax.broadcasted_iota(jnp.int32, (CHUNK, CHUNK), 1))
    low_half = lane < HEAD_DIM
    zero = jnp.zeros((), BF16)

    def piece_kv(j):
        cols = slice(j * PIECE, (j + 1) * PIECE)
        kv = _dot(xb, w_ref[:, O_K + j * PIECE:O_K + (j + 1) * PIECE])
        kv_s[CHUNK:CHUNK + ROW_TILE, cols] = kv.astype(BF16)
        tail_ref[:, cols] = kv[ROW_TILE - WINDOW:, :]

    def piece_q(j):
        cols = slice(j * PIECE, (j + 1) * PIECE)
        q_s[:, cols] = (_dot(xb, wq_ref[:, cols]) * (LOG2_E * HEAD_DIM ** -0.5)).astype(BF16)

    def piece_v(j):
        cols = slice(j * PIECE, (j + 1) * PIECE)
        hv_s[:, cols] = _gelu(_dot(xb, w_ref[:, O_V + j * PIECE:O_V + (j + 1) * PIECE]))

    def piece_v_norm():
        va_s[...] = _layer_norm(hv_s[...], g_ref[...], b_ref[...]).astype(BF16)

    def piece_u(j):
        cols = slice(j * PIECE, (j + 1) * PIECE)
        u_s[:, cols] = _gelu(_dot(xb, w_ref[:, O_U + j * PIECE:O_U + (j + 1) * PIECE])).astype(BF16)

    def piece_gate(j):
        cols = slice(j * PIECE, (j + 1) * PIECE)
        gate_ref[:, cols] = _sigmoid(
            _dot(xb, w_ref[:, O_G + j * PIECE:O_G + (j + 1) * PIECE])).astype(BF16)

    def unit_spatial(c, g):
        rows, cols = slice(c * CHUNK, (c + 1) * CHUNK), slice(g * LANES, (g + 1) * LANES)
        w = jnp.where(tri, ws_ref[g], 0.0).astype(BF16)
        mixed = _dot(w, va_s[rows, cols]) + bs_ref[:, g:g + 1]
        ya_ref[rows, cols] = (u_s[rows, cols].astype(F32) * mixed).astype(BF16)

    def band_operands(c, gp, hi):
        band = slice(c * CHUNK, (c + 2) * CHUNK)
        keep = low_half if hi == 0 else jnp.logical_not(low_half)
        kn = jnp.where(keep, kv_s[band, gp * LANES:(gp + 1) * LANES], zero)
        vn = jnp.where(keep, kv_s[band, KV_WIDTH + gp * LANES:KV_WIDTH + (gp + 1) * LANES], zero)
        not_sink = lax.broadcasted_iota(jnp.int32, (BF16_ROWS, 1), 0) > 0
        kn = jnp.concatenate([jnp.where(not_sink, kn[:BF16_ROWS], zero), kn[BF16_ROWS:]], axis=0)
        vn = jnp.concatenate([jnp.where(not_sink, vn[:BF16_ROWS], zero), vn[BF16_ROWS:]], axis=0)
        return kn, vn

    def unit_attention(c, gp):
        rows = slice(c * CHUNK, (c + 1) * CHUNK)
        table = jnp.where(first_tile, 1, 0) if c == 0 else 0
        pair_cols = [slice((r * 2 + gp) * LANES, (r * 2 + gp + 1) * LANES) for r in range(Q_PER_KV)]
        q4 = jnp.concatenate([q_s[rows, cols] for cols in pair_cols], axis=0)
        out = None
        for hi in range(2):
            kn, vn = band_operands(c, gp, hi)
            h0 = Q_PER_KV * (2 * gp + hi)
            bias = bias_s[table, h0:h0 + Q_PER_KV].reshape(Q_PER_KV * CHUNK, KEY_PAD)
            s = _dot_nt(q4, kn) + bias
            p = jnp.exp2(s - jnp.max(s, axis=-1, keepdims=True))
            o = _dot(p.astype(BF16), vn) * (1.0 / jnp.sum(p, axis=-1, keepdims=True))
            out = o if out is None else jnp.where(low_half, out, o)
        for r, cols in enumerate(pair_cols):
            yb_ref[rows, cols] = out[r * CHUNK:(r + 1) * CHUNK].astype(BF16)

    for j in range(2 * KV_WIDTH // PIECE):
        piece_kv(j)
    for j in range(Q_WIDTH // PIECE):
        piece_q(j)
    for j in range(A_WIDTH // PIECE):
        piece_v(j)
    piece_v_norm()
    for j in range(A_WIDTH // PIECE):
        piece_u(j)
    for j in range(2 * D_MODEL // PIECE):
        piece_gate(j)
    for c in range(n_chunks):
        for g in range(A_GROUPS):
            unit_spatial(c, g)
        for gp in range(N_KV_HEADS // 2):
            unit_attention(c, gp)

    kv_s[0:CHUNK, :] = kv_s[ROW_TILE:ROW_TILE + CHUNK, :]


def _prompt_front(sinks, x, w_in, w_q, ln_g, ln_b, w_s, b_s_t, bias, layer, batch, seq):
    n_tiles = seq // ROW_TILE
    tok_spec = lambda width: pl.BlockSpec((ROW_TILE, width), lambda b, i: (b * n_tiles + i, 0))
    vec_spec = pl.BlockSpec((None, 1, A_WIDTH), lambda b, i: (layer, 0, 0))
    rows = batch * seq
    return pl.pallas_call(
        _prompt_front_kernel,
        out_shape=[jax.ShapeDtypeStruct((rows, A_WIDTH), BF16),
                   jax.ShapeDtypeStruct((rows, Q_WIDTH), BF16),
                   jax.ShapeDtypeStruct((rows, 2 * D_MODEL), BF16),
                   jax.ShapeDtypeStruct((batch, WINDOW, 2 * KV_WIDTH), F32)],
        grid=(batch, n_tiles),
        in_specs=[pl.BlockSpec(memory_space=pltpu.SMEM),
                  tok_spec(D_MODEL),
                  _resident((None, D_MODEL, IN_WIDTH), lambda b, i: (layer, 0, 0)),
                  _resident((None, D_MODEL, Q_WIDTH), lambda b, i: (layer, 0, 0)),
                  vec_spec, vec_spec,
                  _resident((None, A_GROUPS, CHUNK, CHUNK), lambda b, i: (layer, 0, 0, 0)),
                  _resident((None, CHUNK, A_GROUPS), lambda b, i: (layer, 0, 0)),
                  _resident((2, N_HEADS, WINDOW, KEY_PAD), lambda b, i: (0, 0, 0, 0))],
        out_specs=[tok_spec(A_WIDTH), tok_spec(Q_WIDTH), tok_spec(2 * D_MODEL),
                   pl.BlockSpec((None, WINDOW, 2 * KV_WIDTH), lambda b, i: (b, 0, 0))],
        scratch_shapes=[pltpu.VMEM((ROW_TILE, A_WIDTH), BF16),
                        pltpu.VMEM((ROW_TILE, A_WIDTH), BF16),
                        pltpu.VMEM((ROW_TILE, Q_WIDTH), BF16),
                        pltpu.VMEM((CHUNK + ROW_TILE, 2 * KV_WIDTH), BF16),
                        pltpu.VMEM((ROW_TILE, A_WIDTH), F32),
                        pltpu.VMEM((2, N_HEADS, WINDOW, KEY_PAD), F32)],
        compiler_params=_params(2),
        name="prompt_front",
    )(sinks, x, w_in, w_q, ln_g, ln_b, w_s, b_s_t, bias)


def _inproj_sample_kernel(x_ref, w_ref, wq_ref, g_ref, b_ref, u_ref, va_ref, q_ref, gate_ref, kvt_ref):
    xb = x_ref[...].astype(BF16)

    def proj(c0, c1):
        return _dot(xb, w_ref[:, c0:c1])

    half = A_WIDTH // 2
    for c0 in range(O_U, O_V, half):
        u_ref[:, c0:c0 + half] = _gelu(proj(c0, c0 + half)).astype(BF16)
    va_ref[...] = _layer_norm(_gelu(proj(O_V, O_Q)), g_ref[...], b_ref[...])
    for c0 in range(0, Q_WIDTH, half):
        q_ref[:, c0:c0 + half] = (_dot(xb, wq_ref[:, c0:c0 + half]) * (HEAD_DIM ** -0.5)).astype(BF16)
    kvt_ref[...] = proj(O_K, O_G).T
    for c0 in range(0, 2 * D_MODEL, half):
        gate_ref[:, c0:c0 + half] = _sigmoid(proj(O_G + c0, O_G + c0 + half)).astype(BF16)


def _inproj_sample(x, first_tile, rows, w_in, w_q, ln_g, ln_b, layer):
    row_spec = lambda width: pl.BlockSpec((ROW_TILE, width), lambda i: (i, 0))
    vec_spec = pl.BlockSpec((None, 1, A_WIDTH), lambda i: (layer, 0, 0))
    return pl.pallas_call(
        _inproj_sample_kernel,
        out_shape=[jax.ShapeDtypeStruct((rows, A_WIDTH), BF16),
                   jax.ShapeDtypeStruct((rows, A_WIDTH), F32),
                   jax.ShapeDtypeStruct((rows, Q_WIDTH), BF16),
                   jax.ShapeDtypeStruct((rows, 2 * D_MODEL), BF16),
                   jax.ShapeDtypeStruct((2 * KV_WIDTH, rows), F32)],
        grid=(rows // ROW_TILE,),
        in_specs=[pl.BlockSpec((ROW_TILE, D_MODEL), lambda i: (first_tile + i, 0)),
                  _resident((None, D_MODEL, IN_WIDTH), lambda i: (layer, 0, 0)),
                  _resident((None, D_MODEL, Q_WIDTH), lambda i: (layer, 0, 0)),
                  vec_spec, vec_spec],
        out_specs=[row_spec(A_WIDTH), row_spec(A_WIDTH), row_spec(Q_WIDTH), row_spec(2 * D_MODEL),
                   pl.BlockSpec((2 * KV_WIDTH, ROW_TILE), lambda i: (0, i))],
        compiler_params=_params(1),
        name="inproj_sample",
    )(x, w_in, w_q, ln_g, ln_b)


def _sample_mix_kernel(sinks_ref, u_ref, va_ref, q_ref, kvt_ref, ck_ref, cv_ref, wexp_ref, bsexp_ref,
                       biasc_ref, biasn_ref, prev_k_ref, prev_v_ref, ya_ref, yb_ref, ko_ref, vo_ref, yb_acc):
    del prev_k_ref, prev_v_ref
    n_new = SUBLANES
    rows = SAMPLE_SEQS * n_new

    va3 = va_ref[...].reshape(SAMPLE_SEQS, n_new, A_WIDTH)
    t_idx = lax.broadcasted_iota(jnp.int32, (n_new, A_WIDTH), 0)
    mixed = jnp.broadcast_to(bsexp_ref[...][None], va3.shape)
    for s in range(n_new):
        w = jnp.where(t_idx >= s, wexp_ref[s], 0.0)
        mixed = mixed + w[None] * va3[:, s:s + 1, :]
    u3 = u_ref[...].astype(F32).reshape(SAMPLE_SEQS, n_new, A_WIDTH)
    ya_ref[...] = (u3 * mixed).reshape(rows, A_WIDTH).astype(BF16)

    group_of_lane = lax.broadcasted_iota(jnp.int32, (1, KV_WIDTH), 1) // HEAD_DIM
    row = lax.broadcasted_iota(jnp.int32, (N_HEADS * n_new, 1), 0)
    head_of_row = Q_PER_KV * ((row // n_new) % N_KV_HEADS) + row // (n_new * N_KV_HEADS)
    sink = jnp.zeros((N_HEADS * n_new, 1), F32)
    for h in range(N_HEADS):
        sink = jnp.where(head_of_row == h, sinks_ref[h], sink)
    lane = lax.broadcasted_iota(jnp.int32, (1, LANES), 1)
    seq_of_lane = lane // n_new
    keep_old = lane < WINDOW - n_new
    q32 = q_ref[...].astype(F32)
    k_new, v_new = kvt_ref[0:KV_WIDTH, :], kvt_ref[KV_WIDTH:2 * KV_WIDTH, :]
    k_new_b, v_new_b = k_new.astype(BF16), v_new.astype(BF16)
    bias_c, bias_n = biasc_ref[...], biasn_ref[...]
    for b in range(SAMPLE_SEQS):
        new = slice(b * n_new, (b + 1) * n_new)
        pieces = []
        for r in range(Q_PER_KV):
            blk = q32[new, r * KV_WIDTH:(r + 1) * KV_WIDTH]
            for g in range(N_KV_HEADS):
                pieces.append(jnp.where(group_of_lane == g, blk, 0.0))
        q_rows = jnp.concatenate(pieces, axis=0).astype(BF16)
        k_old, v_old = ck_ref[b], cv_ref[b]
        k_all = jnp.concatenate([k_old.astype(BF16), k_new_b], axis=1)
        v_all = jnp.concatenate([v_old.astype(BF16), v_new_b], axis=1)
        bias = jnp.concatenate([bias_c, jnp.where(seq_of_lane == b, bias_n, NEG_INF)], axis=1)
        s = _dot(q_rows, k_all) + bias
        m = jnp.maximum(jnp.max(s, axis=-1, keepdims=True), sink)
        p = jnp.exp(s - m)
        denom = jnp.sum(p, axis=-1, keepdims=True) + jnp.exp(sink - m)
        o = _dot_nt(p.astype(BF16), v_all) * (1.0 / denom)
        for r in range(Q_PER_KV):
            acc = jnp.zeros((n_new, KV_WIDTH), F32)
            for g in range(N_KV_HEADS):
                r0 = (r * N_KV_HEADS + g) * n_new
                acc = jnp.where(group_of_lane == g, o[r0:r0 + n_new, :], acc)
            yb_acc[new, r * KV_WIDTH:(r + 1) * KV_WIDTH] = acc
        shift_new = (WINDOW - n_new - b * n_new) % LANES
        ko_ref[b] = jnp.where(keep_old, pltpu.roll(k_old, WINDOW - n_new, 1), pltpu.roll(k_new, shift_new, 1))
        vo_ref[b] = jnp.where(keep_old, pltpu.roll(v_old, WINDOW - n_new, 1), pltpu.roll(v_new, shift_new, 1))
    yb_ref[...] = yb_acc[...].astype(BF16)


def _sample_mix(sinks, u, va32, q, kvt, cache_k, cache_v, wexp, bsexp, bias_c, bias_n, prev, layer):
    n_seq = cache_k.shape[1]
    rows = SAMPLE_SEQS * SUBLANES
    row_spec = lambda width: pl.BlockSpec((rows, width), lambda i: (i, 0))
    cache_spec = pl.BlockSpec((None, SAMPLE_SEQS, KV_WIDTH, WINDOW), lambda i: (layer, i, 0, 0))
    table_spec = pl.BlockSpec((N_HEADS * SUBLANES, LANES), lambda i: (0, 0))
    in_specs = [pl.BlockSpec(memory_space=pltpu.SMEM),
                row_spec(A_WIDTH), row_spec(A_WIDTH), row_spec(Q_WIDTH),
                pl.BlockSpec((2 * KV_WIDTH, rows), lambda i: (0, i)),
                cache_spec, cache_spec,
                pl.BlockSpec((None, SUBLANES, SUBLANES, A_WIDTH), lambda i: (layer, 0, 0, 0)),
                pl.BlockSpec((None, SUBLANES, A_WIDTH), lambda i: (layer, 0, 0)),
                table_spec, table_spec,
                pl.BlockSpec(memory_space=pl.ANY), pl.BlockSpec(memory_space=pl.ANY)]
    operands = [sinks, u, va32, q, kvt, cache_k, cache_v, wexp, bsexp, bias_c, bias_n, *prev]
    return pl.pallas_call(
        _sample_mix_kernel,
        out_shape=[jax.ShapeDtypeStruct((n_seq * SUBLANES, A_WIDTH), BF16),
                   jax.ShapeDtypeStruct((n_seq * SUBLANES, Q_WIDTH), BF16),
                   jax.ShapeDtypeStruct(cache_k.shape, F32),
                   jax.ShapeDtypeStruct(cache_v.shape, F32)],
        grid=(n_seq // SAMPLE_SEQS,),
        in_specs=in_specs,
        out_specs=[row_spec(A_WIDTH), row_spec(Q_WIDTH), cache_spec, cache_spec],
        scratch_shapes=[pltpu.VMEM((rows, Q_WIDTH), F32)],
        input_output_aliases={len(operands) - 2: 2, len(operands) - 1: 3},
        compiler_params=_params(1),
        name="sample_mix",
    )(*operands)


def _merge_ffn_kernel(*refs, alpha, n_prompt_tiles, split_x, split_out):
    refs = list(refs)
    x_refs = [refs.pop(0) for _ in range(2 if split_x else 1)]
    (ya_ref, yb_ref, gate_ref, yas_ref, ybs_ref, gates_ref,
     wpa_ref, wpb_ref, wo_ref, wg_ref, wu_ref, wd_ref, ln_ref, *out_refs) = refs
    is_sample = pl.program_id(0) >= n_prompt_tiles

    def pick(prompt_ref, sample_ref):
        return jnp.where(is_sample, sample_ref[...], prompt_ref[...])

    x = pick(*x_refs) if split_x else x_refs[0][...]
    gates = pick(gate_ref, gates_ref)
    g_a = gates[:, 0:D_MODEL].astype(F32)
    g_b = gates[:, D_MODEL:2 * D_MODEL].astype(F32)
    merged = g_a * _dot(pick(ya_ref, yas_ref), wpa_ref[...]) + g_b * _dot(pick(yb_ref, ybs_ref), wpb_ref[...])
    mix = _dot(merged.astype(BF16), wo_ref[...])
    x1 = _layer_norm(alpha * x + mix, ln_ref[0:1, :], ln_ref[1:2, :])
    x1b = x1.astype(BF16)
    gate = _dot(x1b, wg_ref[...])
    act = (gate * _sigmoid(gate) * _dot(x1b, wu_ref[...])).astype(BF16)
    ffn = _dot(act, wd_ref[...])
    y = _layer_norm(alpha * x1 + ffn, ln_ref[2:3, :], ln_ref[3:4, :])
    if split_out:
        @pl.when(jnp.logical_not(is_sample))
        def _():
            out_refs[0][...] = y

        @pl.when(is_sample)
        def _():
            out_refs[1][...] = y
    else:
        out_refs[0][...] = y


def _merge_ffn(x, prompt, sample, w_pa, w_pb, w_o, w_gate, w_up, w_down, ln_pack, layer, alpha, split_out):
    split_x = isinstance(x, tuple)
    n_p, n_s = prompt[0].shape[0] // ROW_TILE, sample[0].shape[0] // ROW_TILE
    all_rows = lambda width: pl.BlockSpec((ROW_TILE, width), lambda i: (i, 0))
    p_rows = lambda width: pl.BlockSpec((ROW_TILE, width), lambda i: (jnp.minimum(i, n_p - 1), 0))
    s_rows = lambda width: pl.BlockSpec((ROW_TILE, width), lambda i: (jnp.maximum(i - n_p, 0), 0),
                                        pipeline_mode=pl.Buffered(1))
    weight = lambda k, n: _resident((None, k, n), lambda i: (layer, 0, 0))
    widths = (A_WIDTH, Q_WIDTH, 2 * D_MODEL)
    x_specs = [p_rows(D_MODEL), s_rows(D_MODEL)] if split_x else [all_rows(D_MODEL)]
    if split_out:
        out_shape = [jax.ShapeDtypeStruct((n_p * ROW_TILE, D_MODEL), F32),
                     jax.ShapeDtypeStruct((n_s * ROW_TILE, D_MODEL), F32)]
        out_specs = [p_rows(D_MODEL), pl.BlockSpec((ROW_TILE, D_MODEL), lambda i: (jnp.maximum(i - n_p, 0), 0))]
    else:
        out_shape = jax.ShapeDtypeStruct(((n_p + n_s) * ROW_TILE, D_MODEL), F32)
        out_specs = all_rows(D_MODEL)
    return pl.pallas_call(
        functools.partial(_merge_ffn_kernel, alpha=alpha, n_prompt_tiles=n_p, split_x=split_x,
                          split_out=split_out),
        out_shape=out_shape,
        grid=(n_p + n_s,),
        in_specs=x_specs + [p_rows(w) for w in widths] + [s_rows(w) for w in widths] + [
            weight(A_WIDTH, D_MODEL), weight(Q_WIDTH, D_MODEL), weight(D_MODEL, D_MODEL),
            weight(D_MODEL, D_FF), weight(D_MODEL, D_FF), weight(D_FF, D_MODEL),
            pl.BlockSpec((None, 4, D_MODEL), lambda i: (layer, 0, 0))],
        out_specs=out_specs,
        compiler_params=_params(1),
        name="merge_ffn",
    )(*(x if split_x else (x,)), *prompt, *sample, w_pa, w_pb, w_o, w_gate, w_up, w_down, ln_pack)


def kernel(x_prompt, x_sample, cache_swa_k, cache_swa_v, rel_bias, w_in, ln_v_g, ln_v_b, w_s, b_s,
           sinks, w_pa, w_pb, w_o, ln1_g, ln1_b, w_gate, w_up, w_down, ln2_g, ln2_b):
    depth = w_in.shape[0]
    batch, seq, _ = x_prompt.shape
    n_seq, n_new, _ = x_sample.shape
    assert n_new == SUBLANES and seq % ROW_TILE == 0 and n_seq % SAMPLE_SEQS == 0
    assert (n_seq * n_new) % ROW_TILE == 0
    alpha = (2 * depth) ** 0.25

    def heads_rg(w, axis):
        shape = w.shape
        w = w.reshape(shape[:axis] + (N_KV_HEADS, Q_PER_KV, HEAD_DIM) + shape[axis + 1:])
        return jnp.swapaxes(w, axis, axis + 1).reshape(shape)

    w_in_b = w_in.astype(BF16)
    w_q_b = heads_rg(w_in[..., O_Q:O_K], 2).astype(BF16)
    w_pa_b, w_o_b = w_pa.astype(BF16), w_o.astype(BF16)
    w_pb_b = heads_rg(w_pb, 1).astype(BF16)
    w_gate_b, w_up_b, w_down_b = w_gate.astype(BF16), w_up.astype(BF16), w_down.astype(BF16)
    ln_v_g3, ln_v_b3 = ln_v_g[:, None, :], ln_v_b[:, None, :]
    ln_pack = jnp.stack([ln1_g, ln1_b, ln2_g, ln2_b], axis=1)
    b_s_t = jnp.swapaxes(b_s, 1, 2)
    wexp = jnp.repeat(jnp.transpose(w_s[:, :, :n_new, :n_new], (0, 3, 2, 1)), A_WIDTH // A_GROUPS, axis=-1)
    bsexp = jnp.repeat(jnp.swapaxes(b_s[:, :, :n_new], 1, 2), A_WIDTH // A_GROUPS, axis=-1)

    bias_p = _bias_tables(rel_bias, jnp.stack([_masked_buckets(WINDOW, KEY_PAD, False),
                                               _masked_buckets(WINDOW, KEY_PAD, True)]))
    bias_s = _bias_tables(rel_bias, _masked_buckets(n_new, WINDOW + n_new, False)[None])[0]
    bias_s = bias_s.reshape(N_KV_HEADS, Q_PER_KV, n_new, KEY_PAD)
    bias_s = jnp.swapaxes(bias_s, 0, 1).reshape(N_HEADS * n_new, KEY_PAD)
    bias_c = bias_s[:, :WINDOW]
    bias_n = jnp.tile(bias_s[:, WINDOW:WINDOW + n_new], (1, SAMPLE_SEQS))

    cache_k = jnp.transpose(cache_swa_k, (0, 1, 3, 4, 2)).reshape(depth, n_seq, KV_WIDTH, WINDOW)
    cache_v = jnp.transpose(cache_swa_v, (0, 1, 3, 4, 2)).reshape(depth, n_seq, KV_WIDTH, WINDOW)

    x = (x_prompt.reshape(batch * seq, D_MODEL), x_sample.reshape(n_seq * n_new, D_MODEL))
    n_prompt_tiles, sample_rows = batch * seq // ROW_TILE, n_seq * n_new
    kp_l, vp_l, ga_l = [], [], []
    new_cache = [jnp.zeros(cache_k.shape, F32), jnp.zeros(cache_v.shape, F32)]
    for l in range(depth):
        x_p, x_s, s_tile = (x[0], x[1], 0) if l == 0 else (x, x, n_prompt_tiles)
        ya, yb, gates, kv_tail = _prompt_front(sinks[l], x_p, w_in_b, w_q_b, ln_v_g3, ln_v_b3, w_s, b_s_t,
                                               bias_p, l, batch, seq)
        kp_l.append(kv_tail[..., :KV_WIDTH].reshape(batch, WINDOW, N_KV_HEADS, HEAD_DIM))
        vp_l.append(kv_tail[..., KV_WIDTH:].reshape(batch, WINDOW, N_KV_HEADS, HEAD_DIM))

        u, va32, q, gates_s, kvt = _inproj_sample(x_s, s_tile, sample_rows, w_in_b, w_q_b, ln_v_g3, ln_v_b3, l)
        ya_s, yb_s, *new_cache = _sample_mix(sinks[l], u, va32, q, kvt, cache_k, cache_v, wexp, bsexp,
                                             bias_c, bias_n, new_cache, l)
        ga_l.append(va32.reshape(n_seq, n_new, A_WIDTH))

        x = _merge_ffn(x, (ya, yb, gates), (ya_s, yb_s, gates_s), w_pa_b, w_pb_b, w_o_b, w_gate_b, w_up_b,
                       w_down_b, ln_pack, l, alpha, split_out=l == depth - 1)
    xp, xs = x

    def window_major(c):
        c = c.reshape(depth, n_seq, N_KV_HEADS, HEAD_DIM, WINDOW)
        return jnp.transpose(c, (0, 1, 4, 2, 3))

    return (xp.reshape(batch, seq, D_MODEL), xs.reshape(n_seq, n_new, D_MODEL),
            jnp.stack(kp_l), jnp.stack(vp_l), window_major(new_cache[0]), window_major(new_cache[1]),
            jnp.stack(ga_l))
```

```python
import functools
import math

import jax
import jax.numpy as jnp
from jax import lax
from jax.experimental import pallas as pl
from jax.experimental.pallas import tpu as pltpu

D_MODEL = 1024
CHUNK = 128
A_WIDTH = D_MODEL
A_GROUPS = 8
N_HEADS = 16
HEAD_DIM = 64
N_KV_HEADS = 4
Q_PER_KV = N_HEADS // N_KV_HEADS
WINDOW = 128
N_BUCKETS = 32
MAX_DISTANCE = 128
D_FF = 2816
LN_EPS = 1e-5
NEG_INF = -1e30
LOG2_E = math.log2(math.e)

KV_WIDTH = N_KV_HEADS * HEAD_DIM
Q_WIDTH = N_HEADS * HEAD_DIM
O_U = 0
O_V = O_U + A_WIDTH
O_Q = O_V + A_WIDTH
O_K = O_Q + Q_WIDTH
O_G = O_K + 2 * KV_WIDTH
IN_WIDTH = O_G + 2 * D_MODEL

LANES = 128
SUBLANES = 8
BF16_ROWS = 16
ROW_TILE = 512
PIECE = 512
SAMPLE_SEQS = LANES // SUBLANES
KEY_PAD = 2 * WINDOW
MIX_YA, MIX_YB, MIX_GATE = 0, A_WIDTH, A_WIDTH + Q_WIDTH
MIX_WIDTH = MIX_GATE + 2 * D_MODEL
VMEM_LIMIT = 56 * 1024 * 1024

BF16 = jnp.bfloat16
F32 = jnp.float32


def _layer_norm(x, g, b):
    mu = jnp.mean(x, axis=-1, keepdims=True)
    xc = x - mu
    var = jnp.mean(xc * xc, axis=-1, keepdims=True)
    return xc * lax.rsqrt(var + LN_EPS) * g + b


def _gelu(x):
    return jax.nn.gelu(x, approximate=True)


def _sigmoid(x):
    return 0.5 * jnp.tanh(0.5 * x) + 0.5


def _dot(a, b):
    return jnp.dot(a, b, preferred_element_type=F32)


def _dot_nt(a, b):
    return lax.dot_general(a, b, (((1,), (1,)), ((), ())), preferred_element_type=F32)


def _resident(block_shape, index_map):
    return pl.BlockSpec(block_shape, index_map, pipeline_mode=pl.Buffered(1))


def _params(n_axes):
    return pltpu.CompilerParams(dimension_semantics=("arbitrary",) * n_axes,
                                vmem_limit_bytes=VMEM_LIMIT)


def _bias_kernel(rb_ref, bucket_ref, out_ref):
    bk = bucket_ref[...]
    for h in range(N_HEADS):
        acc = jnp.full(bk.shape, NEG_INF, F32)
        for b in range(N_BUCKETS):
            acc = jnp.where(bk == b, rb_ref[b, h], acc)
        out_ref[h] = acc


def _bias_tables(rel_bias, buckets):
    n, t, kp = buckets.shape
    return pl.pallas_call(
        _bias_kernel,
        out_shape=jax.ShapeDtypeStruct((n, N_HEADS, t, kp), F32),
        grid=(n,),
        in_specs=[pl.BlockSpec(memory_space=pltpu.SMEM),
                  pl.BlockSpec((None, t, kp), lambda i: (i, 0, 0))],
        out_specs=pl.BlockSpec((None, N_HEADS, t, kp), lambda i: (i, 0, 0, 0)),
        compiler_params=_params(1),
        name="bias_tables",
    )(rel_bias, buckets)


def _rel_bucket(dist):
    n = jnp.maximum(dist, 0)
    max_exact = N_BUCKETS // 2
    nf = jnp.maximum(n, 1).astype(F32)
    large = max_exact + (jnp.log(nf / max_exact) / math.log(MAX_DISTANCE / max_exact)
                         * (N_BUCKETS - max_exact)).astype(jnp.int32)
    large = jnp.minimum(large, N_BUCKETS - 1)
    return jnp.where(n < max_exact, n, large)


def _masked_buckets(n_q, n_keys, first_block):
    qi = jnp.arange(n_q, dtype=jnp.int32)[:, None]
    kj = jnp.arange(KEY_PAD, dtype=jnp.int32)[None, :]
    dist = qi + WINDOW - kj
    ok = (dist >= 0) & (dist < WINDOW) & (kj < n_keys)
    if first_block:
        ok = ok & (kj >= WINDOW)
    return jnp.where(ok, _rel_bucket(dist), -1)


def _prompt_front_kernel(sinks_ref, x_ref, w_ref, wq_ref, g_ref, b_ref, ws_ref, bs_ref, bias_ref,
                         mix_ref, tail_ref, u_s, va_s, q_s, kv_s, hv_s, bias_s):
    n_chunks = ROW_TILE // CHUNK
    first_tile = pl.program_id(1) == 0

    lane = lax.broadcasted_iota(jnp.int32, (1, LANES), 1)

    @pl.when(jnp.logical_and(pl.program_id(0) == 0, first_tile))
    def _():
        kv_s[0:CHUNK, :] = jnp.zeros((CHUNK, 2 * KV_WIDTH), BF16)
        for table in range(2):
            for h in range(N_HEADS):
                bias_s[table, h, :, 0:LANES] = LOG2_E * jnp.where(lane == 0, sinks_ref[h],
                                                                  bias_ref[table, h, :, 0:LANES])
                bias_s[table, h, :, LANES:] = LOG2_E * bias_ref[table, h, :, LANES:]

    xb = x_ref[...].astype(BF16)
    tri = (lax.broadcasted_iota(jnp.int32, (CHUNK, CHUNK), 0)
           >= lax.broadcasted_iota(jnp.int32, (CHUNK, CHUNK), 1))
    low_half = lane < HEAD_DIM
    zero = jnp.zeros((), BF16)

    def piece_kv(j):
        cols = slice(j * PIECE, (j + 1) * PIECE)
        kv = _dot(xb, w_ref[:, O_K + j * PIECE:O_K + (j + 1) * PIECE])
        kv_s[CHUNK:CHUNK + ROW_TILE, cols] = kv.astype(BF16)
        tail_ref[:, cols] = kv[ROW_TILE - WINDOW:, :]

    def piece_q(j):
        cols = slice(j * PIECE, (j + 1) * PIECE)
        q_s[:, cols] = (_dot(xb, wq_ref[:, cols]) * (LOG2_E * HEAD_DIM ** -0.5)).astype(BF16)

    def piece_v(j):
        cols = slice(j * PIECE, (j + 1) * PIECE)
        hv_s[:, cols] = _gelu(_dot(xb, w_ref[:, O_V + j * PIECE:O_V + (j + 1) * PIECE]))

    def piece_v_norm():
        va_s[...] = _layer_norm(hv_s[...], g_ref[...], b_ref[...]).astype(BF16)

    def piece_u(j):
        cols = slice(j * PIECE, (j + 1) * PIECE)
        u_s[:, cols] = _gelu(_dot(xb, w_ref[:, O_U + j * PIECE:O_U + (j + 1) * PIECE])).astype(BF16)

    def piece_gate(j):
        cols = slice(MIX_GATE + j * PIECE, MIX_GATE + (j + 1) * PIECE)
        mix_ref[:, cols] = _sigmoid(
            _dot(xb, w_ref[:, O_G + j * PIECE:O_G + (j + 1) * PIECE])).astype(BF16)

    def unit_spatial(c, g):
        rows, cols = slice(c * CHUNK, (c + 1) * CHUNK), slice(g * LANES, (g + 1) * LANES)
        w = jnp.where(tri, ws_ref[g], 0.0).astype(BF16)
        mixed = _dot(w, va_s[rows, cols]) + bs_ref[:, g:g + 1]
        mix_ref[rows, MIX_YA + g * LANES:MIX_YA + (g + 1) * LANES] = (u_s[rows, cols].astype(F32) * mixed).astype(BF16)

    def band_operands(c, gp, hi):
        band = slice(c * CHUNK, (c + 2) * CHUNK)
        keep = low_half if hi == 0 else jnp.logical_not(low_half)
        kn = jnp.where(keep, kv_s[band, gp * LANES:(gp + 1) * LANES], zero)
        vn = jnp.where(keep, kv_s[band, KV_WIDTH + gp * LANES:KV_WIDTH + (gp + 1) * LANES], zero)
        not_sink = lax.broadcasted_iota(jnp.int32, (BF16_ROWS, 1), 0) > 0
        kn = jnp.concatenate([jnp.where(not_sink, kn[:BF16_ROWS], zero), kn[BF16_ROWS:]], axis=0)
        vn = jnp.concatenate([jnp.where(not_sink, vn[:BF16_ROWS], zero), vn[BF16_ROWS:]], axis=0)
        return kn, vn

    def unit_attention(c, gp):
        rows = slice(c * CHUNK, (c + 1) * CHUNK)
        table = jnp.where(first_tile, 1, 0) if c == 0 else 0
        pair_cols = [slice((r * 2 + gp) * LANES, (r * 2 + gp + 1) * LANES) for r in range(Q_PER_KV)]
        q4 = jnp.concatenate([q_s[rows, cols] for cols in pair_cols], axis=0)
        out = None
        for hi in range(2):
            kn, vn = band_operands(c, gp, hi)
            h0 = Q_PER_KV * (2 * gp + hi)
            bias = bias_s[table, h0:h0 + Q_PER_KV].reshape(Q_PER_KV * CHUNK, KEY_PAD)
            s = _dot_nt(q4, kn) + bias
            p = jnp.exp2(s - jnp.max(s, axis=-1, keepdims=True))
            o = _dot(p.astype(BF16), vn) * (1.0 / jnp.sum(p, axis=-1, keepdims=True))
            out = o if out is None else jnp.where(low_half, out, o)
        for r, cols in enumerate(pair_cols):
            mix_ref[rows, MIX_YB + cols.start:MIX_YB + cols.stop] = out[r * CHUNK:(r + 1) * CHUNK].astype(BF16)

    for j in range(2 * KV_WIDTH // PIECE):
        piece_kv(j)
    for j in range(Q_WIDTH // PIECE):
        piece_q(j)
    for j in range(A_WIDTH // PIECE):
        piece_v(j)
    piece_v_norm()
    for j in range(A_WIDTH // PIECE):
        piece_u(j)
    for j in range(2 * D_MODEL // PIECE):
        piece_gate(j)
    for c in range(n_chunks):
        for g in range(A_GROUPS):
            unit_spatial(c, g)
        for gp in range(N_KV_HEADS // 2):
            unit_attention(c, gp)

    kv_s[0:CHUNK, :] = kv_s[ROW_TILE:ROW_TILE + CHUNK, :]


def _prompt_front(sinks, x, w_in, w_q, ln_g, ln_b, w_s, b_s_t, bias, layer, batch, seq):
    n_tiles = seq // ROW_TILE
    tok_spec = lambda width: pl.BlockSpec((ROW_TILE, width), lambda b, i: (b * n_tiles + i, 0))
    vec_spec = pl.BlockSpec((None, 1, A_WIDTH), lambda b, i: (layer, 0, 0))
    rows = batch * seq
    return pl.pallas_call(
        _prompt_front_kernel,
        out_shape=[jax.ShapeDtypeStruct((rows, MIX_WIDTH), BF16),
                   jax.ShapeDtypeStruct((batch, WINDOW, 2 * KV_WIDTH), F32)],
        grid=(batch, n_tiles),
        in_specs=[pl.BlockSpec(memory_space=pltpu.SMEM),
                  tok_spec(D_MODEL),
                  _resident((None, D_MODEL, IN_WIDTH), lambda b, i: (layer, 0, 0)),
                  _resident((None, D_MODEL, Q_WIDTH), lambda b, i: (layer, 0, 0)),
                  vec_spec, vec_spec,
                  _resident((None, A_GROUPS, CHUNK, CHUNK), lambda b, i: (layer, 0, 0, 0)),
                  _resident((None, CHUNK, A_GROUPS), lambda b, i: (layer, 0, 0)),
                  _resident((2, N_HEADS, WINDOW, KEY_PAD), lambda b, i: (0, 0, 0, 0))],
        out_specs=[tok_spec(MIX_WIDTH),
                   pl.BlockSpec((None, WINDOW, 2 * KV_WIDTH), lambda b, i: (b, 0, 0))],
        scratch_shapes=[pltpu.VMEM((ROW_TILE, A_WIDTH), BF16),
                        pltpu.VMEM((ROW_TILE, A_WIDTH), BF16),
                        pltpu.VMEM((ROW_TILE, Q_WIDTH), BF16),
                        pltpu.VMEM((CHUNK + ROW_TILE, 2 * KV_WIDTH), BF16),
                        pltpu.VMEM((ROW_TILE, A_WIDTH), F32),
                        pltpu.VMEM((2, N_HEADS, WINDOW, KEY_PAD), F32)],
        compiler_params=_params(2),
        name="prompt_front",
    )(sinks, x, w_in, w_q, ln_g, ln_b, w_s, b_s_t, bias)


def _inproj_sample_kernel(x_ref, w_ref, wq_ref, g_ref, b_ref, u_ref, va_ref, q_ref, gate_ref, kvt_ref):
    xb = x_ref[...].astype(BF16)

    def proj(c0, c1):
        return _dot(xb, w_ref[:, c0:c1])

    half = A_WIDTH // 2
    for c0 in range(O_U, O_V, half):
        u_ref[:, c0:c0 + half] = _gelu(proj(c0, c0 + half)).astype(BF16)
    va_ref[...] = _layer_norm(_gelu(proj(O_V, O_Q)), g_ref[...], b_ref[...])
    for c0 in range(0, Q_WIDTH, half):
        q_ref[:, c0:c0 + half] = (_dot(xb, wq_ref[:, c0:c0 + half]) * (HEAD_DIM ** -0.5)).astype(BF16)
    kvt_ref[...] = proj(O_K, O_G).T
    for c0 in range(0, 2 * D_MODEL, half):
        gate_ref[:, c0:c0 + half] = _sigmoid(proj(O_G + c0, O_G + c0 + half)).astype(BF16)


def _inproj_sample(x, first_tile, rows, w_in, w_q, ln_g, ln_b, layer):
    row_spec = lambda width: pl.BlockSpec((ROW_TILE, width), lambda i: (i, 0))
    vec_spec = pl.BlockSpec((None, 1, A_WIDTH), lambda i: (layer, 0, 0))
    return pl.pallas_call(
        _inproj_sample_kernel,
        out_shape=[jax.ShapeDtypeStruct((rows, A_WIDTH), BF16),
                   jax.ShapeDtypeStruct((rows, A_WIDTH), F32),
                   jax.ShapeDtypeStruct((rows, Q_WIDTH), BF16),
                   jax.ShapeDtypeStruct((rows, 2 * D_MODEL), BF16),
                   jax.ShapeDtypeStruct((2 * KV_WIDTH, rows), F32)],
        grid=(rows // ROW_TILE,),
        in_specs=[pl.BlockSpec((ROW_TILE, D_MODEL), lambda i: (first_tile + i, 0)),
                  _resident((None, D_MODEL, IN_WIDTH), lambda i: (layer, 0, 0)),
                  _resident((None, D_MODEL, Q_WIDTH), lambda i: (layer, 0, 0)),
                  vec_spec, vec_spec],
        out_specs=[row_spec(A_WIDTH), row_spec(A_WIDTH), row_spec(Q_WIDTH), row_spec(2 * D_MODEL),
                   pl.BlockSpec((2 * KV_WIDTH, ROW_TILE), lambda i: (0, i))],
        compiler_params=_params(1),
        name="inproj_sample",
    )(x, w_in, w_q, ln_g, ln_b)


def _sample_mix_kernel(sinks_ref, u_ref, va_ref, q_ref, kvt_ref, ck_ref, cv_ref, wexp_ref, bsexp_ref,
                       biasc_ref, biasn_ref, prev_k_ref, prev_v_ref, ya_ref, yb_ref, ko_ref, vo_ref, yb_acc):
    del prev_k_ref, prev_v_ref
    n_new = SUBLANES
    rows = SAMPLE_SEQS * n_new

    va3 = va_ref[...].reshape(SAMPLE_SEQS, n_new, A_WIDTH)
    t_idx = lax.broadcasted_iota(jnp.int32, (n_new, A_WIDTH), 0)
    mixed = jnp.broadcast_to(bsexp_ref[...][None], va3.shape)
    for s in range(n_new):
        w = jnp.where(t_idx >= s, wexp_ref[s], 0.0)
        mixed = mixed + w[None] * va3[:, s:s + 1, :]
    u3 = u_ref[...].astype(F32).reshape(SAMPLE_SEQS, n_new, A_WIDTH)
    ya_ref[...] = (u3 * mixed).reshape(rows, A_WIDTH).astype(BF16)

    group_of_lane = lax.broadcasted_iota(jnp.int32, (1, KV_WIDTH), 1) // HEAD_DIM
    row = lax.broadcasted_iota(jnp.int32, (N_HEADS * n_new, 1), 0)
    head_of_row = Q_PER_KV * ((row // n_new) % N_KV_HEADS) + row // (n_new * N_KV_HEADS)
    sink = jnp.zeros((N_HEADS * n_new, 1), F32)
    for h in range(N_HEADS):
        sink = jnp.where(head_of_row == h, sinks_ref[h], sink)
    lane = lax.broadcasted_iota(jnp.int32, (1, LANES), 1)
    seq_of_lane = lane // n_new
    keep_old = lane < WINDOW - n_new
    q32 = q_ref[...].astype(F32)
    k_new, v_new = kvt_ref[0:KV_WIDTH, :], kvt_ref[KV_WIDTH:2 * KV_WIDTH, :]
    k_new_b, v_new_b = k_new.astype(BF16), v_new.astype(BF16)
    bias_c, bias_n = biasc_ref[...], biasn_ref[...]
    for b in range(SAMPLE_SEQS):
        new = slice(b * n_new, (b + 1) * n_new)
        pieces = []
        for r in range(Q_PER_KV):
            blk = q32[new, r * KV_WIDTH:(r + 1) * KV_WIDTH]
            for g in range(N_KV_HEADS):
                pieces.append(jnp.where(group_of_lane == g, blk, 0.0))
        q_rows = jnp.concatenate(pieces, axis=0).astype(BF16)
        k_old, v_old = ck_ref[b], cv_ref[b]
        k_all = jnp.concatenate([k_old.astype(BF16), k_new_b], axis=1)
        v_all = jnp.concatenate([v_old.astype(BF16), v_new_b], axis=1)
        bias = jnp.concatenate([bias_c, jnp.where(seq_of_lane == b, bias_n, NEG_INF)], axis=1)
        s = _dot(q_rows, k_all) + bias
        m = jnp.maximum(jnp.max(s, axis=-1, keepdims=True), sink)
        p = jnp.exp(s - m)
        denom = jnp.sum(p, axis=-1, keepdims=True) + jnp.exp(sink - m)
        o = _dot_nt(p.astype(BF16), v_all) * (1.0 / denom)
        for r in range(Q_PER_KV):
            acc = jnp.zeros((n_new, KV_WIDTH), F32)
            for g in range(N_KV_HEADS):
                r0 = (r * N_KV_HEADS + g) * n_new
                acc = jnp.where(group_of_lane == g, o[r0:r0 + n_new, :], acc)
            yb_acc[new, r * KV_WIDTH:(r + 1) * KV_WIDTH] = acc
        shift_new = (WINDOW - n_new - b * n_new) % LANES
        ko_ref[b] = jnp.where(keep_old, pltpu.roll(k_old, WINDOW - n_new, 1), pltpu.roll(k_new, shift_new, 1))
        vo_ref[b] = jnp.where(keep_old, pltpu.roll(v_old, WINDOW - n_new, 1), pltpu.roll(v_new, shift_new, 1))
    yb_ref[...] = yb_acc[...].astype(BF16)


def _sample_mix(sinks, u, va32, q, kvt, cache_k, cache_v, wexp, bsexp, bias_c, bias_n, prev, layer):
    n_seq = cache_k.shape[1]
    rows = SAMPLE_SEQS * SUBLANES
    row_spec = lambda width: pl.BlockSpec((rows, width), lambda i: (i, 0))
    cache_spec = pl.BlockSpec((None, SAMPLE_SEQS, KV_WIDTH, WINDOW), lambda i: (layer, i, 0, 0))
    table_spec = pl.BlockSpec((N_HEADS * SUBLANES, LANES), lambda i: (0, 0))
    in_specs = [pl.BlockSpec(memory_space=pltpu.SMEM),
                row_spec(A_WIDTH), row_spec(A_WIDTH), row_spec(Q_WIDTH),
                pl.BlockSpec((2 * KV_WIDTH, rows), lambda i: (0, i)),
                cache_spec, cache_spec,
                pl.BlockSpec((None, SUBLANES, SUBLANES, A_WIDTH), lambda i: (layer, 0, 0, 0)),
                pl.BlockSpec((None, SUBLANES, A_WIDTH), lambda i: (layer, 0, 0)),
                table_spec, table_spec,
                pl.BlockSpec(memory_space=pl.ANY), pl.BlockSpec(memory_space=pl.ANY)]
    operands = [sinks, u, va32, q, kvt, cache_k, cache_v, wexp, bsexp, bias_c, bias_n, *prev]
    return pl.pallas_call(
        _sample_mix_kernel,
        out_shape=[jax.ShapeDtypeStruct((n_seq * SUBLANES, A_WIDTH), BF16),
                   jax.ShapeDtypeStruct((n_seq * SUBLANES, Q_WIDTH), BF16),
                   jax.ShapeDtypeStruct(cache_k.shape, F32),
                   jax.ShapeDtypeStruct(cache_v.shape, F32)],
        grid=(n_seq // SAMPLE_SEQS,),
        in_specs=in_specs,
        out_specs=[row_spec(A_WIDTH), row_spec(Q_WIDTH), cache_spec, cache_spec],
        scratch_shapes=[pltpu.VMEM((rows, Q_WIDTH), F32)],
        input_output_aliases={len(operands) - 2: 2, len(operands) - 1: 3},
        compiler_params=_params(1),
        name="sample_mix",
    )(*operands)


def _merge_ffn_kernel(*refs, alpha, n_prompt_tiles, split_x, split_out):
    refs = list(refs)
    x_refs = [refs.pop(0) for _ in range(2 if split_x else 1)]
    (mix_ref, yas_ref, ybs_ref, gates_ref,
     wpa_ref, wpb_ref, wo_ref, wg_ref, wu_ref, wd_ref, ln_ref, *out_refs) = refs
    is_sample = pl.program_id(0) >= n_prompt_tiles

    def pick(prompt, sample_ref):
        return jnp.where(is_sample, sample_ref[...], prompt)

    x = pick(x_refs[0][...], x_refs[1]) if split_x else x_refs[0][...]
    gates = pick(mix_ref[:, MIX_GATE:MIX_WIDTH], gates_ref)
    g_a = gates[:, 0:D_MODEL].astype(F32)
    g_b = gates[:, D_MODEL:2 * D_MODEL].astype(F32)
    y_a = pick(mix_ref[:, MIX_YA:MIX_YA + A_WIDTH], yas_ref)
    y_b = pick(mix_ref[:, MIX_YB:MIX_YB + Q_WIDTH], ybs_ref)
    merged = g_a * _dot(y_a, wpa_ref[...]) + g_b * _dot(y_b, wpb_ref[...])
    mix = _dot(merged.astype(BF16), wo_ref[...])
    x1 = _layer_norm(alpha * x + mix, ln_ref[0:1, :], ln_ref[1:2, :])
    x1b = x1.astype(BF16)
    gate = _dot(x1b, wg_ref[...])
    act = (gate * _sigmoid(gate) * _dot(x1b, wu_ref[...])).astype(BF16)
    ffn = _dot(act, wd_ref[...])
    y = _layer_norm(alpha * x1 + ffn, ln_ref[2:3, :], ln_ref[3:4, :])
    if split_out:
        @pl.when(jnp.logical_not(is_sample))
        def _():
            out_refs[0][...] = y

        @pl.when(is_sample)
        def _():
            out_refs[1][...] = y
    else:
        out_refs[0][...] = y


def _merge_ffn(x, mix, sample, w_pa, w_pb, w_o, w_gate, w_up, w_down, ln_pack, layer, alpha, split_out):
    split_x = isinstance(x, tuple)
    n_p, n_s = mix.shape[0] // ROW_TILE, sample[0].shape[0] // ROW_TILE
    all_rows = lambda width: pl.BlockSpec((ROW_TILE, width), lambda i: (i, 0))
    p_rows = lambda width: pl.BlockSpec((ROW_TILE, width), lambda i: (jnp.minimum(i, n_p - 1), 0))
    s_rows = lambda width: pl.BlockSpec((ROW_TILE, width), lambda i: (jnp.maximum(i - n_p, 0), 0),
                                        pipeline_mode=pl.Buffered(1))
    weight = lambda k, n: _resident((None, k, n), lambda i: (layer, 0, 0))
    widths = (A_WIDTH, Q_WIDTH, 2 * D_MODEL)
    x_specs = [p_rows(D_MODEL), s_rows(D_MODEL)] if split_x else [all_rows(D_MODEL)]
    if split_out:
        out_shape = [jax.ShapeDtypeStruct((n_p * ROW_TILE, D_MODEL), F32),
                     jax.ShapeDtypeStruct((n_s * ROW_TILE, D_MODEL), F32)]
        out_specs = [p_rows(D_MODEL), pl.BlockSpec((ROW_TILE, D_MODEL), lambda i: (jnp.maximum(i - n_p, 0), 0))]
    else:
        out_shape = jax.ShapeDtypeStruct(((n_p + n_s) * ROW_TILE, D_MODEL), F32)
        out_specs = all_rows(D_MODEL)
    return pl.pallas_call(
        functools.partial(_merge_ffn_kernel, alpha=alpha, n_prompt_tiles=n_p, split_x=split_x,
                          split_out=split_out),
        out_shape=out_shape,
        grid=(n_p + n_s,),
        in_specs=x_specs + [p_rows(MIX_WIDTH)] + [s_rows(w) for w in widths] + [
            weight(A_WIDTH, D_MODEL), weight(Q_WIDTH, D_MODEL), weight(D_MODEL, D_MODEL),
            weight(D_MODEL, D_FF), weight(D_MODEL, D_FF), weight(D_FF, D_MODEL),
            pl.BlockSpec((None, 4, D_MODEL), lambda i: (layer, 0, 0))],
        out_specs=out_specs,
        compiler_params=_params(1),
        name="merge_ffn",
    )(*(x if split_x else (x,)), mix, *sample, w_pa, w_pb, w_o, w_gate, w_up, w_down, ln_pack)


def kernel(x_prompt, x_sample, cache_swa_k, cache_swa_v, rel_bias, w_in, ln_v_g, ln_v_b, w_s, b_s,
           sinks, w_pa, w_pb, w_o, ln1_g, ln1_b, w_gate, w_up, w_down, ln2_g, ln2_b):
    depth = w_in.shape[0]
    batch, seq, _ = x_prompt.shape
    n_seq, n_new, _ = x_sample.shape
    assert n_new == SUBLANES and seq % ROW_TILE == 0 and n_seq % SAMPLE_SEQS == 0
    assert (n_seq * n_new) % ROW_TILE == 0
    alpha = (2 * depth) ** 0.25

    def heads_rg(w, axis):
        shape = w.shape
        w = w.reshape(shape[:axis] + (N_KV_HEADS, Q_PER_KV, HEAD_DIM) + shape[axis + 1:])
        return jnp.swapaxes(w, axis, axis + 1).reshape(shape)

    w_in_b = w_in.astype(BF16)
    w_q_b = heads_rg(w_in[..., O_Q:O_K], 2).astype(BF16)
    w_pa_b, w_o_b = w_pa.astype(BF16), w_o.astype(BF16)
    w_pb_b = heads_rg(w_pb, 1).astype(BF16)
    w_gate_b, w_up_b, w_down_b = w_gate.astype(BF16), w_up.astype(BF16), w_down.astype(BF16)
    ln_v_g3, ln_v_b3 = ln_v_g[:, None, :], ln_v_b[:, None, :]
    ln_pack = jnp.stack([ln1_g, ln1_b, ln2_g, ln2_b], axis=1)
    b_s_t = jnp.swapaxes(b_s, 1, 2)
    wexp = jnp.repeat(jnp.transpose(w_s[:, :, :n_new, :n_new], (0, 3, 2, 1)), A_WIDTH // A_GROUPS, axis=-1)
    bsexp = jnp.repeat(jnp.swapaxes(b_s[:, :, :n_new], 1, 2), A_WIDTH // A_GROUPS, axis=-1)

    bias_p = _bias_tables(rel_bias, jnp.stack([_masked_buckets(WINDOW, KEY_PAD, False),
                                               _masked_buckets(WINDOW, KEY_PAD, True)]))
    bias_s = _bias_tables(rel_bias, _masked_buckets(n_new, WINDOW + n_new, False)[None])[0]
    bias_s = bias_s.reshape(N_KV_HEADS, Q_PER_KV, n_new, KEY_PAD)
    bias_s = jnp.swapaxes(bias_s, 0, 1).reshape(N_HEADS * n_new, KEY_PAD)
    bias_c = bias_s[:, :WINDOW]
    bias_n = jnp.tile(bias_s[:, WINDOW:WINDOW + n_new], (1, SAMPLE_SEQS))

    cache_k = jnp.transpose(cache_swa_k, (0, 1, 3, 4, 2)).reshape(depth, n_seq, KV_WIDTH, WINDOW)
    cache_v = jnp.transpose(cache_swa_v, (0, 1, 3, 4, 2)).reshape(depth, n_seq, KV_WIDTH, WINDOW)

    x = (x_prompt.reshape(batch * seq, D_MODEL), x_sample.reshape(n_seq * n_new, D_MODEL))
    n_prompt_tiles, sample_rows = batch * seq // ROW_TILE, n_seq * n_new
    kp_l, vp_l, ga_l = [], [], []
    new_cache = [jnp.zeros(cache_k.shape, F32), jnp.zeros(cache_v.shape, F32)]
    for l in range(depth):
        x_p, x_s, s_tile = (x[0], x[1], 0) if l == 0 else (x, x, n_prompt_tiles)
        mix, kv_tail = _prompt_front(sinks[l], x_p, w_in_b, w_q_b, ln_v_g3, ln_v_b3, w_s, b_s_t, bias_p, l,
                                     batch, seq)
        kp_l.append(kv_tail[..., :KV_WIDTH].reshape(batch, WINDOW, N_KV_HEADS, HEAD_DIM))
        vp_l.append(kv_tail[..., KV_WIDTH:].reshape(batch, WINDOW, N_KV_HEADS, HEAD_DIM))

        u, va32, q, gates_s, kvt = _inproj_sample(x_s, s_tile, sample_rows, w_in_b, w_q_b, ln_v_g3, ln_v_b3, l)
        ya_s, yb_s, *new_cache = _sample_mix(sinks[l], u, va32, q, kvt, cache_k, cache_v, wexp, bsexp,
                                             bias_c, bias_n, new_cache, l)
        ga_l.append(va32.reshape(n_seq, n_new, A_WIDTH))

        x = _merge_ffn(x, mix, (ya_s, yb_s, gates_s), w_pa_b, w_pb_b, w_o_b, w_gate_b, w_up_b,
                       w_down_b, ln_pack, l, alpha, split_out=l == depth - 1)
    xp, xs = x

    def window_major(c):
        c = c.reshape(depth, n_seq, N_KV_HEADS, HEAD_DIM, WINDOW)
        return jnp.transpose(c, (0, 1, 4, 2, 3))

    return (xp.reshape(batch, seq, D_MODEL), xs.reshape(n_seq, n_new, D_MODEL),
            jnp.stack(kp_l), jnp.stack(vp_l), window_major(new_cache[0]), window_major(new_cache[1]),
            jnp.stack(ga_l))
```

```python
import functools
import math

import jax
import jax.numpy as jnp
from jax import lax
from jax.experimental import pallas as pl
from jax.experimental.pallas import tpu as pltpu

D_MODEL = 1024
CHUNK = 128
A_WIDTH = D_MODEL
A_GROUPS = 8
N_HEADS = 16
HEAD_DIM = 64
N_KV_HEADS = 4
Q_PER_KV = N_HEADS // N_KV_HEADS
WINDOW = 128
N_BUCKETS = 32
MAX_DISTANCE = 128
D_FF = 2816
LN_EPS = 1e-5
NEG_INF = -1e30
LOG2_E = math.log2(math.e)

KV_WIDTH = N_KV_HEADS * HEAD_DIM
Q_WIDTH = N_HEADS * HEAD_DIM
O_U = 0
O_V = O_U + A_WIDTH
O_Q = O_V + A_WIDTH
O_K = O_Q + Q_WIDTH
O_G = O_K + 2 * KV_WIDTH
IN_WIDTH = O_G + 2 * D_MODEL

LANES = 128
SUBLANES = 8
BF16_ROWS = 16
ROW_TILE = 512
PIECE = 512
SAMPLE_SEQS = LANES // SUBLANES
KEY_PAD = 2 * WINDOW
MIX_YA, MIX_YB, MIX_GATE = 0, A_WIDTH, A_WIDTH + Q_WIDTH
MIX_WIDTH = MIX_GATE + 2 * D_MODEL
VMEM_LIMIT = 56 * 1024 * 1024

BF16 = jnp.bfloat16
F32 = jnp.float32


def _layer_norm(x, g, b):
    mu = jnp.mean(x, axis=-1, keepdims=True)
    xc = x - mu
    var = jnp.mean(xc * xc, axis=-1, keepdims=True)
    return xc * lax.rsqrt(var + LN_EPS) * g + b


def _gelu(x):
    return jax.nn.gelu(x, approximate=True)


def _sigmoid(x):
    return 0.5 * jnp.tanh(0.5 * x) + 0.5


def _dot(a, b):
    return jnp.dot(a, b, preferred_element_type=F32)


def _dot_nt(a, b):
    return lax.dot_general(a, b, (((1,), (1,)), ((), ())), preferred_element_type=F32)


def _resident(block_shape, index_map):
    return pl.BlockSpec(block_shape, index_map, pipeline_mode=pl.Buffered(1))


def _params(n_axes):
    return pltpu.CompilerParams(dimension_semantics=("arbitrary",) * n_axes,
                                vmem_limit_bytes=VMEM_LIMIT)


def _bias_kernel(rb_ref, bucket_ref, out_ref):
    bk = bucket_ref[...]
    for h in range(N_HEADS):
        acc = jnp.full(bk.shape, NEG_INF, F32)
        for b in range(N_BUCKETS):
            acc = jnp.where(bk == b, rb_ref[b, h], acc)
        out_ref[h] = acc


def _bias_tables(rel_bias, buckets):
    n, t, kp = buckets.shape
    return pl.pallas_call(
        _bias_kernel,
        out_shape=jax.ShapeDtypeStruct((n, N_HEADS, t, kp), F32),
        grid=(n,),
        in_specs=[pl.BlockSpec(memory_space=pltpu.SMEM),
                  pl.BlockSpec((None, t, kp), lambda i: (i, 0, 0))],
        out_specs=pl.BlockSpec((None, N_HEADS, t, kp), lambda i: (i, 0, 0, 0)),
        compiler_params=_params(1),
        name="bias_tables",
    )(rel_bias, buckets)


def _rel_bucket(dist):
    n = jnp.maximum(dist, 0)
    max_exact = N_BUCKETS // 2
    nf = jnp.maximum(n, 1).astype(F32)
    large = max_exact + (jnp.log(nf / max_exact) / math.log(MAX_DISTANCE / max_exact)
                         * (N_BUCKETS - max_exact)).astype(jnp.int32)
    large = jnp.minimum(large, N_BUCKETS - 1)
    return jnp.where(n < max_exact, n, large)


def _masked_buckets(n_q, n_keys, first_block):
    qi = jnp.arange(n_q, dtype=jnp.int32)[:, None]
    kj = jnp.arange(KEY_PAD, dtype=jnp.int32)[None, :]
    dist = qi + WINDOW - kj
    ok = (dist >= 0) & (dist < WINDOW) & (kj < n_keys)
    if first_block:
        ok = ok & (kj >= WINDOW)
    return jnp.where(ok, _rel_bucket(dist), -1)


def _prompt_front_kernel(sinks_ref, x_ref, w_ref, wq_ref, g_ref, b_ref, ws_ref, bs_ref, bias_ref,
                         mix_ref, tail_ref, u_s, va_s, q_s, kv_s, hv_s, bias_s):
    n_chunks = ROW_TILE // CHUNK
    first_tile = pl.program_id(1) == 0

    lane = lax.broadcasted_iota(jnp.int32, (1, LANES), 1)

    @pl.when(jnp.logical_and(pl.program_id(0) == 0, first_tile))
    def _():
        kv_s[0:CHUNK, :] = jnp.zeros((CHUNK, 2 * KV_WIDTH), BF16)
        for table in range(2):
            for h in range(N_HEADS):
                bias_s[table, h, :, 0:LANES] = LOG2_E * jnp.where(lane == 0, sinks_ref[h],
                                                                  bias_ref[table, h, :, 0:LANES])
                bias_s[table, h, :, LANES:] = LOG2_E * bias_ref[table, h, :, LANES:]

    xb = x_ref[...].astype(BF16)
    tri = (lax.broadcasted_iota(jnp.int32, (CHUNK, CHUNK), 0)
           >= lax.broadcasted_iota(jnp.int32, (CHUNK, CHUNK), 1))
    low_half = lane < HEAD_DIM
    zero = jnp.zeros((), BF16)

    def piece_kv(j):
        cols = slice(j * PIECE, (j + 1) * PIECE)
        kv = _dot(xb, w_ref[:, O_K + j * PIECE:O_K + (j + 1) * PIECE])
        kv_s[CHUNK:CHUNK + ROW_TILE, cols] = kv.astype(BF16)
        tail_ref[:, cols] = kv[ROW_TILE - WINDOW:, :]

    def piece_q(j):
        cols = slice(j * PIECE, (j + 1) * PIECE)
        q_s[:, cols] = (_dot(xb, wq_ref[:, cols]) * (LOG2_E * HEAD_DIM ** -0.5)).astype(BF16)

    def piece_v(j):
        cols = slice(j * PIECE, (j + 1) * PIECE)
        hv_s[:, cols] = _gelu(_dot(xb, w_ref[:, O_V + j * PIECE:O_V + (j + 1) * PIECE]))

    def piece_v_norm():
        va_s[...] = _layer_norm(hv_s[...], g_ref[...], b_ref[...]).astype(BF16)

    def piece_u(j):
        cols = slice(j * PIECE, (j + 1) * PIECE)
        u_s[:, cols] = _gelu(_dot(xb, w_ref[:, O_U + j * PIECE:O_U + (j + 1) * PIECE])).astype(BF16)

    def piece_gate(j):
        cols = slice(MIX_GATE + j * PIECE, MIX_GATE + (j + 1) * PIECE)
        mix_ref[:, cols] = _sigmoid(
            _dot(xb, w_ref[:, O_G + j * PIECE:O_G + (j + 1) * PIECE])).astype(BF16)

    def unit_spatial(c, g):
        rows, cols = slice(c * CHUNK, (c + 1) * CHUNK), slice(g * LANES, (g + 1) * LANES)
        w = jnp.where(tri, ws_ref[g], 0.0).astype(BF16)
        mixed = _dot(w, va_s[rows, cols]) + bs_ref[:, g:g + 1]
        mix_ref[rows, MIX_YA + g * LANES:MIX_YA + (g + 1) * LANES] = (u_s[rows, cols].astype(F32) * mixed).astype(BF16)

    def band_operands(c, gp, hi):
        band = slice(c * CHUNK, (c + 2) * CHUNK)
        keep = low_half if hi == 0 else jnp.logical_not(low_half)
        kn = jnp.where(keep, kv_s[band, gp * LANES:(gp + 1) * LANES], zero)
        vn = jnp.where(keep, kv_s[band, KV_WIDTH + gp * LANES:KV_WIDTH + (gp + 1) * LANES], zero)
        not_sink = lax.broadcasted_iota(jnp.int32, (BF16_ROWS, 1), 0) > 0
        kn = jnp.concatenate([jnp.where(not_sink, kn[:BF16_ROWS], zero), kn[BF16_ROWS:]], axis=0)
        vn = jnp.concatenate([jnp.where(not_sink, vn[:BF16_ROWS], zero), vn[BF16_ROWS:]], axis=0)
        return kn, vn

    def unit_attention(c, gp):
        rows = slice(c * CHUNK, (c + 1) * CHUNK)
        table = jnp.where(first_tile, 1, 0) if c == 0 else 0
        pair_cols = [slice((r * 2 + gp) * LANES, (r * 2 + gp + 1) * LANES) for r in range(Q_PER_KV)]
        q4 = jnp.concatenate([q_s[rows, cols] for cols in pair_cols], axis=0)
        out = None
        for hi in range(2):
            kn, vn = band_operands(c, gp, hi)
            h0 = Q_PER_KV * (2 * gp + hi)
            bias = bias_s[table, h0:h0 + Q_PER_KV].reshape(Q_PER_KV * CHUNK, KEY_PAD)
            s = _dot_nt(q4, kn) + bias
            p = jnp.exp2(s - jnp.max(s, axis=-1, keepdims=True))
            o = _dot(p.astype(BF16), vn) * (1.0 / jnp.sum(p, axis=-1, keepdims=True))
            out = o if out is None else jnp.where(low_half, out, o)
        for r, cols in enumerate(pair_cols):
            mix_ref[rows, MIX_YB + cols.start:MIX_YB + cols.stop] = out[r * CHUNK:(r + 1) * CHUNK].astype(BF16)

    for j in range(2 * KV_WIDTH // PIECE):
        piece_kv(j)
    for j in range(Q_WIDTH // PIECE):
        piece_q(j)
    for j in range(A_WIDTH // PIECE):
        piece_v(j)
    piece_v_norm()
    for j in range(A_WIDTH // PIECE):
        piece_u(j)
    for j in range(2 * D_MODEL // PIECE):
        piece_gate(j)
    for c in range(n_chunks):
        for g in range(A_GROUPS):
            unit_spatial(c, g)
        for gp in range(N_KV_HEADS // 2):
            unit_attention(c, gp)

    kv_s[0:CHUNK, :] = kv_s[ROW_TILE:ROW_TILE + CHUNK, :]


def _prompt_front(sinks, x, w_in, w_q, ln_g, ln_b, w_s, b_s_t, bias, layer, batch, seq):
    n_tiles = seq // ROW_TILE
    tok_spec = lambda width: pl.BlockSpec((ROW_TILE, width), lambda b, i: (b * n_tiles + i, 0))
    vec_spec = pl.BlockSpec((None, 1, A_WIDTH), lambda b, i: (layer, 0, 0))
    rows = batch * seq
    return pl.pallas_call(
        _prompt_front_kernel,
        out_shape=[jax.ShapeDtypeStruct((rows, MIX_WIDTH), BF16),
                   jax.ShapeDtypeStruct((batch, WINDOW, 2 * KV_WIDTH), F32)],
        grid=(batch, n_tiles),
        in_specs=[pl.BlockSpec(memory_space=pltpu.SMEM),
                  tok_spec(D_MODEL),
                  _resident((None, D_MODEL, IN_WIDTH), lambda b, i: (layer, 0, 0)),
                  _resident((None, D_MODEL, Q_WIDTH), lambda b, i: (layer, 0, 0)),
                  vec_spec, vec_spec,
                  _resident((None, A_GROUPS, CHUNK, CHUNK), lambda b, i: (layer, 0, 0, 0)),
                  _resident((None, CHUNK, A_GROUPS), lambda b, i: (layer, 0, 0)),
                  _resident((2, N_HEADS, WINDOW, KEY_PAD), lambda b, i: (0, 0, 0, 0))],
        out_specs=[tok_spec(MIX_WIDTH),
                   pl.BlockSpec((None, WINDOW, 2 * KV_WIDTH), lambda b, i: (b, 0, 0))],
        scratch_shapes=[pltpu.VMEM((ROW_TILE, A_WIDTH), BF16),
                        pltpu.VMEM((ROW_TILE, A_WIDTH), BF16),
                        pltpu.VMEM((ROW_TILE, Q_WIDTH), BF16),
                        pltpu.VMEM((CHUNK + ROW_TILE, 2 * KV_WIDTH), BF16),
                        pltpu.VMEM((ROW_TILE, A_WIDTH), F32),
                        pltpu.VMEM((2, N_HEADS, WINDOW, KEY_PAD), F32)],
        compiler_params=_params(2),
        name="prompt_front",
    )(sinks, x, w_in, w_q, ln_g, ln_b, w_s, b_s_t, bias)


def _inproj_sample_kernel(x_ref, w_ref, wq_ref, g_ref, b_ref, u_ref, va_ref, q_ref, gate_ref, kvt_ref):
    xb = x_ref[...].astype(BF16)

    def proj(c0, c1):
        return _dot(xb, w_ref[:, c0:c1])

    half = A_WIDTH // 2
    for c0 in range(O_U, O_V, half):
        u_ref[:, c0:c0 + half] = _gelu(proj(c0, c0 + half)).astype(BF16)
    va_ref[...] = _layer_norm(_gelu(proj(O_V, O_Q)), g_ref[...], b_ref[...])
    for c0 in range(0, Q_WIDTH, half):
        q_ref[:, c0:c0 + half] = (_dot(xb, wq_ref[:, c0:c0 + half]) * (HEAD_DIM ** -0.5)).astype(BF16)
    kvt_ref[...] = proj(O_K, O_G).T
    for c0 in range(0, 2 * D_MODEL, half):
        gate_ref[:, c0:c0 + half] = _sigmoid(proj(O_G + c0, O_G + c0 + half)).astype(BF16)


def _inproj_sample(x, first_tile, rows, w_in, w_q, ln_g, ln_b, layer):
    row_spec = lambda width: pl.BlockSpec((ROW_TILE, width), lambda i: (i, 0))
    vec_spec = pl.BlockSpec((None, 1, A_WIDTH), lambda i: (layer, 0, 0))
    return pl.pallas_call(
        _inproj_sample_kernel,
        out_shape=[jax.ShapeDtypeStruct((rows, A_WIDTH), BF16),
                   jax.ShapeDtypeStruct((rows, A_WIDTH), F32),
                   jax.ShapeDtypeStruct((rows, Q_WIDTH), BF16),
                   jax.ShapeDtypeStruct((rows, 2 * D_MODEL), BF16),
                   jax.ShapeDtypeStruct((2 * KV_WIDTH, rows), F32)],
        grid=(rows // ROW_TILE,),
        in_specs=[pl.BlockSpec((ROW_TILE, D_MODEL), lambda i: (first_tile + i, 0)),
                  _resident((None, D_MODEL, IN_WIDTH), lambda i: (layer, 0, 0)),
                  _resident((None, D_MODEL, Q_WIDTH), lambda i: (layer, 0, 0)),
                  vec_spec, vec_spec],
        out_specs=[row_spec(A_WIDTH), row_spec(A_WIDTH), row_spec(Q_WIDTH), row_spec(2 * D_MODEL),
                   pl.BlockSpec((2 * KV_WIDTH, ROW_TILE), lambda i: (0, i))],
        compiler_params=_params(1),
        name="inproj_sample",
    )(x, w_in, w_q, ln_g, ln_b)


def _sample_mix_kernel(sinks_ref, u_ref, va_ref, q_ref, kvt_ref, ck_ref, cv_ref, wexp_ref, bsexp_ref,
                       biasc_ref, biasn_ref, prev_k_ref, prev_v_ref, ya_ref, yb_ref, ko_ref, vo_ref, yb_acc):
    del prev_k_ref, prev_v_ref
    n_new = SUBLANES
    rows = SAMPLE_SEQS * n_new

    r_idx = lax.broadcasted_iota(jnp.int32, (rows, rows), 0)
    c_idx = lax.broadcasted_iota(jnp.int32, (rows, rows), 1)
    same_seq_causal = jnp.logical_and(r_idx // n_new == c_idx // n_new, c_idx % n_new <= r_idx % n_new)
    va_b = va_ref[...].astype(BF16)
    for g in range(A_GROUPS):
        cols = slice(g * LANES, (g + 1) * LANES)
        w = jnp.where(same_seq_causal, wexp_ref[g], 0.0).astype(BF16)
        mixed = _dot(w, va_b[:, cols]) + bsexp_ref[:, cols]
        ya_ref[:, cols] = (u_ref[:, cols].astype(F32) * mixed).astype(BF16)

    group_of_lane = lax.broadcasted_iota(jnp.int32, (1, KV_WIDTH), 1) // HEAD_DIM
    row = lax.broadcasted_iota(jnp.int32, (N_HEADS * n_new, 1), 0)
    head_of_row = Q_PER_KV * ((row // n_new) % N_KV_HEADS) + row // (n_new * N_KV_HEADS)
    sink = jnp.zeros((N_HEADS * n_new, 1), F32)
    for h in range(N_HEADS):
        sink = jnp.where(head_of_row == h, sinks_ref[h], sink)
    lane = lax.broadcasted_iota(jnp.int32, (1, LANES), 1)
    seq_of_lane = lane // n_new
    keep_old = lane < WINDOW - n_new
    q32 = q_ref[...].astype(F32)
    k_new, v_new = kvt_ref[0:KV_WIDTH, :], kvt_ref[KV_WIDTH:2 * KV_WIDTH, :]
    k_new_b, v_new_b = k_new.astype(BF16), v_new.astype(BF16)
    bias_c, bias_n = biasc_ref[...], biasn_ref[...]
    for b in range(SAMPLE_SEQS):
        new = slice(b * n_new, (b + 1) * n_new)
        pieces = []
        for r in range(Q_PER_KV):
            blk = q32[new, r * KV_WIDTH:(r + 1) * KV_WIDTH]
            for g in range(N_KV_HEADS):
                pieces.append(jnp.where(group_of_lane == g, blk, 0.0))
        q_rows = jnp.concatenate(pieces, axis=0).astype(BF16)
        k_old, v_old = ck_ref[b], cv_ref[b]
        k_all = jnp.concatenate([k_old.astype(BF16), k_new_b], axis=1)
        v_all = jnp.concatenate([v_old.astype(BF16), v_new_b], axis=1)
        bias = jnp.concatenate([bias_c, jnp.where(seq_of_lane == b, bias_n, NEG_INF)], axis=1)
        s = _dot(q_rows, k_all) + bias
        m = jnp.maximum(jnp.max(s, axis=-1, keepdims=True), sink)
        p = jnp.exp(s - m)
        denom = jnp.sum(p, axis=-1, keepdims=True) + jnp.exp(sink - m)
        o = _dot_nt(p.astype(BF16), v_all) * (1.0 / denom)
        for r in range(Q_PER_KV):
            acc = jnp.zeros((n_new, KV_WIDTH), F32)
            for g in range(N_KV_HEADS):
                r0 = (r * N_KV_HEADS + g) * n_new
                acc = jnp.where(group_of_lane == g, o[r0:r0 + n_new, :], acc)
            yb_acc[new, r * KV_WIDTH:(r + 1) * KV_WIDTH] = acc
        shift_new = (WINDOW - n_new - b * n_new) % LANES
        ko_ref[b] = jnp.where(keep_old, pltpu.roll(k_old, WINDOW - n_new, 1), pltpu.roll(k_new, shift_new, 1))
        vo_ref[b] = jnp.where(keep_old, pltpu.roll(v_old, WINDOW - n_new, 1), pltpu.roll(v_new, shift_new, 1))
    yb_ref[...] = yb_acc[...].astype(BF16)


def _sample_mix(sinks, u, va32, q, kvt, cache_k, cache_v, wexp, bsexp, bias_c, bias_n, prev, layer):
    n_seq = cache_k.shape[1]
    rows = SAMPLE_SEQS * SUBLANES
    row_spec = lambda width: pl.BlockSpec((rows, width), lambda i: (i, 0))
    cache_spec = pl.BlockSpec((None, SAMPLE_SEQS, KV_WIDTH, WINDOW), lambda i: (layer, i, 0, 0))
    table_spec = pl.BlockSpec((N_HEADS * SUBLANES, LANES), lambda i: (0, 0))
    in_specs = [pl.BlockSpec(memory_space=pltpu.SMEM),
                row_spec(A_WIDTH), row_spec(A_WIDTH), row_spec(Q_WIDTH),
                pl.BlockSpec((2 * KV_WIDTH, rows), lambda i: (0, i)),
                cache_spec, cache_spec,
                pl.BlockSpec((None, A_GROUPS, rows, rows), lambda i: (layer, 0, 0, 0)),
                pl.BlockSpec((None, rows, A_WIDTH), lambda i: (layer, 0, 0)),
                table_spec, table_spec,
                pl.BlockSpec(memory_space=pl.ANY), pl.BlockSpec(memory_space=pl.ANY)]
    operands = [sinks, u, va32, q, kvt, cache_k, cache_v, wexp, bsexp, bias_c, bias_n, *prev]
    return pl.pallas_call(
        _sample_mix_kernel,
        out_shape=[jax.ShapeDtypeStruct((n_seq * SUBLANES, A_WIDTH), BF16),
                   jax.ShapeDtypeStruct((n_seq * SUBLANES, Q_WIDTH), BF16),
                   jax.ShapeDtypeStruct(cache_k.shape, F32),
                   jax.ShapeDtypeStruct(cache_v.shape, F32)],
        grid=(n_seq // SAMPLE_SEQS,),
        in_specs=in_specs,
        out_specs=[row_spec(A_WIDTH), row_spec(Q_WIDTH), cache_spec, cache_spec],
        scratch_shapes=[pltpu.VMEM((rows, Q_WIDTH), F32)],
        input_output_aliases={len(operands) - 2: 2, len(operands) - 1: 3},
        compiler_params=_params(1),
        name="sample_mix",
    )(*operands)


def _merge_ffn_kernel(*refs, alpha, n_prompt_tiles, split_x, split_out):
    refs = list(refs)
    x_refs = [refs.pop(0) for _ in range(2 if split_x else 1)]
    (mix_ref, yas_ref, ybs_ref, gates_ref,
     wpa_ref, wpb_ref, wo_ref, wg_ref, wu_ref, wd_ref, ln_ref, *out_refs) = refs
    is_sample = pl.program_id(0) >= n_prompt_tiles

    def pick(prompt, sample_ref):
        return jnp.where(is_sample, sample_ref[...], prompt)

    x = pick(x_refs[0][...], x_refs[1]) if split_x else x_refs[0][...]
    gates = pick(mix_ref[:, MIX_GATE:MIX_WIDTH], gates_ref)
    g_a = gates[:, 0:D_MODEL].astype(F32)
    g_b = gates[:, D_MODEL:2 * D_MODEL].astype(F32)
    y_a = pick(mix_ref[:, MIX_YA:MIX_YA + A_WIDTH], yas_ref)
    y_b = pick(mix_ref[:, MIX_YB:MIX_YB + Q_WIDTH], ybs_ref)
    merged = g_a * _dot(y_a, wpa_ref[...]) + g_b * _dot(y_b, wpb_ref[...])
    mix = _dot(merged.astype(BF16), wo_ref[...])
    x1 = _layer_norm(alpha * x + mix, ln_ref[0:1, :], ln_ref[1:2, :])
    x1b = x1.astype(BF16)
    gate = _dot(x1b, wg_ref[...])
    act = (gate * _sigmoid(gate) * _dot(x1b, wu_ref[...])).astype(BF16)
    ffn = _dot(act, wd_ref[...])
    y = _layer_norm(alpha * x1 + ffn, ln_ref[2:3, :], ln_ref[3:4, :])
    if split_out:
        @pl.when(jnp.logical_not(is_sample))
        def _():
            out_refs[0][...] = y

        @pl.when(is_sample)
        def _():
            out_refs[1][...] = y
    else:
        out_refs[0][...] = y


def _merge_ffn(x, mix, sample, w_pa, w_pb, w_o, w_gate, w_up, w_down, ln_pack, layer, alpha, split_out):
    split_x = isinstance(x, tuple)
    n_p, n_s = mix.shape[0] // ROW_TILE, sample[0].shape[0] // ROW_TILE
    all_rows = lambda width: pl.BlockSpec((ROW_TILE, width), lambda i: (i, 0))
    p_rows = lambda width: pl.BlockSpec((ROW_TILE, width), lambda i: (jnp.minimum(i, n_p - 1), 0))
    s_rows = lambda width: pl.BlockSpec((ROW_TILE, width), lambda i: (jnp.maximum(i - n_p, 0), 0),
                                        pipeline_mode=pl.Buffered(1))
    weight = lambda k, n: _resident((None, k, n), lambda i: (layer, 0, 0))
    widths = (A_WIDTH, Q_WIDTH, 2 * D_MODEL)
    x_specs = [p_rows(D_MODEL), s_rows(D_MODEL)] if split_x else [all_rows(D_MODEL)]
    if split_out:
        out_shape = [jax.ShapeDtypeStruct((n_p * ROW_TILE, D_MODEL), F32),
                     jax.ShapeDtypeStruct((n_s * ROW_TILE, D_MODEL), F32)]
        out_specs = [p_rows(D_MODEL), pl.BlockSpec((ROW_TILE, D_MODEL), lambda i: (jnp.maximum(i - n_p, 0), 0))]
    else:
        out_shape = jax.ShapeDtypeStruct(((n_p + n_s) * ROW_TILE, D_MODEL), F32)
        out_specs = all_rows(D_MODEL)
    return pl.pallas_call(
        functools.partial(_merge_ffn_kernel, alpha=alpha, n_prompt_tiles=n_p, split_x=split_x,
                          split_out=split_out),
        out_shape=out_shape,
        grid=(n_p + n_s,),
        in_specs=x_specs + [p_rows(MIX_WIDTH)] + [s_rows(w) for w in widths] + [
            weight(A_WIDTH, D_MODEL), weight(Q_WIDTH, D_MODEL), weight(D_MODEL, D_MODEL),
            weight(D_MODEL, D_FF), weight(D_MODEL, D_FF), weight(D_FF, D_MODEL),
            pl.BlockSpec((None, 4, D_MODEL), lambda i: (layer, 0, 0))],
        out_specs=out_specs,
        compiler_params=_params(1),
        name="merge_ffn",
    )(*(x if split_x else (x,)), mix, *sample, w_pa, w_pb, w_o, w_gate, w_up, w_down, ln_pack)


def kernel(x_prompt, x_sample, cache_swa_k, cache_swa_v, rel_bias, w_in, ln_v_g, ln_v_b, w_s, b_s,
           sinks, w_pa, w_pb, w_o, ln1_g, ln1_b, w_gate, w_up, w_down, ln2_g, ln2_b):
    depth = w_in.shape[0]
    batch, seq, _ = x_prompt.shape
    n_seq, n_new, _ = x_sample.shape
    assert n_new == SUBLANES and seq % ROW_TILE == 0 and n_seq % SAMPLE_SEQS == 0
    assert (n_seq * n_new) % ROW_TILE == 0
    alpha = (2 * depth) ** 0.25

    def heads_rg(w, axis):
        shape = w.shape
        w = w.reshape(shape[:axis] + (N_KV_HEADS, Q_PER_KV, HEAD_DIM) + shape[axis + 1:])
        return jnp.swapaxes(w, axis, axis + 1).reshape(shape)

    w_in_b = w_in.astype(BF16)
    w_q_b = heads_rg(w_in_b[..., O_Q:O_K], 2)
    w_pa_b, w_o_b = w_pa.astype(BF16), w_o.astype(BF16)
    w_pb_b = heads_rg(w_pb, 1).astype(BF16)
    w_gate_b, w_up_b, w_down_b = w_gate.astype(BF16), w_up.astype(BF16), w_down.astype(BF16)
    ln_v_g3, ln_v_b3 = ln_v_g[:, None, :], ln_v_b[:, None, :]
    ln_pack = jnp.stack([ln1_g, ln1_b, ln2_g, ln2_b], axis=1)
    b_s_t = jnp.swapaxes(b_s, 1, 2)
    wexp = jnp.tile(w_s[:, :, :n_new, :n_new], (1, 1, SAMPLE_SEQS, SAMPLE_SEQS))
    bsexp = jnp.tile(jnp.repeat(jnp.swapaxes(b_s[:, :, :n_new], 1, 2), A_WIDTH // A_GROUPS, axis=-1),
                     (1, SAMPLE_SEQS, 1))

    bias_p = _bias_tables(rel_bias, jnp.stack([_masked_buckets(WINDOW, KEY_PAD, False),
                                               _masked_buckets(WINDOW, KEY_PAD, True)]))
    bias_s = _bias_tables(rel_bias, _masked_buckets(n_new, WINDOW + n_new, False)[None])[0]
    bias_s = bias_s.reshape(N_KV_HEADS, Q_PER_KV, n_new, KEY_PAD)
    bias_s = jnp.swapaxes(bias_s, 0, 1).reshape(N_HEADS * n_new, KEY_PAD)
    bias_c = bias_s[:, :WINDOW]
    bias_n = jnp.tile(bias_s[:, WINDOW:WINDOW + n_new], (1, SAMPLE_SEQS))

    cache_k = jnp.transpose(cache_swa_k, (0, 1, 3, 4, 2)).reshape(depth, n_seq, KV_WIDTH, WINDOW)
    cache_v = jnp.transpose(cache_swa_v, (0, 1, 3, 4, 2)).reshape(depth, n_seq, KV_WIDTH, WINDOW)

    x = (x_prompt.reshape(batch * seq, D_MODEL), x_sample.reshape(n_seq * n_new, D_MODEL))
    n_prompt_tiles, sample_rows = batch * seq // ROW_TILE, n_seq * n_new
    kp_l, vp_l, ga_l = [], [], []
    new_cache = [jnp.zeros(cache_k.shape, F32), jnp.zeros(cache_v.shape, F32)]
    for l in range(depth):
        x_p, x_s, s_tile = (x[0], x[1], 0) if l == 0 else (x, x, n_prompt_tiles)
        mix, kv_tail = _prompt_front(sinks[l], x_p, w_in_b, w_q_b, ln_v_g3, ln_v_b3, w_s, b_s_t, bias_p, l,
                                     batch, seq)
        kp_l.append(kv_tail[..., :KV_WIDTH].reshape(batch, WINDOW, N_KV_HEADS, HEAD_DIM))
        vp_l.append(kv_tail[..., KV_WIDTH:].reshape(batch, WINDOW, N_KV_HEADS, HEAD_DIM))

        u, va32, q, gates_s, kvt = _inproj_sample(x_s, s_tile, sample_rows, w_in_b, w_q_b, ln_v_g3, ln_v_b3, l)
        ya_s, yb_s, *new_cache = _sample_mix(sinks[l], u, va32, q, kvt, cache_k, cache_v, wexp, bsexp,
                                             bias_c, bias_n, new_cache, l)
        ga_l.append(va32.reshape(n_seq, n_new, A_WIDTH))

        x = _merge_ffn(x, mix, (ya_s, yb_s, gates_s), w_pa_b, w_pb_b, w_o_b, w_gate_b, w_up_b,
                       w_down_b, ln_pack, l, alpha, split_out=l == depth - 1)
    xp, xs = x

    def window_major(c):
        c = c.reshape(depth, n_seq, N_KV_HEADS, HEAD_DIM, WINDOW)
        return jnp.transpose(c, (0, 1, 4, 2, 3))

    return (xp.reshape(batch, seq, D_MODEL), xs.reshape(n_seq, n_new, D_MODEL),
            jnp.stack(kp_l), jnp.stack(vp_l), window_major(new_cache[0]), window_major(new_cache[1]),
            jnp.stack(ga_l))
```

```python
import functools
import math

import jax
import jax.numpy as jnp
from jax import lax
from jax.experimental import pallas as pl
from jax.experimental.pallas import tpu as pltpu

D_MODEL = 1024
CHUNK = 128
A_WIDTH = D_MODEL
A_GROUPS = 8
N_HEADS = 16
HEAD_DIM = 64
N_KV_HEADS = 4
Q_PER_KV = N_HEADS // N_KV_HEADS
WINDOW = 128
N_BUCKETS = 32
MAX_DISTANCE = 128
D_FF = 2816
LN_EPS = 1e-5
NEG_INF = -1e30
LOG2_E = math.log2(math.e)

KV_WIDTH = N_KV_HEADS * HEAD_DIM
Q_WIDTH = N_HEADS * HEAD_DIM
O_U = 0
O_V = O_U + A_WIDTH
O_Q = O_V + A_WIDTH
O_K = O_Q + Q_WIDTH
O_G = O_K + 2 * KV_WIDTH
IN_WIDTH = O_G + 2 * D_MODEL

LANES = 128
SUBLANES = 8
BF16_ROWS = 16
ROW_TILE = 512
PIECE = 512
SAMPLE_SEQS = LANES // SUBLANES
KEY_PAD = 2 * WINDOW
MIX_YA, MIX_YB, MIX_GATE = 0, A_WIDTH, A_WIDTH + Q_WIDTH
MIX_WIDTH = MIX_GATE + 2 * D_MODEL
VMEM_LIMIT = 56 * 1024 * 1024

BF16 = jnp.bfloat16
F32 = jnp.float32


def _layer_norm(x, g, b):
    mu = jnp.mean(x, axis=-1, keepdims=True)
    xc = x - mu
    var = jnp.mean(xc * xc, axis=-1, keepdims=True)
    return xc * lax.rsqrt(var + LN_EPS) * g + b


def _gelu(x):
    return jax.nn.gelu(x, approximate=True)


def _sigmoid(x):
    return 0.5 * jnp.tanh(0.5 * x) + 0.5


def _dot(a, b):
    return jnp.dot(a, b, preferred_element_type=F32)


def _dot_nt(a, b):
    return lax.dot_general(a, b, (((1,), (1,)), ((), ())), preferred_element_type=F32)


def _resident(block_shape, index_map):
    return pl.BlockSpec(block_shape, index_map, pipeline_mode=pl.Buffered(1))


def _params(n_axes):
    return pltpu.CompilerParams(dimension_semantics=("arbitrary",) * n_axes,
                                vmem_limit_bytes=VMEM_LIMIT)


def _bias_kernel(rb_ref, bucket_ref, out_ref):
    bk = bucket_ref[...]
    for h in range(N_HEADS):
        acc = jnp.full(bk.shape, NEG_INF, F32)
        for b in range(N_BUCKETS):
            acc = jnp.where(bk == b, rb_ref[b, h], acc)
        out_ref[h] = acc


def _bias_tables(rel_bias, buckets):
    n, t, kp = buckets.shape
    return pl.pallas_call(
        _bias_kernel,
        out_shape=jax.ShapeDtypeStruct((n, N_HEADS, t, kp), F32),
        grid=(n,),
        in_specs=[pl.BlockSpec(memory_space=pltpu.SMEM),
                  pl.BlockSpec((None, t, kp), lambda i: (i, 0, 0))],
        out_specs=pl.BlockSpec((None, N_HEADS, t, kp), lambda i: (i, 0, 0, 0)),
        compiler_params=_params(1),
        name="bias_tables",
    )(rel_bias, buckets)


def _rel_bucket(dist):
    n = jnp.maximum(dist, 0)
    max_exact = N_BUCKETS // 2
    nf = jnp.maximum(n, 1).astype(F32)
    large = max_exact + (jnp.log(nf / max_exact) / math.log(MAX_DISTANCE / max_exact)
                         * (N_BUCKETS - max_exact)).astype(jnp.int32)
    large = jnp.minimum(large, N_BUCKETS - 1)
    return jnp.where(n < max_exact, n, large)


def _masked_buckets(n_q, n_keys, first_block):
    qi = jnp.arange(n_q, dtype=jnp.int32)[:, None]
    kj = jnp.arange(KEY_PAD, dtype=jnp.int32)[None, :]
    dist = qi + WINDOW - kj
    ok = (dist >= 0) & (dist < WINDOW) & (kj < n_keys)
    if first_block:
        ok = ok & (kj >= WINDOW)
    return jnp.where(ok, _rel_bucket(dist), -1)


def _prompt_front_kernel(sinks_ref, x_ref, w_ref, wq_ref, g_ref, b_ref, ws_ref, bs_ref, bias_ref,
                         mix_ref, tail_ref, u_s, va_s, q_s, kv_s, hv_s, bias_s):
    n_chunks = ROW_TILE // CHUNK
    first_tile = pl.program_id(1) == 0

    lane = lax.broadcasted_iota(jnp.int32, (1, LANES), 1)

    @pl.when(jnp.logical_and(pl.program_id(0) == 0, first_tile))
    def _():
        kv_s[0:CHUNK, :] = jnp.zeros((CHUNK, 2 * KV_WIDTH), BF16)
        for table in range(2):
            for h in range(N_HEADS):
                bias_s[table, h, :, 0:LANES] = LOG2_E * jnp.where(lane == 0, sinks_ref[h],
                                                                  bias_ref[table, h, :, 0:LANES])
                bias_s[table, h, :, LANES:] = LOG2_E * bias_ref[table, h, :, LANES:]

    xb = x_ref[...].astype(BF16)
    tri = (lax.broadcasted_iota(jnp.int32, (CHUNK, CHUNK), 0)
           >= lax.broadcasted_iota(jnp.int32, (CHUNK, CHUNK), 1))
    low_half = lane < HEAD_DIM
    zero = jnp.zeros((), BF16)

    def piece_kv(j):
        cols = slice(j * PIECE, (j + 1) * PIECE)
        kv = _dot(xb, w_ref[:, O_K + j * PIECE:O_K + (j + 1) * PIECE])
        kv_s[CHUNK:CHUNK + ROW_TILE, cols] = kv.astype(BF16)
        tail_ref[:, cols] = kv[ROW_TILE - WINDOW:, :]

    def piece_q(j):
        cols = slice(j * PIECE, (j + 1) * PIECE)
        q_s[:, cols] = (_dot(xb, wq_ref[:, cols]) * (LOG2_E * HEAD_DIM ** -0.5)).astype(BF16)

    def piece_v(j):
        cols = slice(j * PIECE, (j + 1) * PIECE)
        hv_s[:, cols] = _gelu(_dot(xb, w_ref[:, O_V + j * PIECE:O_V + (j + 1) * PIECE]))

    def piece_v_norm():
        va_s[...] = _layer_norm(hv_s[...], g_ref[...], b_ref[...]).astype(BF16)

    def piece_u(j):
        cols = slice(j * PIECE, (j + 1) * PIECE)
        u_s[:, cols] = _gelu(_dot(xb, w_ref[:, O_U + j * PIECE:O_U + (j + 1) * PIECE])).astype(BF16)

    def piece_gate(j):
        cols = slice(MIX_GATE + j * PIECE, MIX_GATE + (j + 1) * PIECE)
        mix_ref[:, cols] = _sigmoid(
            _dot(xb, w_ref[:, O_G + j * PIECE:O_G + (j + 1) * PIECE])).astype(BF16)

    def unit_spatial(c, g):
        rows, cols = slice(c * CHUNK, (c + 1) * CHUNK), slice(g * LANES, (g + 1) * LANES)
        w = jnp.where(tri, ws_ref[g], 0.0).astype(BF16)
        mixed = _dot(w, va_s[rows, cols]) + bs_ref[:, g:g + 1]
        mix_ref[rows, MIX_YA + g * LANES:MIX_YA + (g + 1) * LANES] = (u_s[rows, cols].astype(F32) * mixed).astype(BF16)

    def band_operands(c, gp, hi):
        band = slice(c * CHUNK, (c + 2) * CHUNK)
        keep = low_half if hi == 0 else jnp.logical_not(low_half)
        kn = jnp.where(keep, kv_s[band, gp * LANES:(gp + 1) * LANES], zero)
        vn = jnp.where(keep, kv_s[band, KV_WIDTH + gp * LANES:KV_WIDTH + (gp + 1) * LANES], zero)
        not_sink = lax.broadcasted_iota(jnp.int32, (BF16_ROWS, 1), 0) > 0
        kn = jnp.concatenate([jnp.where(not_sink, kn[:BF16_ROWS], zero), kn[BF16_ROWS:]], axis=0)
        vn = jnp.concatenate([jnp.where(not_sink, vn[:BF16_ROWS], zero), vn[BF16_ROWS:]], axis=0)
        return kn, vn

    def unit_attention(c, gp):
        rows = slice(c * CHUNK, (c + 1) * CHUNK)
        table = jnp.where(first_tile, 1, 0) if c == 0 else 0
        pair_cols = [slice((r * 2 + gp) * LANES, (r * 2 + gp + 1) * LANES) for r in range(Q_PER_KV)]
        q4 = jnp.concatenate([q_s[rows, cols] for cols in pair_cols], axis=0)
        out = None
        for hi in range(2):
            kn, vn = band_operands(c, gp, hi)
            h0 = Q_PER_KV * (2 * gp + hi)
            bias = bias_s[table, h0:h0 + Q_PER_KV].reshape(Q_PER_KV * CHUNK, KEY_PAD)
            s = _dot_nt(q4, kn) + bias
            p = jnp.exp2(s - jnp.max(s, axis=-1, keepdims=True))
            o = _dot(p.astype(BF16), vn) * (1.0 / jnp.sum(p, axis=-1, keepdims=True))
            out = o if out is None else jnp.where(low_half, out, o)
        for r, cols in enumerate(pair_cols):
            mix_ref[rows, MIX_YB + cols.start:MIX_YB + cols.stop] = out[r * CHUNK:(r + 1) * CHUNK].astype(BF16)

    for j in range(2 * KV_WIDTH // PIECE):
        piece_kv(j)
    for j in range(Q_WIDTH // PIECE):
        piece_q(j)
    for j in range(A_WIDTH // PIECE):
        piece_v(j)
    piece_v_norm()
    for j in range(A_WIDTH // PIECE):
        piece_u(j)
    for j in range(2 * D_MODEL // PIECE):
        piece_gate(j)
    for c in range(n_chunks):
        for g in range(A_GROUPS):
            unit_spatial(c, g)
        for gp in range(N_KV_HEADS // 2):
            unit_attention(c, gp)

    kv_s[0:CHUNK, :] = kv_s[ROW_TILE:ROW_TILE + CHUNK, :]


def _prompt_front(sinks, x, w_in, w_q, ln_g, ln_b, w_s, b_s_t, bias, layer, batch, seq):
    n_tiles = seq // ROW_TILE
    tok_spec = lambda width: pl.BlockSpec((ROW_TILE, width), lambda b, i: (b * n_tiles + i, 0))
    vec_spec = pl.BlockSpec((None, 1, A_WIDTH), lambda b, i: (layer, 0, 0))
    rows = batch * seq
    return pl.pallas_call(
        _prompt_front_kernel,
        out_shape=[jax.ShapeDtypeStruct((rows, MIX_WIDTH), BF16),
                   jax.ShapeDtypeStruct((batch, WINDOW, 2 * KV_WIDTH), F32)],
        grid=(batch, n_tiles),
        in_specs=[pl.BlockSpec(memory_space=pltpu.SMEM),
                  tok_spec(D_MODEL),
                  _resident((None, D_MODEL, IN_WIDTH), lambda b, i: (layer, 0, 0)),
                  _resident((None, D_MODEL, Q_WIDTH), lambda b, i: (layer, 0, 0)),
                  vec_spec, vec_spec,
                  _resident((None, A_GROUPS, CHUNK, CHUNK), lambda b, i: (layer, 0, 0, 0)),
                  _resident((None, CHUNK, A_GROUPS), lambda b, i: (layer, 0, 0)),
                  _resident((2, N_HEADS, WINDOW, KEY_PAD), lambda b, i: (0, 0, 0, 0))],
        out_specs=[tok_spec(MIX_WIDTH),
                   pl.BlockSpec((None, WINDOW, 2 * KV_WIDTH), lambda b, i: (b, 0, 0))],
        scratch_shapes=[pltpu.VMEM((ROW_TILE, A_WIDTH), BF16),
                        pltpu.VMEM((ROW_TILE, A_WIDTH), BF16),
                        pltpu.VMEM((ROW_TILE, Q_WIDTH), BF16),
                        pltpu.VMEM((CHUNK + ROW_TILE, 2 * KV_WIDTH), BF16),
                        pltpu.VMEM((ROW_TILE, A_WIDTH), F32),
                        pltpu.VMEM((2, N_HEADS, WINDOW, KEY_PAD), F32)],
        compiler_params=_params(2),
        name="prompt_front",
    )(sinks, x, w_in, w_q, ln_g, ln_b, w_s, b_s_t, bias)


def _inproj_sample_kernel(x_ref, w_ref, wq_ref, g_ref, b_ref, u_ref, va_ref, q_ref, gate_ref, kvt_ref):
    xb = x_ref[...].astype(BF16)

    def proj(c0, c1):
        return _dot(xb, w_ref[:, c0:c1])

    half = A_WIDTH // 2
    for c0 in range(O_U, O_V, half):
        u_ref[:, c0:c0 + half] = _gelu(proj(c0, c0 + half)).astype(BF16)
    va_ref[...] = _layer_norm(_gelu(proj(O_V, O_Q)), g_ref[...], b_ref[...])
    for c0 in range(0, Q_WIDTH, half):
        q_ref[:, c0:c0 + half] = (_dot(xb, wq_ref[:, c0:c0 + half]) * (HEAD_DIM ** -0.5)).astype(BF16)
    kvt_ref[...] = proj(O_K, O_G).T
    for c0 in range(0, 2 * D_MODEL, half):
        gate_ref[:, c0:c0 + half] = _sigmoid(proj(O_G + c0, O_G + c0 + half)).astype(BF16)


def _inproj_sample(x, first_tile, rows, w_in, w_q, ln_g, ln_b, layer):
    row_spec = lambda width: pl.BlockSpec((ROW_TILE, width), lambda i: (i, 0))
    vec_spec = pl.BlockSpec((None, 1, A_WIDTH), lambda i: (layer, 0, 0))
    return pl.pallas_call(
        _inproj_sample_kernel,
        out_shape=[jax.ShapeDtypeStruct((rows, A_WIDTH), BF16),
                   jax.ShapeDtypeStruct((rows, A_WIDTH), F32),
                   jax.ShapeDtypeStruct((rows, Q_WIDTH), BF16),
                   jax.ShapeDtypeStruct((rows, 2 * D_MODEL), BF16),
                   jax.ShapeDtypeStruct((2 * KV_WIDTH, rows), F32)],
        grid=(rows // ROW_TILE,),
        in_specs=[pl.BlockSpec((ROW_TILE, D_MODEL), lambda i: (first_tile + i, 0)),
                  _resident((None, D_MODEL, IN_WIDTH), lambda i: (layer, 0, 0)),
                  _resident((None, D_MODEL, Q_WIDTH), lambda i: (layer, 0, 0)),
                  vec_spec, vec_spec],
        out_specs=[row_spec(A_WIDTH), row_spec(A_WIDTH), row_spec(Q_WIDTH), row_spec(2 * D_MODEL),
                   pl.BlockSpec((2 * KV_WIDTH, ROW_TILE), lambda i: (0, i))],
        compiler_params=_params(1),
        name="inproj_sample",
    )(x, w_in, w_q, ln_g, ln_b)


def _sample_mix_kernel(sinks_ref, u_ref, va_ref, q_ref, kvt_ref, ck_ref, cv_ref, wexp_ref, bsexp_ref,
                       biasc_ref, biasn_ref, prev_k_ref, prev_v_ref, ya_ref, yb_ref, ko_ref, vo_ref, yb_acc):
    del prev_k_ref, prev_v_ref
    n_new = SUBLANES
    rows = SAMPLE_SEQS * n_new

    r_idx = lax.broadcasted_iota(jnp.int32, (rows, rows), 0)
    c_idx = lax.broadcasted_iota(jnp.int32, (rows, rows), 1)
    same_seq_causal = jnp.logical_and(r_idx // n_new == c_idx // n_new, c_idx % n_new <= r_idx % n_new)
    va_b = va_ref[...].astype(BF16)
    per_seq = lambda a: jnp.broadcast_to(a[None], (SAMPLE_SEQS,) + a.shape).reshape(rows, a.shape[-1])
    for g in range(A_GROUPS):
        cols = slice(g * LANES, (g + 1) * LANES)
        w = jnp.where(same_seq_causal, per_seq(wexp_ref[g]), 0.0).astype(BF16)
        mixed = _dot(w, va_b[:, cols]) + per_seq(bsexp_ref[:, cols])
        ya_ref[:, cols] = (u_ref[:, cols].astype(F32) * mixed).astype(BF16)

    group_of_lane = lax.broadcasted_iota(jnp.int32, (1, KV_WIDTH), 1) // HEAD_DIM
    row = lax.broadcasted_iota(jnp.int32, (N_HEADS * n_new, 1), 0)
    head_of_row = Q_PER_KV * ((row // n_new) % N_KV_HEADS) + row // (n_new * N_KV_HEADS)
    sink = jnp.zeros((N_HEADS * n_new, 1), F32)
    for h in range(N_HEADS):
        sink = jnp.where(head_of_row == h, sinks_ref[h], sink)
    lane = lax.broadcasted_iota(jnp.int32, (1, LANES), 1)
    seq_of_lane = lane // n_new
    keep_old = lane < WINDOW - n_new
    q32 = q_ref[...].astype(F32)
    k_new, v_new = kvt_ref[0:KV_WIDTH, :], kvt_ref[KV_WIDTH:2 * KV_WIDTH, :]
    k_new_b, v_new_b = k_new.astype(BF16), v_new.astype(BF16)
    bias_c, bias_n = biasc_ref[...], biasn_ref[...]
    for b in range(SAMPLE_SEQS):
        new = slice(b * n_new, (b + 1) * n_new)
        pieces = []
        for r in range(Q_PER_KV):
            blk = q32[new, r * KV_WIDTH:(r + 1) * KV_WIDTH]
            for g in range(N_KV_HEADS):
                pieces.append(jnp.where(group_of_lane == g, blk, 0.0))
        q_rows = jnp.concatenate(pieces, axis=0).astype(BF16)
        k_old, v_old = ck_ref[b], cv_ref[b]
        k_all = jnp.concatenate([k_old.astype(BF16), k_new_b], axis=1)
        v_all = jnp.concatenate([v_old.astype(BF16), v_new_b], axis=1)
        bias = jnp.concatenate([bias_c, jnp.where(seq_of_lane == b, bias_n, NEG_INF)], axis=1)
        s = _dot(q_rows, k_all) + bias
        m = jnp.maximum(jnp.max(s, axis=-1, keepdims=True), sink)
        p = jnp.exp(s - m)
        denom = jnp.sum(p, axis=-1, keepdims=True) + jnp.exp(sink - m)
        o = _dot_nt(p.astype(BF16), v_all) * (1.0 / denom)
        for r in range(Q_PER_KV):
            acc = jnp.zeros((n_new, KV_WIDTH), F32)
            for g in range(N_KV_HEADS):
                r0 = (r * N_KV_HEADS + g) * n_new
                acc = jnp.where(group_of_lane == g, o[r0:r0 + n_new, :], acc)
            yb_acc[new, r * KV_WIDTH:(r + 1) * KV_WIDTH] = acc
        shift_new = (WINDOW - n_new - b * n_new) % LANES
        ko_ref[b] = jnp.where(keep_old, pltpu.roll(k_old, WINDOW - n_new, 1), pltpu.roll(k_new, shift_new, 1))
        vo_ref[b] = jnp.where(keep_old, pltpu.roll(v_old, WINDOW - n_new, 1), pltpu.roll(v_new, shift_new, 1))
    yb_ref[...] = yb_acc[...].astype(BF16)


def _sample_mix(sinks, u, va32, q, kvt, cache_k, cache_v, wexp, bsexp, bias_c, bias_n, prev, layer):
    n_seq = cache_k.shape[1]
    rows = SAMPLE_SEQS * SUBLANES
    row_spec = lambda width: pl.BlockSpec((rows, width), lambda i: (i, 0))
    cache_spec = pl.BlockSpec((None, SAMPLE_SEQS, KV_WIDTH, WINDOW), lambda i: (layer, i, 0, 0))
    table_spec = pl.BlockSpec((N_HEADS * SUBLANES, LANES), lambda i: (0, 0))
    in_specs = [pl.BlockSpec(memory_space=pltpu.SMEM),
                row_spec(A_WIDTH), row_spec(A_WIDTH), row_spec(Q_WIDTH),
                pl.BlockSpec((2 * KV_WIDTH, rows), lambda i: (0, i)),
                cache_spec, cache_spec,
                pl.BlockSpec((None, A_GROUPS, SUBLANES, rows), lambda i: (layer, 0, 0, 0)),
                pl.BlockSpec((None, SUBLANES, A_WIDTH), lambda i: (layer, 0, 0)),
                table_spec, table_spec,
                pl.BlockSpec(memory_space=pl.ANY), pl.BlockSpec(memory_space=pl.ANY)]
    operands = [sinks, u, va32, q, kvt, cache_k, cache_v, wexp, bsexp, bias_c, bias_n, *prev]
    return pl.pallas_call(
        _sample_mix_kernel,
        out_shape=[jax.ShapeDtypeStruct((n_seq * SUBLANES, A_WIDTH), BF16),
                   jax.ShapeDtypeStruct((n_seq * SUBLANES, Q_WIDTH), BF16),
                   jax.ShapeDtypeStruct(cache_k.shape, F32),
                   jax.ShapeDtypeStruct(cache_v.shape, F32)],
        grid=(n_seq // SAMPLE_SEQS,),
        in_specs=in_specs,
        out_specs=[row_spec(A_WIDTH), row_spec(Q_WIDTH), cache_spec, cache_spec],
        scratch_shapes=[pltpu.VMEM((rows, Q_WIDTH), F32)],
        input_output_aliases={len(operands) - 2: 2, len(operands) - 1: 3},
        compiler_params=_params(1),
        name="sample_mix",
    )(*operands)


def _merge_ffn_kernel(*refs, alpha, n_prompt_tiles, split_x, split_out):
    refs = list(refs)
    x_refs = [refs.pop(0) for _ in range(2 if split_x else 1)]
    (mix_ref, yas_ref, ybs_ref, gates_ref,
     wpa_ref, wpb_ref, wo_ref, wg_ref, wu_ref, wd_ref, ln_ref, *out_refs) = refs
    is_sample = pl.program_id(0) >= n_prompt_tiles

    def pick(prompt, sample_ref):
        return jnp.where(is_sample, sample_ref[...], prompt)

    x = pick(x_refs[0][...], x_refs[1]) if split_x else x_refs[0][...]
    gates = pick(mix_ref[:, MIX_GATE:MIX_WIDTH], gates_ref)
    g_a = gates[:, 0:D_MODEL].astype(F32)
    g_b = gates[:, D_MODEL:2 * D_MODEL].astype(F32)
    y_a = pick(mix_ref[:, MIX_YA:MIX_YA + A_WIDTH], yas_ref)
    y_b = pick(mix_ref[:, MIX_YB:MIX_YB + Q_WIDTH], ybs_ref)
    merged = g_a * _dot(y_a, wpa_ref[...]) + g_b * _dot(y_b, wpb_ref[...])
    mix = _dot(merged.astype(BF16), wo_ref[...])
    x1 = _layer_norm(alpha * x + mix, ln_ref[0:1, :], ln_ref[1:2, :])
    x1b = x1.astype(BF16)
    gate = _dot(x1b, wg_ref[...])
    act = (gate * _sigmoid(gate) * _dot(x1b, wu_ref[...])).astype(BF16)
    ffn = _dot(act, wd_ref[...])
    y = _layer_norm(alpha * x1 + ffn, ln_ref[2:3, :], ln_ref[3:4, :])
    if split_out:
        @pl.when(jnp.logical_not(is_sample))
        def _():
            out_refs[0][...] = y

        @pl.when(is_sample)
        def _():
            out_refs[1][...] = y
    else:
        out_refs[0][...] = y


def _merge_ffn(x, mix, sample, w_pa, w_pb, w_o, w_gate, w_up, w_down, ln_pack, layer, alpha, split_out):
    split_x = isinstance(x, tuple)
    n_p, n_s = mix.shape[0] // ROW_TILE, sample[0].shape[0] // ROW_TILE
    all_rows = lambda width: pl.BlockSpec((ROW_TILE, width), lambda i: (i, 0))
    p_rows = lambda width: pl.BlockSpec((ROW_TILE, width), lambda i: (jnp.minimum(i, n_p - 1), 0))
    s_rows = lambda width: pl.BlockSpec((ROW_TILE, width), lambda i: (jnp.maximum(i - n_p, 0), 0),
                                        pipeline_mode=pl.Buffered(1))
    weight = lambda k, n: _resident((None, k, n), lambda i: (layer, 0, 0))
    widths = (A_WIDTH, Q_WIDTH, 2 * D_MODEL)
    x_specs = [p_rows(D_MODEL), s_rows(D_MODEL)] if split_x else [all_rows(D_MODEL)]
    if split_out:
        out_shape = [jax.ShapeDtypeStruct((n_p * ROW_TILE, D_MODEL), F32),
                     jax.ShapeDtypeStruct((n_s * ROW_TILE, D_MODEL), F32)]
        out_specs = [p_rows(D_MODEL), pl.BlockSpec((ROW_TILE, D_MODEL), lambda i: (jnp.maximum(i - n_p, 0), 0))]
    else:
        out_shape = jax.ShapeDtypeStruct(((n_p + n_s) * ROW_TILE, D_MODEL), F32)
        out_specs = all_rows(D_MODEL)
    return pl.pallas_call(
        functools.partial(_merge_ffn_kernel, alpha=alpha, n_prompt_tiles=n_p, split_x=split_x,
                          split_out=split_out),
        out_shape=out_shape,
        grid=(n_p + n_s,),
        in_specs=x_specs + [p_rows(MIX_WIDTH)] + [s_rows(w) for w in widths] + [
            weight(A_WIDTH, D_MODEL), weight(Q_WIDTH, D_MODEL), weight(D_MODEL, D_MODEL),
            weight(D_MODEL, D_FF), weight(D_MODEL, D_FF), weight(D_FF, D_MODEL),
            pl.BlockSpec((None, 4, D_MODEL), lambda i: (layer, 0, 0))],
        out_specs=out_specs,
        compiler_params=_params(1),
        name="merge_ffn",
    )(*(x if split_x else (x,)), mix, *sample, w_pa, w_pb, w_o, w_gate, w_up, w_down, ln_pack)


def kernel(x_prompt, x_sample, cache_swa_k, cache_swa_v, rel_bias, w_in, ln_v_g, ln_v_b, w_s, b_s,
           sinks, w_pa, w_pb, w_o, ln1_g, ln1_b, w_gate, w_up, w_down, ln2_g, ln2_b):
    depth = w_in.shape[0]
    batch, seq, _ = x_prompt.shape
    n_seq, n_new, _ = x_sample.shape
    assert n_new == SUBLANES and seq % ROW_TILE == 0 and n_seq % SAMPLE_SEQS == 0
    assert (n_seq * n_new) % ROW_TILE == 0
    alpha = (2 * depth) ** 0.25

    def heads_rg(w, axis):
        shape = w.shape
        w = w.reshape(shape[:axis] + (N_KV_HEADS, Q_PER_KV, HEAD_DIM) + shape[axis + 1:])
        return jnp.swapaxes(w, axis, axis + 1).reshape(shape)

    w_in_b = w_in.astype(BF16)
    w_q_b = heads_rg(w_in[..., O_Q:O_K], 2).astype(BF16)
    w_pa_b, w_o_b = w_pa.astype(BF16), w_o.astype(BF16)
    w_pb_b = heads_rg(w_pb, 1).astype(BF16)
    w_gate_b, w_up_b, w_down_b = w_gate.astype(BF16), w_up.astype(BF16), w_down.astype(BF16)
    ln_v_g3, ln_v_b3 = ln_v_g[:, None, :], ln_v_b[:, None, :]
    ln_pack = jnp.stack([ln1_g, ln1_b, ln2_g, ln2_b], axis=1)
    b_s_t = jnp.swapaxes(b_s, 1, 2)
    wexp = jnp.tile(w_s[:, :, :n_new, :n_new], (1, 1, 1, SAMPLE_SEQS))
    bsexp = jnp.repeat(jnp.swapaxes(b_s[:, :, :n_new], 1, 2), A_WIDTH // A_GROUPS, axis=-1)

    bias_p = _bias_tables(rel_bias, jnp.stack([_masked_buckets(WINDOW, KEY_PAD, False),
                                               _masked_buckets(WINDOW, KEY_PAD, True)]))
    bias_s = _bias_tables(rel_bias, _masked_buckets(n_new, WINDOW + n_new, False)[None])[0]
    bias_s = bias_s.reshape(N_KV_HEADS, Q_PER_KV, n_new, KEY_PAD)
    bias_s = jnp.swapaxes(bias_s, 0, 1).reshape(N_HEADS * n_new, KEY_PAD)
    bias_c = bias_s[:, :WINDOW]
    bias_n = jnp.tile(bias_s[:, WINDOW:WINDOW + n_new], (1, SAMPLE_SEQS))

    cache_k = jnp.transpose(cache_swa_k, (0, 1, 3, 4, 2)).reshape(depth, n_seq, KV_WIDTH, WINDOW)
    cache_v = jnp.transpose(cache_swa_v, (0, 1, 3, 4, 2)).reshape(depth, n_seq, KV_WIDTH, WINDOW)

    x = (x_prompt.reshape(batch * seq, D_MODEL), x_sample.reshape(n_seq * n_new, D_MODEL))
    n_prompt_tiles, sample_rows = batch * seq // ROW_TILE, n_seq * n_new
    kp_l, vp_l, ga_l = [], [], []
    new_cache = [jnp.zeros(cache_k.shape, F32), jnp.zeros(cache_v.shape, F32)]
    for l in range(depth):
        x_p, x_s, s_tile = (x[0], x[1], 0) if l == 0 else (x, x, n_prompt_tiles)
        mix, kv_tail = _prompt_front(sinks[l], x_p, w_in_b, w_q_b, ln_v_g3, ln_v_b3, w_s, b_s_t, bias_p, l,
                                     batch, seq)
        kp_l.append(kv_tail[..., :KV_WIDTH].reshape(batch, WINDOW, N_KV_HEADS, HEAD_DIM))
        vp_l.append(kv_tail[..., KV_WIDTH:].reshape(batch, WINDOW, N_KV_HEADS, HEAD_DIM))

        u, va32, q, gates_s, kvt = _inproj_sample(x_s, s_tile, sample_rows, w_in_b, w_q_b, ln_v_g3, ln_v_b3, l)
        ya_s, yb_s, *new_cache = _sample_mix(sinks[l], u, va32, q, kvt, cache_k, cache_v, wexp, bsexp,
                                             bias_c, bias_n, new_cache, l)
        ga_l.append(va32.reshape(n_seq, n_new, A_WIDTH))

        x = _merge_ffn(x, mix, (ya_s, yb_s, gates_s), w_pa_b, w_pb_b, w_o_b, w_gate_b, w_up_b,
                       w_down_b, ln_pack, l, alpha, split_out=l == depth - 1)
    xp, xs = x

    def window_major(c):
        c = c.reshape(depth, n_seq, N_KV_HEADS, HEAD_DIM, WINDOW)
        return jnp.transpose(c, (0, 1, 4, 2, 3))

    return (xp.reshape(batch, seq, D_MODEL), xs.reshape(n_seq, n_new, D_MODEL),
            jnp.stack(kp_l), jnp.stack(vp_l), window_major(new_cache[0]), window_major(new_cache[1]),
            jnp.stack(ga_l))
```

```python
import functools
import math

import jax
import jax.numpy as jnp
from jax import lax
from jax.experimental import pallas as pl
from jax.experimental.pallas import tpu as pltpu

D_MODEL = 1024
CHUNK = 128
A_WIDTH = D_MODEL
A_GROUPS = 8
N_HEADS = 16
HEAD_DIM = 64
N_KV_HEADS = 4
Q_PER_KV = N_HEADS // N_KV_HEADS
WINDOW = 128
N_BUCKETS = 32
MAX_DISTANCE = 128
D_FF = 2816
LN_EPS = 1e-5
NEG_INF = -1e30
LOG2_E = math.log2(math.e)

KV_WIDTH = N_KV_HEADS * HEAD_DIM
Q_WIDTH = N_HEADS * HEAD_DIM
O_U = 0
O_V = O_U + A_WIDTH
O_Q = O_V + A_WIDTH
O_K = O_Q + Q_WIDTH
O_G = O_K + 2 * KV_WIDTH
IN_WIDTH = O_G + 2 * D_MODEL

LANES = 128
SUBLANES = 8
BF16_ROWS = 16
ROW_TILE = 512
PIECE = 512
SAMPLE_SEQS = LANES // SUBLANES
KEY_PAD = 2 * WINDOW
MIX_YA, MIX_YB, MIX_GATE = 0, A_WIDTH, A_WIDTH + Q_WIDTH
MIX_WIDTH = MIX_GATE + 2 * D_MODEL
VMEM_LIMIT = 56 * 1024 * 1024

BF16 = jnp.bfloat16
F32 = jnp.float32


def _layer_norm(x, g, b):
    mu = jnp.mean(x, axis=-1, keepdims=True)
    xc = x - mu
    var = jnp.mean(xc * xc, axis=-1, keepdims=True)
    return xc * lax.rsqrt(var + LN_EPS) * g + b


def _gelu(x):
    return jax.nn.gelu(x, approximate=True)


def _sigmoid(x):
    return 0.5 * jnp.tanh(0.5 * x) + 0.5


def _dot(a, b):
    return jnp.dot(a, b, preferred_element_type=F32)


def _dot_nt(a, b):
    return lax.dot_general(a, b, (((1,), (1,)), ((), ())), preferred_element_type=F32)


def _resident(block_shape, index_map):
    return pl.BlockSpec(block_shape, index_map, pipeline_mode=pl.Buffered(1))


def _params(n_axes):
    return pltpu.CompilerParams(dimension_semantics=("arbitrary",) * n_axes,
                                vmem_limit_bytes=VMEM_LIMIT)


def _bias_kernel(rb_ref, bucket_ref, out_ref):
    bk = bucket_ref[...]
    for h in range(N_HEADS):
        acc = jnp.full(bk.shape, NEG_INF, F32)
        for b in range(N_BUCKETS):
            acc = jnp.where(bk == b, rb_ref[b, h], acc)
        out_ref[h] = acc


def _bias_tables(rel_bias, buckets):
    n, t, kp = buckets.shape
    return pl.pallas_call(
        _bias_kernel,
        out_shape=jax.ShapeDtypeStruct((n, N_HEADS, t, kp), F32),
        grid=(n,),
        in_specs=[pl.BlockSpec(memory_space=pltpu.SMEM),
                  pl.BlockSpec((None, t, kp), lambda i: (i, 0, 0))],
        out_specs=pl.BlockSpec((None, N_HEADS, t, kp), lambda i: (i, 0, 0, 0)),
        compiler_params=_params(1),
        name="bias_tables",
    )(rel_bias, buckets)


def _rel_bucket(dist):
    n = jnp.maximum(dist, 0)
    max_exact = N_BUCKETS // 2
    nf = jnp.maximum(n, 1).astype(F32)
    large = max_exact + (jnp.log(nf / max_exact) / math.log(MAX_DISTANCE / max_exact)
                         * (N_BUCKETS - max_exact)).astype(jnp.int32)
    large = jnp.minimum(large, N_BUCKETS - 1)
    return jnp.where(n < max_exact, n, large)


def _masked_buckets(n_q, n_keys, first_block):
    qi = jnp.arange(n_q, dtype=jnp.int32)[:, None]
    kj = jnp.arange(KEY_PAD, dtype=jnp.int32)[None, :]
    dist = qi + WINDOW - kj
    ok = (dist >= 0) & (dist < WINDOW) & (kj < n_keys)
    if first_block:
        ok = ok & (kj >= WINDOW)
    return jnp.where(ok, _rel_bucket(dist), -1)


def _prompt_front_kernel(sinks_ref, x_ref, w_ref, wq_ref, g_ref, b_ref, ws_ref, bs_ref, bias_ref,
                         mix_ref, tail_ref, u_s, va_s, q_s, kv_s, hv_s, bias_s):
    n_chunks = ROW_TILE // CHUNK
    first_tile = pl.program_id(1) == 0

    lane = lax.broadcasted_iota(jnp.int32, (1, LANES), 1)

    @pl.when(jnp.logical_and(pl.program_id(0) == 0, first_tile))
    def _():
        kv_s[0:CHUNK, :] = jnp.zeros((CHUNK, 2 * KV_WIDTH), BF16)
        for table in range(2):
            for h in range(N_HEADS):
                bias_s[table, h, :, 0:LANES] = LOG2_E * jnp.where(lane == 0, sinks_ref[h],
                                                                  bias_ref[table, h, :, 0:LANES])
                bias_s[table, h, :, LANES:] = LOG2_E * bias_ref[table, h, :, LANES:]

    xb = x_ref[...].astype(BF16)
    tri = (lax.broadcasted_iota(jnp.int32, (CHUNK, CHUNK), 0)
           >= lax.broadcasted_iota(jnp.int32, (CHUNK, CHUNK), 1))
    low_half = lane < HEAD_DIM
    zero = jnp.zeros((), BF16)

    def piece_kv(j):
        cols = slice(j * PIECE, (j + 1) * PIECE)
        kv = _dot(xb, w_ref[:, O_K + j * PIECE:O_K + (j + 1) * PIECE])
        kv_s[CHUNK:CHUNK + ROW_TILE, cols] = kv.astype(BF16)
        tail_ref[:, cols] = kv[ROW_TILE - WINDOW:, :]

    def piece_q(j):
        cols = slice(j * PIECE, (j + 1) * PIECE)
        q_s[:, cols] = (_dot(xb, wq_ref[:, cols]) * (LOG2_E * HEAD_DIM ** -0.5)).astype(BF16)

    def piece_v(j):
        cols = slice(j * PIECE, (j + 1) * PIECE)
        hv_s[:, cols] = _gelu(_dot(xb, w_ref[:, O_V + j * PIECE:O_V + (j + 1) * PIECE]))

    def piece_v_norm():
        va_s[...] = _layer_norm(hv_s[...], g_ref[...], b_ref[...]).astype(BF16)

    def piece_u(j):
        cols = slice(j * PIECE, (j + 1) * PIECE)
        u_s[:, cols] = _gelu(_dot(xb, w_ref[:, O_U + j * PIECE:O_U + (j + 1) * PIECE])).astype(BF16)

    def piece_gate(j):
        cols = slice(MIX_GATE + j * PIECE, MIX_GATE + (j + 1) * PIECE)
        mix_ref[:, cols] = _sigmoid(
            _dot(xb, w_ref[:, O_G + j * PIECE:O_G + (j + 1) * PIECE])).astype(BF16)

    def unit_spatial(c, g):
        rows, cols = slice(c * CHUNK, (c + 1) * CHUNK), slice(g * LANES, (g + 1) * LANES)
        w = jnp.where(tri, ws_ref[g], 0.0).astype(BF16)
        mixed = _dot(w, va_s[rows, cols]) + bs_ref[:, g:g + 1]
        mix_ref[rows, MIX_YA + g * LANES:MIX_YA + (g + 1) * LANES] = (u_s[rows, cols].astype(F32) * mixed).astype(BF16)

    def band_operands(c, gp, hi):
        band = slice(c * CHUNK, (c + 2) * CHUNK)
        keep = low_half if hi == 0 else jnp.logical_not(low_half)
        kn = jnp.where(keep, kv_s[band, gp * LANES:(gp + 1) * LANES], zero)
        vn = jnp.where(keep, kv_s[band, KV_WIDTH + gp * LANES:KV_WIDTH + (gp + 1) * LANES], zero)
        not_sink = lax.broadcasted_iota(jnp.int32, (BF16_ROWS, 1), 0) > 0
        kn = jnp.concatenate([jnp.where(not_sink, kn[:BF16_ROWS], zero), kn[BF16_ROWS:]], axis=0)
        vn = jnp.concatenate([jnp.where(not_sink, vn[:BF16_ROWS], zero), vn[BF16_ROWS:]], axis=0)
        return kn, vn

    def unit_attention(c, gp):
        rows = slice(c * CHUNK, (c + 1) * CHUNK)
        table = jnp.where(first_tile, 1, 0) if c == 0 else 0
        pair_cols = [slice((r * 2 + gp) * LANES, (r * 2 + gp + 1) * LANES) for r in range(Q_PER_KV)]
        q4 = jnp.concatenate([q_s[rows, cols] for cols in pair_cols], axis=0)
        out = None
        for hi in range(2):
            kn, vn = band_operands(c, gp, hi)
            h0 = Q_PER_KV * (2 * gp + hi)
            bias = bias_s[table, h0:h0 + Q_PER_KV].reshape(Q_PER_KV * CHUNK, KEY_PAD)
            s = _dot_nt(q4, kn) + bias
            p = jnp.exp2(s - jnp.max(s, axis=-1, keepdims=True))
            o = _dot(p.astype(BF16), vn) * (1.0 / jnp.sum(p, axis=-1, keepdims=True))
            out = o if out is None else jnp.where(low_half, out, o)
        for r, cols in enumerate(pair_cols):
            mix_ref[rows, MIX_YB + cols.start:MIX_YB + cols.stop] = out[r * CHUNK:(r + 1) * CHUNK].astype(BF16)

    for j in range(2 * KV_WIDTH // PIECE):
        piece_kv(j)
    for j in range(Q_WIDTH // PIECE):
        piece_q(j)
    for j in range(A_WIDTH // PIECE):
        piece_v(j)
    piece_v_norm()
    for j in range(A_WIDTH // PIECE):
        piece_u(j)
    for j in range(2 * D_MODEL // PIECE):
        piece_gate(j)
    for c in range(n_chunks):
        for g in range(A_GROUPS):
            unit_spatial(c, g)
        for gp in range(N_KV_HEADS // 2):
            unit_attention(c, gp)

    kv_s[0:CHUNK, :] = kv_s[ROW_TILE:ROW_TILE + CHUNK, :]


def _prompt_front(sinks, x, w_in, w_q, ln_g, ln_b, w_s, b_s_t, bias, layer, batch, seq):
    n_tiles = seq // ROW_TILE
    tok_spec = lambda width: pl.BlockSpec((ROW_TILE, width), lambda b, i: (b * n_tiles + i, 0))
    vec_spec = pl.BlockSpec((None, 1, A_WIDTH), lambda b, i: (layer, 0, 0))
    rows = batch * seq
    return pl.pallas_call(
        _prompt_front_kernel,
        out_shape=[jax.ShapeDtypeStruct((rows, MIX_WIDTH), BF16),
                   jax.ShapeDtypeStruct((batch, WINDOW, 2 * KV_WIDTH), F32)],
        grid=(batch, n_tiles),
        in_specs=[pl.BlockSpec(memory_space=pltpu.SMEM),
                  tok_spec(D_MODEL),
                  _resident((None, D_MODEL, IN_WIDTH), lambda b, i: (layer, 0, 0)),
                  _resident((None, D_MODEL, Q_WIDTH), lambda b, i: (layer, 0, 0)),
                  vec_spec, vec_spec,
                  _resident((None, A_GROUPS, CHUNK, CHUNK), lambda b, i: (layer, 0, 0, 0)),
                  _resident((None, CHUNK, A_GROUPS), lambda b, i: (layer, 0, 0)),
                  _resident((2, N_HEADS, WINDOW, KEY_PAD), lambda b, i: (0, 0, 0, 0))],
        out_specs=[tok_spec(MIX_WIDTH),
                   pl.BlockSpec((None, WINDOW, 2 * KV_WIDTH), lambda b, i: (b, 0, 0))],
        scratch_shapes=[pltpu.VMEM((ROW_TILE, A_WIDTH), BF16),
                        pltpu.VMEM((ROW_TILE, A_WIDTH), BF16),
                        pltpu.VMEM((ROW_TILE, Q_WIDTH), BF16),
                        pltpu.VMEM((CHUNK + ROW_TILE, 2 * KV_WIDTH), BF16),
                        pltpu.VMEM((ROW_TILE, A_WIDTH), F32),
                        pltpu.VMEM((2, N_HEADS, WINDOW, KEY_PAD), F32)],
        compiler_params=_params(2),
        name="prompt_front",
    )(sinks, x, w_in, w_q, ln_g, ln_b, w_s, b_s_t, bias)


def _inproj_sample_kernel(x_ref, w_ref, wq_ref, g_ref, b_ref, u_ref, va_ref, q_ref, gate_ref, kvt_ref):
    xb = x_ref[...].astype(BF16)

    def proj(c0, c1):
        return _dot(xb, w_ref[:, c0:c1])

    half = A_WIDTH // 2
    for c0 in range(O_U, O_V, half):
        u_ref[:, c0:c0 + half] = _gelu(proj(c0, c0 + half)).astype(BF16)
    va_ref[...] = _layer_norm(_gelu(proj(O_V, O_Q)), g_ref[...], b_ref[...])
    for c0 in range(0, Q_WIDTH, half):
        q_ref[:, c0:c0 + half] = (_dot(xb, wq_ref[:, c0:c0 + half]) * (LOG2_E * HEAD_DIM ** -0.5)).astype(BF16)
    kvt_ref[...] = proj(O_K, O_G).T
    for c0 in range(0, 2 * D_MODEL, half):
        gate_ref[:, c0:c0 + half] = _sigmoid(proj(O_G + c0, O_G + c0 + half)).astype(BF16)


def _inproj_sample(x, first_tile, rows, w_in, w_q, ln_g, ln_b, layer):
    row_spec = lambda width: pl.BlockSpec((ROW_TILE, width), lambda i: (i, 0))
    vec_spec = pl.BlockSpec((None, 1, A_WIDTH), lambda i: (layer, 0, 0))
    return pl.pallas_call(
        _inproj_sample_kernel,
        out_shape=[jax.ShapeDtypeStruct((rows, A_WIDTH), BF16),
                   jax.ShapeDtypeStruct((rows, A_WIDTH), F32),
                   jax.ShapeDtypeStruct((rows, Q_WIDTH), BF16),
                   jax.ShapeDtypeStruct((rows, 2 * D_MODEL), BF16),
                   jax.ShapeDtypeStruct((2 * KV_WIDTH, rows), F32)],
        grid=(rows // ROW_TILE,),
        in_specs=[pl.BlockSpec((ROW_TILE, D_MODEL), lambda i: (first_tile + i, 0)),
                  _resident((None, D_MODEL, IN_WIDTH), lambda i: (layer, 0, 0)),
                  _resident((None, D_MODEL, Q_WIDTH), lambda i: (layer, 0, 0)),
                  vec_spec, vec_spec],
        out_specs=[row_spec(A_WIDTH), row_spec(A_WIDTH), row_spec(Q_WIDTH), row_spec(2 * D_MODEL),
                   pl.BlockSpec((2 * KV_WIDTH, ROW_TILE), lambda i: (0, i))],
        compiler_params=_params(1),
        name="inproj_sample",
    )(x, w_in, w_q, ln_g, ln_b)


def _sample_mix_kernel(sinks_ref, u_ref, va_ref, q_ref, kvt_ref, ck_ref, cv_ref, wexp_ref, bsexp_ref,
                       biasc_ref, biasn_ref, prev_k_ref, prev_v_ref, ya_ref, yb_ref, ko_ref, vo_ref, yb_acc):
    del prev_k_ref, prev_v_ref
    n_new = SUBLANES
    rows = SAMPLE_SEQS * n_new

    r_idx = lax.broadcasted_iota(jnp.int32, (rows, rows), 0)
    c_idx = lax.broadcasted_iota(jnp.int32, (rows, rows), 1)
    same_seq_causal = jnp.logical_and(r_idx // n_new == c_idx // n_new, c_idx % n_new <= r_idx % n_new)
    va_b = va_ref[...].astype(BF16)
    per_seq = lambda a: jnp.broadcast_to(a[None], (SAMPLE_SEQS,) + a.shape).reshape(rows, a.shape[-1])
    for g in range(A_GROUPS):
        cols = slice(g * LANES, (g + 1) * LANES)
        w = jnp.where(same_seq_causal, per_seq(wexp_ref[g]), 0.0).astype(BF16)
        mixed = _dot(w, va_b[:, cols]) + per_seq(bsexp_ref[:, cols])
        ya_ref[:, cols] = (u_ref[:, cols].astype(F32) * mixed).astype(BF16)

    group_of_lane = lax.broadcasted_iota(jnp.int32, (1, KV_WIDTH), 1) // HEAD_DIM
    row = lax.broadcasted_iota(jnp.int32, (N_HEADS * n_new, 1), 0)
    head_of_row = Q_PER_KV * ((row // n_new) % N_KV_HEADS) + row // (n_new * N_KV_HEADS)
    sink = jnp.zeros((N_HEADS * n_new, 1), F32)
    for h in range(N_HEADS):
        sink = jnp.where(head_of_row == h, sinks_ref[h], sink)
    lane = lax.broadcasted_iota(jnp.int32, (1, LANES), 1)
    seq_of_lane = lane // n_new
    keep_old = lane < WINDOW - n_new
    q32 = q_ref[...].astype(F32)
    k_new, v_new = kvt_ref[0:KV_WIDTH, :], kvt_ref[KV_WIDTH:2 * KV_WIDTH, :]
    k_new_b, v_new_b = k_new.astype(BF16), v_new.astype(BF16)
    bias_c = jnp.where(lane == 0, LOG2_E * sink, LOG2_E * biasc_ref[...])
    bias_n = LOG2_E * biasn_ref[...]
    zero = jnp.zeros((), BF16)
    for b in range(SAMPLE_SEQS):
        new = slice(b * n_new, (b + 1) * n_new)
        pieces = []
        for r in range(Q_PER_KV):
            blk = q32[new, r * KV_WIDTH:(r + 1) * KV_WIDTH]
            for g in range(N_KV_HEADS):
                pieces.append(jnp.where(group_of_lane == g, blk, 0.0))
        q_rows = jnp.concatenate(pieces, axis=0).astype(BF16)
        k_old, v_old = ck_ref[b], cv_ref[b]
        k_all = jnp.concatenate([jnp.where(lane == 0, zero, k_old.astype(BF16)), k_new_b], axis=1)
        v_all = jnp.concatenate([jnp.where(lane == 0, zero, v_old.astype(BF16)), v_new_b], axis=1)
        bias = jnp.concatenate([bias_c, jnp.where(seq_of_lane == b, bias_n, NEG_INF)], axis=1)
        s = _dot(q_rows, k_all) + bias
        p = jnp.exp2(s - jnp.max(s, axis=-1, keepdims=True))
        o = _dot_nt(p.astype(BF16), v_all) * (1.0 / jnp.sum(p, axis=-1, keepdims=True))
        for r in range(Q_PER_KV):
            acc = jnp.zeros((n_new, KV_WIDTH), F32)
            for g in range(N_KV_HEADS):
                r0 = (r * N_KV_HEADS + g) * n_new
                acc = jnp.where(group_of_lane == g, o[r0:r0 + n_new, :], acc)
            yb_acc[new, r * KV_WIDTH:(r + 1) * KV_WIDTH] = acc
        shift_new = (WINDOW - n_new - b * n_new) % LANES
        ko_ref[b] = jnp.where(keep_old, pltpu.roll(k_old, WINDOW - n_new, 1), pltpu.roll(k_new, shift_new, 1))
        vo_ref[b] = jnp.where(keep_old, pltpu.roll(v_old, WINDOW - n_new, 1), pltpu.roll(v_new, shift_new, 1))
    yb_ref[...] = yb_acc[...].astype(BF16)


def _sample_mix(sinks, u, va32, q, kvt, cache_k, cache_v, wexp, bsexp, bias_c, bias_n, prev, layer):
    n_seq = cache_k.shape[1]
    rows = SAMPLE_SEQS * SUBLANES
    row_spec = lambda width: pl.BlockSpec((rows, width), lambda i: (i, 0))
    cache_spec = pl.BlockSpec((None, SAMPLE_SEQS, KV_WIDTH, WINDOW), lambda i: (layer, i, 0, 0))
    table_spec = pl.BlockSpec((N_HEADS * SUBLANES, LANES), lambda i: (0, 0))
    in_specs = [pl.BlockSpec(memory_space=pltpu.SMEM),
                row_spec(A_WIDTH), row_spec(A_WIDTH), row_spec(Q_WIDTH),
                pl.BlockSpec((2 * KV_WIDTH, rows), lambda i: (0, i)),
                cache_spec, cache_spec,
                pl.BlockSpec((None, A_GROUPS, SUBLANES, rows), lambda i: (layer, 0, 0, 0)),
                pl.BlockSpec((None, SUBLANES, A_WIDTH), lambda i: (layer, 0, 0)),
                table_spec, table_spec,
                pl.BlockSpec(memory_space=pl.ANY), pl.BlockSpec(memory_space=pl.ANY)]
    operands = [sinks, u, va32, q, kvt, cache_k, cache_v, wexp, bsexp, bias_c, bias_n, *prev]
    return pl.pallas_call(
        _sample_mix_kernel,
        out_shape=[jax.ShapeDtypeStruct((n_seq * SUBLANES, A_WIDTH), BF16),
                   jax.ShapeDtypeStruct((n_seq * SUBLANES, Q_WIDTH), BF16),
                   jax.ShapeDtypeStruct(cache_k.shape, F32),
                   jax.ShapeDtypeStruct(cache_v.shape, F32)],
        grid=(n_seq // SAMPLE_SEQS,),
        in_specs=in_specs,
        out_specs=[row_spec(A_WIDTH), row_spec(Q_WIDTH), cache_spec, cache_spec],
        scratch_shapes=[pltpu.VMEM((rows, Q_WIDTH), F32)],
        input_output_aliases={len(operands) - 2: 2, len(operands) - 1: 3},
        compiler_params=_params(1),
        name="sample_mix",
    )(*operands)


def _merge_ffn_kernel(*refs, alpha, n_prompt_tiles, split_x, split_out):
    refs = list(refs)
    x_refs = [refs.pop(0) for _ in range(2 if split_x else 1)]
    (mix_ref, yas_ref, ybs_ref, gates_ref,
     wpa_ref, wpb_ref, wo_ref, wg_ref, wu_ref, wd_ref, ln_ref, *out_refs) = refs
    is_sample = pl.program_id(0) >= n_prompt_tiles

    def pick(prompt, sample_ref):
        return jnp.where(is_sample, sample_ref[...], prompt)

    x = pick(x_refs[0][...], x_refs[1]) if split_x else x_refs[0][...]
    gates = pick(mix_ref[:, MIX_GATE:MIX_WIDTH], gates_ref)
    g_a = gates[:, 0:D_MODEL].astype(F32)
    g_b = gates[:, D_MODEL:2 * D_MODEL].astype(F32)
    y_a = pick(mix_ref[:, MIX_YA:MIX_YA + A_WIDTH], yas_ref)
    y_b = pick(mix_ref[:, MIX_YB:MIX_YB + Q_WIDTH], ybs_ref)
    merged = g_a * _dot(y_a, wpa_ref[...]) + g_b * _dot(y_b, wpb_ref[...])
    mix = _dot(merged.astype(BF16), wo_ref[...])
    x1 = _layer_norm(alpha * x + mix, ln_ref[0:1, :], ln_ref[1:2, :])
    x1b = x1.astype(BF16)
    gate = _dot(x1b, wg_ref[...])
    act = (gate * _sigmoid(gate) * _dot(x1b, wu_ref[...])).astype(BF16)
    ffn = _dot(act, wd_ref[...])
    y = _layer_norm(alpha * x1 + ffn, ln_ref[2:3, :], ln_ref[3:4, :])
    if split_out:
        @pl.when(jnp.logical_not(is_sample))
        def _():
            out_refs[0][...] = y

        @pl.when(is_sample)
        def _():
            out_refs[1][...] = y
    else:
        out_refs[0][...] = y


def _merge_ffn(x, mix, sample, w_pa, w_pb, w_o, w_gate, w_up, w_down, ln_pack, layer, alpha, split_out):
    split_x = isinstance(x, tuple)
    n_p, n_s = mix.shape[0] // ROW_TILE, sample[0].shape[0] // ROW_TILE
    all_rows = lambda width: pl.BlockSpec((ROW_TILE, width), lambda i: (i, 0))
    p_rows = lambda width: pl.BlockSpec((ROW_TILE, width), lambda i: (jnp.minimum(i, n_p - 1), 0))
    s_rows = lambda width: pl.BlockSpec((ROW_TILE, width), lambda i: (jnp.maximum(i - n_p, 0), 0),
                                        pipeline_mode=pl.Buffered(1))
    weight = lambda k, n: _resident((None, k, n), lambda i: (layer, 0, 0))
    widths = (A_WIDTH, Q_WIDTH, 2 * D_MODEL)
    x_specs = [p_rows(D_MODEL), s_rows(D_MODEL)] if split_x else [all_rows(D_MODEL)]
    if split_out:
        out_shape = [jax.ShapeDtypeStruct((n_p * ROW_TILE, D_MODEL), F32),
                     jax.ShapeDtypeStruct((n_s * ROW_TILE, D_MODEL), F32)]
        out_specs = [p_rows(D_MODEL), pl.BlockSpec((ROW_TILE, D_MODEL), lambda i: (jnp.maximum(i - n_p, 0), 0))]
    else:
        out_shape = jax.ShapeDtypeStruct(((n_p + n_s) * ROW_TILE, D_MODEL), F32)
        out_specs = all_rows(D_MODEL)
    return pl.pallas_call(
        functools.partial(_merge_ffn_kernel, alpha=alpha, n_prompt_tiles=n_p, split_x=split_x,
                          split_out=split_out),
        out_shape=out_shape,
        grid=(n_p + n_s,),
        in_specs=x_specs + [p_rows(MIX_WIDTH)] + [s_rows(w) for w in widths] + [
            weight(A_WIDTH, D_MODEL), weight(Q_WIDTH, D_MODEL), weight(D_MODEL, D_MODEL),
            weight(D_MODEL, D_FF), weight(D_MODEL, D_FF), weight(D_FF, D_MODEL),
            pl.BlockSpec((None, 4, D_MODEL), lambda i: (layer, 0, 0))],
        out_specs=out_specs,
        compiler_params=_params(1),
        name="merge_ffn",
    )(*(x if split_x else (x,)), mix, *sample, w_pa, w_pb, w_o, w_gate, w_up, w_down, ln_pack)


def kernel(x_prompt, x_sample, cache_swa_k, cache_swa_v, rel_bias, w_in, ln_v_g, ln_v_b, w_s, b_s,
           sinks, w_pa, w_pb, w_o, ln1_g, ln1_b, w_gate, w_up, w_down, ln2_g, ln2_b):
    depth = w_in.shape[0]
    batch, seq, _ = x_prompt.shape
    n_seq, n_new, _ = x_sample.shape
    assert n_new == SUBLANES and seq % ROW_TILE == 0 and n_seq % SAMPLE_SEQS == 0
    assert (n_seq * n_new) % ROW_TILE == 0
    alpha = (2 * depth) ** 0.25

    def heads_rg(w, axis):
        shape = w.shape
        w = w.reshape(shape[:axis] + (N_KV_HEADS, Q_PER_KV, HEAD_DIM) + shape[axis + 1:])
        return jnp.swapaxes(w, axis, axis + 1).reshape(shape)

    w_in_b = w_in.astype(BF16)
    w_q_b = heads_rg(w_in[..., O_Q:O_K], 2).astype(BF16)
    w_pa_b, w_o_b = w_pa.astype(BF16), w_o.astype(BF16)
    w_pb_b = heads_rg(w_pb, 1).astype(BF16)
    w_gate_b, w_up_b, w_down_b = w_gate.astype(BF16), w_up.astype(BF16), w_down.astype(BF16)
    ln_v_g3, ln_v_b3 = ln_v_g[:, None, :], ln_v_b[:, None, :]
    ln_pack = jnp.stack([ln1_g, ln1_b, ln2_g, ln2_b], axis=1)
    b_s_t = jnp.swapaxes(b_s, 1, 2)
    wexp = jnp.tile(w_s[:, :, :n_new, :n_new], (1, 1, 1, SAMPLE_SEQS))
    bsexp = jnp.repeat(jnp.swapaxes(b_s[:, :, :n_new], 1, 2), A_WIDTH // A_GROUPS, axis=-1)

    bias_p = _bias_tables(rel_bias, jnp.stack([_masked_buckets(WINDOW, KEY_PAD, False),
                                               _masked_buckets(WINDOW, KEY_PAD, True)]))
    bias_s = _bias_tables(rel_bias, _masked_buckets(n_new, WINDOW + n_new, False)[None])[0]
    bias_s = bias_s.reshape(N_KV_HEADS, Q_PER_KV, n_new, KEY_PAD)
    bias_s = jnp.swapaxes(bias_s, 0, 1).reshape(N_HEADS * n_new, KEY_PAD)
    bias_c = bias_s[:, :WINDOW]
    bias_n = jnp.tile(bias_s[:, WINDOW:WINDOW + n_new], (1, SAMPLE_SEQS))

    cache_k = jnp.transpose(cache_swa_k, (0, 1, 3, 4, 2)).reshape(depth, n_seq, KV_WIDTH, WINDOW)
    cache_v = jnp.transpose(cache_swa_v, (0, 1, 3, 4, 2)).reshape(depth, n_seq, KV_WIDTH, WINDOW)

    x = (x_prompt.reshape(batch * seq, D_MODEL), x_sample.reshape(n_seq * n_new, D_MODEL))
    n_prompt_tiles, sample_rows = batch * seq // ROW_TILE, n_seq * n_new
    kp_l, vp_l, ga_l = [], [], []
    new_cache = [jnp.zeros(cache_k.shape, F32), jnp.zeros(cache_v.shape, F32)]
    for l in range(depth):
        x_p, x_s, s_tile = (x[0], x[1], 0) if l == 0 else (x, x, n_prompt_tiles)
        mix, kv_tail = _prompt_front(sinks[l], x_p, w_in_b, w_q_b, ln_v_g3, ln_v_b3, w_s, b_s_t, bias_p, l,
                                     batch, seq)
        kp_l.append(kv_tail[..., :KV_WIDTH].reshape(batch, WINDOW, N_KV_HEADS, HEAD_DIM))
        vp_l.append(kv_tail[..., KV_WIDTH:].reshape(batch, WINDOW, N_KV_HEADS, HEAD_DIM))

        u, va32, q, gates_s, kvt = _inproj_sample(x_s, s_tile, sample_rows, w_in_b, w_q_b, ln_v_g3, ln_v_b3, l)
        ya_s, yb_s, *new_cache = _sample_mix(sinks[l], u, va32, q, kvt, cache_k, cache_v, wexp, bsexp,
                                             bias_c, bias_n, new_cache, l)
        ga_l.append(va32.reshape(n_seq, n_new, A_WIDTH))

        x = _merge_ffn(x, mix, (ya_s, yb_s, gates_s), w_pa_b, w_pb_b, w_o_b, w_gate_b, w_up_b,
                       w_down_b, ln_pack, l, alpha, split_out=l == depth - 1)
    xp, xs = x

    def window_major(c):
        c = c.reshape(depth, n_seq, N_KV_HEADS, HEAD_DIM, WINDOW)
        return jnp.transpose(c, (0, 1, 4, 2, 3))

    return (xp.reshape(batch, seq, D_MODEL), xs.reshape(n_seq, n_new, D_MODEL),
            jnp.stack(kp_l), jnp.stack(vp_l), window_major(new_cache[0]), window_major(new_cache[1]),
            jnp.stack(ga_l))
```

```python
import functools
import math

import jax
import jax.numpy as jnp
from jax import lax
from jax.experimental import pallas as pl
from jax.experimental.pallas import tpu as pltpu

D_MODEL = 1024
CHUNK = 128
A_WIDTH = D_MODEL
A_GROUPS = 8
N_HEADS = 16
HEAD_DIM = 64
N_KV_HEADS = 4
Q_PER_KV = N_HEADS // N_KV_HEADS
WINDOW = 128
N_BUCKETS = 32
MAX_DISTANCE = 128
D_FF = 2816
LN_EPS = 1e-5
NEG_INF = -1e30
LOG2_E = math.log2(math.e)

KV_WIDTH = N_KV_HEADS * HEAD_DIM
Q_WIDTH = N_HEADS * HEAD_DIM
O_U = 0
O_V = O_U + A_WIDTH
O_Q = O_V + A_WIDTH
O_K = O_Q + Q_WIDTH
O_G = O_K + 2 * KV_WIDTH
IN_WIDTH = O_G + 2 * D_MODEL

LANES = 128
SUBLANES = 8
BF16_ROWS = 16
ROW_TILE = 512
PIECE = 512
SAMPLE_SEQS = LANES // SUBLANES
KEY_PAD = 2 * WINDOW
MIX_YA, MIX_YB, MIX_GATE = 0, A_WIDTH, A_WIDTH + Q_WIDTH
MIX_WIDTH = MIX_GATE + 2 * D_MODEL
VMEM_LIMIT = 56 * 1024 * 1024

BF16 = jnp.bfloat16
F32 = jnp.float32


def _layer_norm(x, g, b):
    mu = jnp.mean(x, axis=-1, keepdims=True)
    xc = x - mu
    var = jnp.mean(xc * xc, axis=-1, keepdims=True)
    return xc * lax.rsqrt(var + LN_EPS) * g + b


def _gelu(x):
    return jax.nn.gelu(x, approximate=True)


def _sigmoid(x):
    return 0.5 * jnp.tanh(0.5 * x) + 0.5


def _dot(a, b):
    return jnp.dot(a, b, preferred_element_type=F32)


def _dot_nt(a, b):
    return lax.dot_general(a, b, (((1,), (1,)), ((), ())), preferred_element_type=F32)


def _resident(block_shape, index_map):
    return pl.BlockSpec(block_shape, index_map, pipeline_mode=pl.Buffered(1))


def _params(n_axes):
    return pltpu.CompilerParams(dimension_semantics=("arbitrary",) * n_axes,
                                vmem_limit_bytes=VMEM_LIMIT)


def _bias_kernel(rb_ref, bucket_ref, out_ref):
    bk = bucket_ref[...]
    for h in range(N_HEADS):
        acc = jnp.full(bk.shape, NEG_INF, F32)
        for b in range(N_BUCKETS):
            acc = jnp.where(bk == b, rb_ref[b, h], acc)
        out_ref[h] = acc


def _bias_tables(rel_bias, buckets):
    n, t, kp = buckets.shape
    return pl.pallas_call(
        _bias_kernel,
        out_shape=jax.ShapeDtypeStruct((n, N_HEADS, t, kp), F32),
        grid=(n,),
        in_specs=[pl.BlockSpec(memory_space=pltpu.SMEM),
                  pl.BlockSpec((None, t, kp), lambda i: (i, 0, 0))],
        out_specs=pl.BlockSpec((None, N_HEADS, t, kp), lambda i: (i, 0, 0, 0)),
        compiler_params=_params(1),
        name="bias_tables",
    )(rel_bias, buckets)


def _rel_bucket(dist):
    n = jnp.maximum(dist, 0)
    max_exact = N_BUCKETS // 2
    nf = jnp.maximum(n, 1).astype(F32)
    large = max_exact + (jnp.log(nf / max_exact) / math.log(MAX_DISTANCE / max_exact)
                         * (N_BUCKETS - max_exact)).astype(jnp.int32)
    large = jnp.minimum(large, N_BUCKETS - 1)
    return jnp.where(n < max_exact, n, large)


def _masked_buckets(n_q, n_keys, first_block):
    qi = jnp.arange(n_q, dtype=jnp.int32)[:, None]
    kj = jnp.arange(KEY_PAD, dtype=jnp.int32)[None, :]
    dist = qi + WINDOW - kj
    ok = (dist >= 0) & (dist < WINDOW) & (kj < n_keys)
    if first_block:
        ok = ok & (kj >= WINDOW)
    return jnp.where(ok, _rel_bucket(dist), -1)


def _prompt_front_kernel(sinks_ref, x_ref, w_ref, wq_ref, g_ref, b_ref, ws_ref, bs_ref, bias_ref,
                         mix_ref, tail_ref, u_s, va_s, q_s, kv_s, hv_s, bias_s):
    n_chunks = ROW_TILE // CHUNK
    first_tile = pl.program_id(1) == 0

    lane = lax.broadcasted_iota(jnp.int32, (1, LANES), 1)

    @pl.when(jnp.logical_and(pl.program_id(0) == 0, first_tile))
    def _():
        kv_s[0:CHUNK, :] = jnp.zeros((CHUNK, 2 * KV_WIDTH), BF16)
        for table in range(2):
            for h in range(N_HEADS):
                bias_s[table, h, :, 0:LANES] = LOG2_E * jnp.where(lane == 0, sinks_ref[h],
                                                                  bias_ref[table, h, :, 0:LANES])
                bias_s[table, h, :, LANES:] = LOG2_E * bias_ref[table, h, :, LANES:]

    xb = x_ref[...].astype(BF16)
    tri = (lax.broadcasted_iota(jnp.int32, (CHUNK, CHUNK), 0)
           >= lax.broadcasted_iota(jnp.int32, (CHUNK, CHUNK), 1))
    low_half = lane < HEAD_DIM
    zero = jnp.zeros((), BF16)

    def piece_kv(j):
        cols = slice(j * PIECE, (j + 1) * PIECE)
        kv = _dot(xb, w_ref[:, O_K + j * PIECE:O_K + (j + 1) * PIECE])
        kv_s[CHUNK:CHUNK + ROW_TILE, cols] = kv.astype(BF16)
        tail_ref[:, cols] = kv[ROW_TILE - WINDOW:, :]

    def piece_q(j):
        cols = slice(j * PIECE, (j + 1) * PIECE)
        q_s[:, cols] = (_dot(xb, wq_ref[:, cols]) * (LOG2_E * HEAD_DIM ** -0.5)).astype(BF16)

    def piece_v(j):
        cols = slice(j * PIECE, (j + 1) * PIECE)
        hv_s[:, cols] = _gelu(_dot(xb, w_ref[:, O_V + j * PIECE:O_V + (j + 1) * PIECE]))

    def piece_v_norm():
        va_s[...] = _layer_norm(hv_s[...], g_ref[...], b_ref[...]).astype(BF16)

    def piece_u(j):
        cols = slice(j * PIECE, (j + 1) * PIECE)
        u_s[:, cols] = _gelu(_dot(xb, w_ref[:, O_U + j * PIECE:O_U + (j + 1) * PIECE])).astype(BF16)

    def piece_gate(j):
        cols = slice(MIX_GATE + j * PIECE, MIX_GATE + (j + 1) * PIECE)
        mix_ref[:, cols] = _dot(xb, w_ref[:, O_G + j * PIECE:O_G + (j + 1) * PIECE]).astype(BF16)

    def unit_spatial(c, g):
        rows, cols = slice(c * CHUNK, (c + 1) * CHUNK), slice(g * LANES, (g + 1) * LANES)
        w = jnp.where(tri, ws_ref[g], 0.0).astype(BF16)
        mixed = _dot(w, va_s[rows, cols]) + bs_ref[:, g:g + 1]
        mix_ref[rows, MIX_YA + g * LANES:MIX_YA + (g + 1) * LANES] = (u_s[rows, cols].astype(F32) * mixed).astype(BF16)

    def band_operands(c, gp, hi):
        band = slice(c * CHUNK, (c + 2) * CHUNK)
        keep = low_half if hi == 0 else jnp.logical_not(low_half)
        kn = jnp.where(keep, kv_s[band, gp * LANES:(gp + 1) * LANES], zero)
        vn = jnp.where(keep, kv_s[band, KV_WIDTH + gp * LANES:KV_WIDTH + (gp + 1) * LANES], zero)
        not_sink = lax.broadcasted_iota(jnp.int32, (BF16_ROWS, 1), 0) > 0
        kn = jnp.concatenate([jnp.where(not_sink, kn[:BF16_ROWS], zero), kn[BF16_ROWS:]], axis=0)
        vn = jnp.concatenate([jnp.where(not_sink, vn[:BF16_ROWS], zero), vn[BF16_ROWS:]], axis=0)
        return kn, vn

    def unit_attention(c, gp):
        rows = slice(c * CHUNK, (c + 1) * CHUNK)
        table = jnp.where(first_tile, 1, 0) if c == 0 else 0
        pair_cols = [slice((r * 2 + gp) * LANES, (r * 2 + gp + 1) * LANES) for r in range(Q_PER_KV)]
        q4 = jnp.concatenate([q_s[rows, cols] for cols in pair_cols], axis=0)
        out = None
        for hi in range(2):
            kn, vn = band_operands(c, gp, hi)
            h0 = Q_PER_KV * (2 * gp + hi)
            bias = bias_s[table, h0:h0 + Q_PER_KV].reshape(Q_PER_KV * CHUNK, KEY_PAD)
            s = _dot_nt(q4, kn) + bias
            p = jnp.exp2(s - jnp.max(s, axis=-1, keepdims=True))
            o = _dot(p.astype(BF16), vn) * (1.0 / jnp.sum(p, axis=-1, keepdims=True))
            out = o if out is None else jnp.where(low_half, out, o)
        for r, cols in enumerate(pair_cols):
            mix_ref[rows, MIX_YB + cols.start:MIX_YB + cols.stop] = out[r * CHUNK:(r + 1) * CHUNK].astype(BF16)

    for j in range(2 * KV_WIDTH // PIECE):
        piece_kv(j)
    for j in range(Q_WIDTH // PIECE):
        piece_q(j)
    for j in range(A_WIDTH // PIECE):
        piece_v(j)
    piece_v_norm()
    for j in range(A_WIDTH // PIECE):
        piece_u(j)
    for j in range(2 * D_MODEL // PIECE):
        piece_gate(j)
    for c in range(n_chunks):
        for g in range(A_GROUPS):
            unit_spatial(c, g)
        for gp in range(N_KV_HEADS // 2):
            unit_attention(c, gp)

    kv_s[0:CHUNK, :] = kv_s[ROW_TILE:ROW_TILE + CHUNK, :]


def _prompt_front(sinks, x, w_in, w_q, ln_g, ln_b, w_s, b_s_t, bias, layer, batch, seq):
    n_tiles = seq // ROW_TILE
    tok_spec = lambda width: pl.BlockSpec((ROW_TILE, width), lambda b, i: (b * n_tiles + i, 0))
    vec_spec = pl.BlockSpec((None, 1, A_WIDTH), lambda b, i: (layer, 0, 0))
    rows = batch * seq
    return pl.pallas_call(
        _prompt_front_kernel,
        out_shape=[jax.ShapeDtypeStruct((rows, MIX_WIDTH), BF16),
                   jax.ShapeDtypeStruct((batch, WINDOW, 2 * KV_WIDTH), F32)],
        grid=(batch, n_tiles),
        in_specs=[pl.BlockSpec(memory_space=pltpu.SMEM),
                  tok_spec(D_MODEL),
                  _resident((None, D_MODEL, IN_WIDTH), lambda b, i: (layer, 0, 0)),
                  _resident((None, D_MODEL, Q_WIDTH), lambda b, i: (layer, 0, 0)),
                  vec_spec, vec_spec,
                  _resident((None, A_GROUPS, CHUNK, CHUNK), lambda b, i: (layer, 0, 0, 0)),
                  _resident((None, CHUNK, A_GROUPS), lambda b, i: (layer, 0, 0)),
                  _resident((2, N_HEADS, WINDOW, KEY_PAD), lambda b, i: (0, 0, 0, 0))],
        out_specs=[tok_spec(MIX_WIDTH),
                   pl.BlockSpec((None, WINDOW, 2 * KV_WIDTH), lambda b, i: (b, 0, 0))],
        scratch_shapes=[pltpu.VMEM((ROW_TILE, A_WIDTH), BF16),
                        pltpu.VMEM((ROW_TILE, A_WIDTH), BF16),
                        pltpu.VMEM((ROW_TILE, Q_WIDTH), BF16),
                        pltpu.VMEM((CHUNK + ROW_TILE, 2 * KV_WIDTH), BF16),
                        pltpu.VMEM((ROW_TILE, A_WIDTH), F32),
                        pltpu.VMEM((2, N_HEADS, WINDOW, KEY_PAD), F32)],
        compiler_params=_params(2),
        name="prompt_front",
    )(sinks, x, w_in, w_q, ln_g, ln_b, w_s, b_s_t, bias)


def _inproj_sample_kernel(x_ref, w_ref, wq_ref, g_ref, b_ref, u_ref, va_ref, q_ref, gate_ref, kvt_ref):
    xb = x_ref[...].astype(BF16)

    def proj(c0, c1):
        return _dot(xb, w_ref[:, c0:c1])

    half = A_WIDTH // 2
    for c0 in range(O_U, O_V, half):
        u_ref[:, c0:c0 + half] = _gelu(proj(c0, c0 + half)).astype(BF16)
    va_ref[...] = _layer_norm(_gelu(proj(O_V, O_Q)), g_ref[...], b_ref[...])
    for c0 in range(0, Q_WIDTH, half):
        q_ref[:, c0:c0 + half] = (_dot(xb, wq_ref[:, c0:c0 + half]) * (HEAD_DIM ** -0.5)).astype(BF16)
    kvt_ref[...] = proj(O_K, O_G).T
    for c0 in range(0, 2 * D_MODEL, half):
        gate_ref[:, c0:c0 + half] = proj(O_G + c0, O_G + c0 + half).astype(BF16)


def _inproj_sample(x, first_tile, rows, w_in, w_q, ln_g, ln_b, layer):
    row_spec = lambda width: pl.BlockSpec((ROW_TILE, width), lambda i: (i, 0))
    vec_spec = pl.BlockSpec((None, 1, A_WIDTH), lambda i: (layer, 0, 0))
    return pl.pallas_call(
        _inproj_sample_kernel,
        out_shape=[jax.ShapeDtypeStruct((rows, A_WIDTH), BF16),
                   jax.ShapeDtypeStruct((rows, A_WIDTH), F32),
                   jax.ShapeDtypeStruct((rows, Q_WIDTH), BF16),
                   jax.ShapeDtypeStruct((rows, 2 * D_MODEL), BF16),
                   jax.ShapeDtypeStruct((2 * KV_WIDTH, rows), F32)],
        grid=(rows // ROW_TILE,),
        in_specs=[pl.BlockSpec((ROW_TILE, D_MODEL), lambda i: (first_tile + i, 0)),
                  _resident((None, D_MODEL, IN_WIDTH), lambda i: (layer, 0, 0)),
                  _resident((None, D_MODEL, Q_WIDTH), lambda i: (layer, 0, 0)),
                  vec_spec, vec_spec],
        out_specs=[row_spec(A_WIDTH), row_spec(A_WIDTH), row_spec(Q_WIDTH), row_spec(2 * D_MODEL),
                   pl.BlockSpec((2 * KV_WIDTH, ROW_TILE), lambda i: (0, i))],
        compiler_params=_params(1),
        name="inproj_sample",
    )(x, w_in, w_q, ln_g, ln_b)


def _sample_mix_kernel(sinks_ref, u_ref, va_ref, q_ref, kvt_ref, ck_ref, cv_ref, wexp_ref, bsexp_ref,
                       biasc_ref, biasn_ref, prev_k_ref, prev_v_ref, ya_ref, yb_ref, ko_ref, vo_ref, yb_acc):
    del prev_k_ref, prev_v_ref
    n_new = SUBLANES
    rows = SAMPLE_SEQS * n_new

    r_idx = lax.broadcasted_iota(jnp.int32, (rows, rows), 0)
    c_idx = lax.broadcasted_iota(jnp.int32, (rows, rows), 1)
    same_seq_causal = jnp.logical_and(r_idx // n_new == c_idx // n_new, c_idx % n_new <= r_idx % n_new)
    va_b = va_ref[...].astype(BF16)
    per_seq = lambda a: jnp.broadcast_to(a[None], (SAMPLE_SEQS,) + a.shape).reshape(rows, a.shape[-1])
    for g in range(A_GROUPS):
        cols = slice(g * LANES, (g + 1) * LANES)
        w = jnp.where(same_seq_causal, per_seq(wexp_ref[g]), 0.0).astype(BF16)
        mixed = _dot(w, va_b[:, cols]) + per_seq(bsexp_ref[:, cols])
        ya_ref[:, cols] = (u_ref[:, cols].astype(F32) * mixed).astype(BF16)

    group_of_lane = lax.broadcasted_iota(jnp.int32, (1, KV_WIDTH), 1) // HEAD_DIM
    row = lax.broadcasted_iota(jnp.int32, (N_HEADS * n_new, 1), 0)
    head_of_row = Q_PER_KV * ((row // n_new) % N_KV_HEADS) + row // (n_new * N_KV_HEADS)
    sink = jnp.zeros((N_HEADS * n_new, 1), F32)
    for h in range(N_HEADS):
        sink = jnp.where(head_of_row == h, sinks_ref[h], sink)
    lane = lax.broadcasted_iota(jnp.int32, (1, LANES), 1)
    seq_of_lane = lane // n_new
    keep_old = lane < WINDOW - n_new
    q32 = q_ref[...].astype(F32)
    k_new, v_new = kvt_ref[0:KV_WIDTH, :], kvt_ref[KV_WIDTH:2 * KV_WIDTH, :]
    k_new_b, v_new_b = k_new.astype(BF16), v_new.astype(BF16)
    bias_c, bias_n = biasc_ref[...], biasn_ref[...]
    for b in range(SAMPLE_SEQS):
        new = slice(b * n_new, (b + 1) * n_new)
        pieces = []
        for r in range(Q_PER_KV):
            blk = q32[new, r * KV_WIDTH:(r + 1) * KV_WIDTH]
            for g in range(N_KV_HEADS):
                pieces.append(jnp.where(group_of_lane == g, blk, 0.0))
        q_rows = jnp.concatenate(pieces, axis=0).astype(BF16)
        k_old, v_old = ck_ref[b], cv_ref[b]
        k_all = jnp.concatenate([k_old.astype(BF16), k_new_b], axis=1)
        v_all = jnp.concatenate([v_old.astype(BF16), v_new_b], axis=1)
        bias = jnp.concatenate([bias_c, jnp.where(seq_of_lane == b, bias_n, NEG_INF)], axis=1)
        s = _dot(q_rows, k_all) + bias
        m = jnp.maximum(jnp.max(s, axis=-1, keepdims=True), sink)
        p = jnp.exp(s - m)
        denom = jnp.sum(p, axis=-1, keepdims=True) + jnp.exp(sink - m)
        o = _dot_nt(p.astype(BF16), v_all) * (1.0 / denom)
        for r in range(Q_PER_KV):
            acc = jnp.zeros((n_new, KV_WIDTH), F32)
            for g in range(N_KV_HEADS):
                r0 = (r * N_KV_HEADS + g) * n_new
                acc = jnp.where(group_of_lane == g, o[r0:r0 + n_new, :], acc)
            yb_acc[new, r * KV_WIDTH:(r + 1) * KV_WIDTH] = acc
        shift_new = (WINDOW - n_new - b * n_new) % LANES
        ko_ref[b] = jnp.where(keep_old, pltpu.roll(k_old, WINDOW - n_new, 1), pltpu.roll(k_new, shift_new, 1))
        vo_ref[b] = jnp.where(keep_old, pltpu.roll(v_old, WINDOW - n_new, 1), pltpu.roll(v_new, shift_new, 1))
    yb_ref[...] = yb_acc[...].astype(BF16)


def _sample_mix(sinks, u, va32, q, kvt, cache_k, cache_v, wexp, bsexp, bias_c, bias_n, prev, layer):
    n_seq = cache_k.shape[1]
    rows = SAMPLE_SEQS * SUBLANES
    row_spec = lambda width: pl.BlockSpec((rows, width), lambda i: (i, 0))
    cache_spec = pl.BlockSpec((None, SAMPLE_SEQS, KV_WIDTH, WINDOW), lambda i: (layer, i, 0, 0))
    table_spec = pl.BlockSpec((N_HEADS * SUBLANES, LANES), lambda i: (0, 0))
    in_specs = [pl.BlockSpec(memory_space=pltpu.SMEM),
                row_spec(A_WIDTH), row_spec(A_WIDTH), row_spec(Q_WIDTH),
                pl.BlockSpec((2 * KV_WIDTH, rows), lambda i: (0, i)),
                cache_spec, cache_spec,
                pl.BlockSpec((None, A_GROUPS, SUBLANES, rows), lambda i: (layer, 0, 0, 0)),
                pl.BlockSpec((None, SUBLANES, A_WIDTH), lambda i: (layer, 0, 0)),
                table_spec, table_spec,
                pl.BlockSpec(memory_space=pl.ANY), pl.BlockSpec(memory_space=pl.ANY)]
    operands = [sinks, u, va32, q, kvt, cache_k, cache_v, wexp, bsexp, bias_c, bias_n, *prev]
    return pl.pallas_call(
        _sample_mix_kernel,
        out_shape=[jax.ShapeDtypeStruct((n_seq * SUBLANES, A_WIDTH), BF16),
                   jax.ShapeDtypeStruct((n_seq * SUBLANES, Q_WIDTH), BF16),
                   jax.ShapeDtypeStruct(cache_k.shape, F32),
                   jax.ShapeDtypeStruct(cache_v.shape, F32)],
        grid=(n_seq // SAMPLE_SEQS,),
        in_specs=in_specs,
        out_specs=[row_spec(A_WIDTH), row_spec(Q_WIDTH), cache_spec, cache_spec],
        scratch_shapes=[pltpu.VMEM((rows, Q_WIDTH), F32)],
        input_output_aliases={len(operands) - 2: 2, len(operands) - 1: 3},
        compiler_params=_params(1),
        name="sample_mix",
    )(*operands)


def _merge_ffn_kernel(*refs, alpha, n_prompt_tiles, split_x, split_out):
    refs = list(refs)
    x_refs = [refs.pop(0) for _ in range(2 if split_x else 1)]
    (mix_ref, yas_ref, ybs_ref, gates_ref,
     wpa_ref, wpb_ref, wo_ref, wg_ref, wu_ref, wd_ref, ln_ref, *out_refs) = refs
    is_sample = pl.program_id(0) >= n_prompt_tiles

    def pick(prompt, sample_ref):
        return jnp.where(is_sample, sample_ref[...], prompt)

    x = pick(x_refs[0][...], x_refs[1]) if split_x else x_refs[0][...]
    gates = pick(mix_ref[:, MIX_GATE:MIX_WIDTH], gates_ref)
    g_a = _sigmoid(gates[:, 0:D_MODEL].astype(F32))
    g_b = _sigmoid(gates[:, D_MODEL:2 * D_MODEL].astype(F32))
    y_a = pick(mix_ref[:, MIX_YA:MIX_YA + A_WIDTH], yas_ref)
    y_b = pick(mix_ref[:, MIX_YB:MIX_YB + Q_WIDTH], ybs_ref)
    merged = g_a * _dot(y_a, wpa_ref[...]) + g_b * _dot(y_b, wpb_ref[...])
    mix = _dot(merged.astype(BF16), wo_ref[...])
    x1 = _layer_norm(alpha * x + mix, ln_ref[0:1, :], ln_ref[1:2, :])
    x1b = x1.astype(BF16)
    gate = _dot(x1b, wg_ref[...])
    act = (gate * _sigmoid(gate) * _dot(x1b, wu_ref[...])).astype(BF16)
    ffn = _dot(act, wd_ref[...])
    y = _layer_norm(alpha * x1 + ffn, ln_ref[2:3, :], ln_ref[3:4, :])
    if split_out:
        @pl.when(jnp.logical_not(is_sample))
        def _():
            out_refs[0][...] = y

        @pl.when(is_sample)
        def _():
            out_refs[1][...] = y
    else:
        out_refs[0][...] = y


def _merge_ffn(x, mix, sample, w_pa, w_pb, w_o, w_gate, w_up, w_down, ln_pack, layer, alpha, split_out):
    split_x = isinstance(x, tuple)
    n_p, n_s = mix.shape[0] // ROW_TILE, sample[0].shape[0] // ROW_TILE
    all_rows = lambda width: pl.BlockSpec((ROW_TILE, width), lambda i: (i, 0))
    p_rows = lambda width: pl.BlockSpec((ROW_TILE, width), lambda i: (jnp.minimum(i, n_p - 1), 0))
    s_rows = lambda width: pl.BlockSpec((ROW_TILE, width), lambda i: (jnp.maximum(i - n_p, 0), 0),
                                        pipeline_mode=pl.Buffered(1))
    weight = lambda k, n: _resident((None, k, n), lambda i: (layer, 0, 0))
    widths = (A_WIDTH, Q_WIDTH, 2 * D_MODEL)
    x_specs = [p_rows(D_MODEL), s_rows(D_MODEL)] if split_x else [all_rows(D_MODEL)]
    if split_out:
        out_shape = [jax.ShapeDtypeStruct((n_p * ROW_TILE, D_MODEL), F32),
                     jax.ShapeDtypeStruct((n_s * ROW_TILE, D_MODEL), F32)]
        out_specs = [p_rows(D_MODEL), pl.BlockSpec((ROW_TILE, D_MODEL), lambda i: (jnp.maximum(i - n_p, 0), 0))]
    else:
        out_shape = jax.ShapeDtypeStruct(((n_p + n_s) * ROW_TILE, D_MODEL), F32)
        out_specs = all_rows(D_MODEL)
    return pl.pallas_call(
        functools.partial(_merge_ffn_kernel, alpha=alpha, n_prompt_tiles=n_p, split_x=split_x,
                          split_out=split_out),
        out_shape=out_shape,
        grid=(n_p + n_s,),
        in_specs=x_specs + [p_rows(MIX_WIDTH)] + [s_rows(w) for w in widths] + [
            weight(A_WIDTH, D_MODEL), weight(Q_WIDTH, D_MODEL), weight(D_MODEL, D_MODEL),
            weight(D_MODEL, D_FF), weight(D_MODEL, D_FF), weight(D_FF, D_MODEL),
            pl.BlockSpec((None, 4, D_MODEL), lambda i: (layer, 0, 0))],
        out_specs=out_specs,
        compiler_params=_params(1),
        name="merge_ffn",
    )(*(x if split_x else (x,)), mix, *sample, w_pa, w_pb, w_o, w_gate, w_up, w_down, ln_pack)


def kernel(x_prompt, x_sample, cache_swa_k, cache_swa_v, rel_bias, w_in, ln_v_g, ln_v_b, w_s, b_s,
           sinks, w_pa, w_pb, w_o, ln1_g, ln1_b, w_gate, w_up, w_down, ln2_g, ln2_b):
    depth = w_in.shape[0]
    batch, seq, _ = x_prompt.shape
    n_seq, n_new, _ = x_sample.shape
    assert n_new == SUBLANES and seq % ROW_TILE == 0 and n_seq % SAMPLE_SEQS == 0
    assert (n_seq * n_new) % ROW_TILE == 0
    alpha = (2 * depth) ** 0.25

    def heads_rg(w, axis):
        shape = w.shape
        w = w.reshape(shape[:axis] + (N_KV_HEADS, Q_PER_KV, HEAD_DIM) + shape[axis + 1:])
        return jnp.swapaxes(w, axis, axis + 1).reshape(shape)

    w_in_b = w_in.astype(BF16)
    w_q_b = heads_rg(w_in[..., O_Q:O_K], 2).astype(BF16)
    w_pa_b, w_o_b = w_pa.astype(BF16), w_o.astype(BF16)
    w_pb_b = heads_rg(w_pb, 1).astype(BF16)
    w_gate_b, w_up_b, w_down_b = w_gate.astype(BF16), w_up.astype(BF16), w_down.astype(BF16)
    ln_v_g3, ln_v_b3 = ln_v_g[:, None, :], ln_v_b[:, None, :]
    ln_pack = jnp.stack([ln1_g, ln1_b, ln2_g, ln2_b], axis=1)
    b_s_t = jnp.swapaxes(b_s, 1, 2)
    wexp = jnp.tile(w_s[:, :, :n_new, :n_new], (1, 1, 1, SAMPLE_SEQS))
    bsexp = jnp.repeat(jnp.swapaxes(b_s[:, :, :n_new], 1, 2), A_WIDTH // A_GROUPS, axis=-1)

    bias_p = _bias_tables(rel_bias, jnp.stack([_masked_buckets(WINDOW, KEY_PAD, False),
                                               _masked_buckets(WINDOW, KEY_PAD, True)]))
    bias_s = _bias_tables(rel_bias, _masked_buckets(n_new, WINDOW + n_new, False)[None])[0]
    bias_s = bias_s.reshape(N_KV_HEADS, Q_PER_KV, n_new, KEY_PAD)
    bias_s = jnp.swapaxes(bias_s, 0, 1).reshape(N_HEADS * n_new, KEY_PAD)
    bias_c = bias_s[:, :WINDOW]
    bias_n = jnp.tile(bias_s[:, WINDOW:WINDOW + n_new], (1, SAMPLE_SEQS))

    cache_k = jnp.transpose(cache_swa_k, (0, 1, 3, 4, 2)).reshape(depth, n_seq, KV_WIDTH, WINDOW)
    cache_v = jnp.transpose(cache_swa_v, (0, 1, 3, 4, 2)).reshape(depth, n_seq, KV_WIDTH, WINDOW)

    x = (x_prompt.reshape(batch * seq, D_MODEL), x_sample.reshape(n_seq * n_new, D_MODEL))
    n_prompt_tiles, sample_rows = batch * seq // ROW_TILE, n_seq * n_new
    kp_l, vp_l, ga_l = [], [], []
    new_cache = [jnp.zeros(cache_k.shape, F32), jnp.zeros(cache_v.shape, F32)]
    for l in range(depth):
        x_p, x_s, s_tile = (x[0], x[1], 0) if l == 0 else (x, x, n_prompt_tiles)
        mix, kv_tail = _prompt_front(sinks[l], x_p, w_in_b, w_q_b, ln_v_g3, ln_v_b3, w_s, b_s_t, bias_p, l,
                                     batch, seq)
        kp_l.append(kv_tail[..., :KV_WIDTH].reshape(batch, WINDOW, N_KV_HEADS, HEAD_DIM))
        vp_l.append(kv_tail[..., KV_WIDTH:].reshape(batch, WINDOW, N_KV_HEADS, HEAD_DIM))

        u, va32, q, gates_s, kvt = _inproj_sample(x_s, s_tile, sample_rows, w_in_b, w_q_b, ln_v_g3, ln_v_b3, l)
        ya_s, yb_s, *new_cache = _sample_mix(sinks[l], u, va32, q, kvt, cache_k, cache_v, wexp, bsexp,
                                             bias_c, bias_n, new_cache, l)
        ga_l.append(va32.reshape(n_seq, n_new, A_WIDTH))

        x = _merge_ffn(x, mix, (ya_s, yb_s, gates_s), w_pa_b, w_pb_b, w_o_b, w_gate_b, w_up_b,
                       w_down_b, ln_pack, l, alpha, split_out=l == depth - 1)
    xp, xs = x

    def window_major(c):
        c = c.reshape(depth, n_seq, N_KV_HEADS, HEAD_DIM, WINDOW)
        return jnp.transpose(c, (0, 1, 4, 2, 3))

    return (xp.reshape(batch, seq, D_MODEL), xs.reshape(n_seq, n_new, D_MODEL),
            jnp.stack(kp_l), jnp.stack(vp_l), window_major(new_cache[0]), window_major(new_cache[1]),
            jnp.stack(ga_l))
```

```python
import functools
import math

import jax
import jax.numpy as jnp
from jax import lax
from jax.experimental import pallas as pl
from jax.experimental.pallas import tpu as pltpu

D_MODEL = 1024
CHUNK = 128
A_WIDTH = D_MODEL
A_GROUPS = 8
N_HEADS = 16
HEAD_DIM = 64
N_KV_HEADS = 4
Q_PER_KV = N_HEADS // N_KV_HEADS
WINDOW = 128
N_BUCKETS = 32
MAX_DISTANCE = 128
D_FF = 2816
LN_EPS = 1e-5
NEG_INF = -1e30
LOG2_E = math.log2(math.e)

KV_WIDTH = N_KV_HEADS * HEAD_DIM
Q_WIDTH = N_HEADS * HEAD_DIM
O_U = 0
O_V = O_U + A_WIDTH
O_Q = O_V + A_WIDTH
O_K = O_Q + Q_WIDTH
O_G = O_K + 2 * KV_WIDTH
IN_WIDTH = O_G + 2 * D_MODEL

LANES = 128
SUBLANES = 8
BF16_ROWS = 16
ROW_TILE = 512
PIECE = 512
SAMPLE_SEQS = LANES // SUBLANES
KEY_PAD = 2 * WINDOW
MIX_YA, MIX_YB, MIX_GATE = 0, A_WIDTH, A_WIDTH + Q_WIDTH
MIX_WIDTH = MIX_GATE + 2 * D_MODEL
VMEM_LIMIT = 56 * 1024 * 1024

BF16 = jnp.bfloat16
F32 = jnp.float32


def _layer_norm(x, g, b):
    mu = jnp.mean(x, axis=-1, keepdims=True)
    xc = x - mu
    var = jnp.mean(xc * xc, axis=-1, keepdims=True)
    return xc * lax.rsqrt(var + LN_EPS) * g + b


def _gelu(x):
    return jax.nn.gelu(x, approximate=True)


def _sigmoid(x):
    return 0.5 * jnp.tanh(0.5 * x) + 0.5


def _dot(a, b):
    return jnp.dot(a, b, preferred_element_type=F32)


def _dot_nt(a, b):
    return lax.dot_general(a, b, (((1,), (1,)), ((), ())), preferred_element_type=F32)


def _resident(block_shape, index_map):
    return pl.BlockSpec(block_shape, index_map, pipeline_mode=pl.Buffered(1))


def _params(n_axes):
    return pltpu.CompilerParams(dimension_semantics=("arbitrary",) * n_axes,
                                vmem_limit_bytes=VMEM_LIMIT)


def _bias_kernel(rb_ref, bucket_ref, out_ref):
    bk = bucket_ref[...]
    for h in range(N_HEADS):
        acc = jnp.full(bk.shape, NEG_INF, F32)
        for b in range(N_BUCKETS):
            acc = jnp.where(bk == b, rb_ref[b, h], acc)
        out_ref[h] = acc


def _bias_tables(rel_bias, buckets):
    n, t, kp = buckets.shape
    return pl.pallas_call(
        _bias_kernel,
        out_shape=jax.ShapeDtypeStruct((n, N_HEADS, t, kp), F32),
        grid=(n,),
        in_specs=[pl.BlockSpec(memory_space=pltpu.SMEM),
                  pl.BlockSpec((None, t, kp), lambda i: (i, 0, 0))],
        out_specs=pl.BlockSpec((None, N_HEADS, t, kp), lambda i: (i, 0, 0, 0)),
        compiler_params=_params(1),
        name="bias_tables",
    )(rel_bias, buckets)


def _rel_bucket(dist):
    n = jnp.maximum(dist, 0)
    max_exact = N_BUCKETS // 2
    nf = jnp.maximum(n, 1).astype(F32)
    large = max_exact + (jnp.log(nf / max_exact) / math.log(MAX_DISTANCE / max_exact)
                         * (N_BUCKETS - max_exact)).astype(jnp.int32)
    large = jnp.minimum(large, N_BUCKETS - 1)
    return jnp.where(n < max_exact, n, large)


def _masked_buckets(n_q, n_keys, first_block):
    qi = jnp.arange(n_q, dtype=jnp.int32)[:, None]
    kj = jnp.arange(KEY_PAD, dtype=jnp.int32)[None, :]
    dist = qi + WINDOW - kj
    ok = (dist >= 0) & (dist < WINDOW) & (kj < n_keys)
    if first_block:
        ok = ok & (kj >= WINDOW)
    return jnp.where(ok, _rel_bucket(dist), -1)


def _prompt_front_kernel(sinks_ref, x_ref, w_ref, wq_ref, g_ref, b_ref, ws_ref, bs_ref, bias_ref,
                         mix_ref, tail_ref, u_s, va_s, q_s, kv_s, hv_s, bias_s):
    n_chunks = ROW_TILE // CHUNK
    first_tile = pl.program_id(1) == 0

    lane = lax.broadcasted_iota(jnp.int32, (1, LANES), 1)

    @pl.when(jnp.logical_and(pl.program_id(0) == 0, first_tile))
    def _():
        kv_s[0:CHUNK, :] = jnp.zeros((CHUNK, 2 * KV_WIDTH), BF16)
        for table in range(2):
            for h in range(N_HEADS):
                bias_s[table, h, :, 0:LANES] = LOG2_E * jnp.where(lane == 0, sinks_ref[h],
                                                                  bias_ref[table, h, :, 0:LANES])
                bias_s[table, h, :, LANES:] = LOG2_E * bias_ref[table, h, :, LANES:]

    xb = x_ref[...].astype(BF16)
    tri = (lax.broadcasted_iota(jnp.int32, (CHUNK, CHUNK), 0)
           >= lax.broadcasted_iota(jnp.int32, (CHUNK, CHUNK), 1))
    low_half = lane < HEAD_DIM
    zero = jnp.zeros((), BF16)

    def piece_kv(j):
        cols = slice(j * PIECE, (j + 1) * PIECE)
        kv = _dot(xb, w_ref[:, O_K + j * PIECE:O_K + (j + 1) * PIECE])
        kv_s[CHUNK:CHUNK + ROW_TILE, cols] = kv.astype(BF16)
        tail_ref[:, cols] = kv[ROW_TILE - WINDOW:, :]

    def piece_q(j):
        cols = slice(j * PIECE, (j + 1) * PIECE)
        q_s[:, cols] = (_dot(xb, wq_ref[:, cols]) * (LOG2_E * HEAD_DIM ** -0.5)).astype(BF16)

    def piece_v(j):
        cols = slice(j * PIECE, (j + 1) * PIECE)
        hv_s[:, cols] = _gelu(_dot(xb, w_ref[:, O_V + j * PIECE:O_V + (j + 1) * PIECE]))

    def piece_v_norm():
        va_s[...] = _layer_norm(hv_s[...], g_ref[...], b_ref[...]).astype(BF16)

    def piece_u(j):
        cols = slice(j * PIECE, (j + 1) * PIECE)
        u_s[:, cols] = _dot(xb, w_ref[:, O_U + j * PIECE:O_U + (j + 1) * PIECE]).astype(BF16)

    def piece_gate(j):
        cols = slice(MIX_GATE + j * PIECE, MIX_GATE + (j + 1) * PIECE)
        mix_ref[:, cols] = _dot(xb, w_ref[:, O_G + j * PIECE:O_G + (j + 1) * PIECE]).astype(BF16)

    def unit_spatial(c, g):
        rows, cols = slice(c * CHUNK, (c + 1) * CHUNK), slice(g * LANES, (g + 1) * LANES)
        w = jnp.where(tri, ws_ref[g], 0.0).astype(BF16)
        mixed = _dot(w, va_s[rows, cols]) + bs_ref[:, g:g + 1]
        mix_ref[rows, MIX_YA + g * LANES:MIX_YA + (g + 1) * LANES] = (_gelu(u_s[rows, cols].astype(F32)) * mixed).astype(BF16)

    def band_operands(c, gp, hi):
        band = slice(c * CHUNK, (c + 2) * CHUNK)
        keep = low_half if hi == 0 else jnp.logical_not(low_half)
        kn = jnp.where(keep, kv_s[band, gp * LANES:(gp + 1) * LANES], zero)
        vn = jnp.where(keep, kv_s[band, KV_WIDTH + gp * LANES:KV_WIDTH + (gp + 1) * LANES], zero)
        not_sink = lax.broadcasted_iota(jnp.int32, (BF16_ROWS, 1), 0) > 0
        kn = jnp.concatenate([jnp.where(not_sink, kn[:BF16_ROWS], zero), kn[BF16_ROWS:]], axis=0)
        vn = jnp.concatenate([jnp.where(not_sink, vn[:BF16_ROWS], zero), vn[BF16_ROWS:]], axis=0)
        return kn, vn

    def unit_attention(c, gp):
        rows = slice(c * CHUNK, (c + 1) * CHUNK)
        table = jnp.where(first_tile, 1, 0) if c == 0 else 0
        pair_cols = [slice((r * 2 + gp) * LANES, (r * 2 + gp + 1) * LANES) for r in range(Q_PER_KV)]
        q4 = jnp.concatenate([q_s[rows, cols] for cols in pair_cols], axis=0)
        out = None
        for hi in range(2):
            kn, vn = band_operands(c, gp, hi)
            h0 = Q_PER_KV * (2 * gp + hi)
            bias = bias_s[table, h0:h0 + Q_PER_KV].reshape(Q_PER_KV * CHUNK, KEY_PAD)
            s = _dot_nt(q4, kn) + bias
            p = jnp.exp2(s - jnp.max(s, axis=-1, keepdims=True))
            o = _dot(p.astype(BF16), vn) * (1.0 / jnp.sum(p, axis=-1, keepdims=True))
            out = o if out is None else jnp.where(low_half, out, o)
        for r, cols in enumerate(pair_cols):
            mix_ref[rows, MIX_YB + cols.start:MIX_YB + cols.stop] = out[r * CHUNK:(r + 1) * CHUNK].astype(BF16)

    for j in range(2 * KV_WIDTH // PIECE):
        piece_kv(j)
    for j in range(Q_WIDTH // PIECE):
        piece_q(j)
    for j in range(A_WIDTH // PIECE):
        piece_v(j)
    piece_v_norm()
    for j in range(A_WIDTH // PIECE):
        piece_u(j)
    for j in range(2 * D_MODEL // PIECE):
        piece_gate(j)
    for c in range(n_chunks):
        for g in range(A_GROUPS):
            unit_spatial(c, g)
        for gp in range(N_KV_HEADS // 2):
            unit_attention(c, gp)

    kv_s[0:CHUNK, :] = kv_s[ROW_TILE:ROW_TILE + CHUNK, :]


def _prompt_front(sinks, x, w_in, w_q, ln_g, ln_b, w_s, b_s_t, bias, layer, batch, seq):
    n_tiles = seq // ROW_TILE
    tok_spec = lambda width: pl.BlockSpec((ROW_TILE, width), lambda b, i: (b * n_tiles + i, 0))
    vec_spec = pl.BlockSpec((None, 1, A_WIDTH), lambda b, i: (layer, 0, 0))
    rows = batch * seq
    return pl.pallas_call(
        _prompt_front_kernel,
        out_shape=[jax.ShapeDtypeStruct((rows, MIX_WIDTH), BF16),
                   jax.ShapeDtypeStruct((batch, WINDOW, 2 * KV_WIDTH), F32)],
        grid=(batch, n_tiles),
        in_specs=[pl.BlockSpec(memory_space=pltpu.SMEM),
                  tok_spec(D_MODEL),
                  _resident((None, D_MODEL, IN_WIDTH), lambda b, i: (layer, 0, 0)),
                  _resident((None, D_MODEL, Q_WIDTH), lambda b, i: (layer, 0, 0)),
                  vec_spec, vec_spec,
                  _resident((None, A_GROUPS, CHUNK, CHUNK), lambda b, i: (layer, 0, 0, 0)),
                  _resident((None, CHUNK, A_GROUPS), lambda b, i: (layer, 0, 0)),
                  _resident((2, N_HEADS, WINDOW, KEY_PAD), lambda b, i: (0, 0, 0, 0))],
        out_specs=[tok_spec(MIX_WIDTH),
                   pl.BlockSpec((None, WINDOW, 2 * KV_WIDTH), lambda b, i: (b, 0, 0))],
        scratch_shapes=[pltpu.VMEM((ROW_TILE, A_WIDTH), BF16),
                        pltpu.VMEM((ROW_TILE, A_WIDTH), BF16),
                        pltpu.VMEM((ROW_TILE, Q_WIDTH), BF16),
                        pltpu.VMEM((CHUNK + ROW_TILE, 2 * KV_WIDTH), BF16),
                        pltpu.VMEM((ROW_TILE, A_WIDTH), F32),
                        pltpu.VMEM((2, N_HEADS, WINDOW, KEY_PAD), F32)],
        compiler_params=_params(2),
        name="prompt_front",
    )(sinks, x, w_in, w_q, ln_g, ln_b, w_s, b_s_t, bias)


def _inproj_sample_kernel(x_ref, w_ref, wq_ref, g_ref, b_ref, u_ref, va_ref, q_ref, gate_ref, kvt_ref):
    xb = x_ref[...].astype(BF16)

    def proj(c0, c1):
        return _dot(xb, w_ref[:, c0:c1])

    half = A_WIDTH // 2
    for c0 in range(O_U, O_V, half):
        u_ref[:, c0:c0 + half] = _gelu(proj(c0, c0 + half)).astype(BF16)
    va_ref[...] = _layer_norm(_gelu(proj(O_V, O_Q)), g_ref[...], b_ref[...])
    for c0 in range(0, Q_WIDTH, half):
        q_ref[:, c0:c0 + half] = (_dot(xb, wq_ref[:, c0:c0 + half]) * (HEAD_DIM ** -0.5)).astype(BF16)
    kvt_ref[...] = proj(O_K, O_G).T
    for c0 in range(0, 2 * D_MODEL, half):
        gate_ref[:, c0:c0 + half] = proj(O_G + c0, O_G + c0 + half).astype(BF16)


def _inproj_sample(x, first_tile, rows, w_in, w_q, ln_g, ln_b, layer):
    row_spec = lambda width: pl.BlockSpec((ROW_TILE, width), lambda i: (i, 0))
    vec_spec = pl.BlockSpec((None, 1, A_WIDTH), lambda i: (layer, 0, 0))
    return pl.pallas_call(
        _inproj_sample_kernel,
        out_shape=[jax.ShapeDtypeStruct((rows, A_WIDTH), BF16),
                   jax.ShapeDtypeStruct((rows, A_WIDTH), F32),
                   jax.ShapeDtypeStruct((rows, Q_WIDTH), BF16),
                   jax.ShapeDtypeStruct((rows, 2 * D_MODEL), BF16),
                   jax.ShapeDtypeStruct((2 * KV_WIDTH, rows), F32)],
        grid=(rows // ROW_TILE,),
        in_specs=[pl.BlockSpec((ROW_TILE, D_MODEL), lambda i: (first_tile + i, 0)),
                  _resident((None, D_MODEL, IN_WIDTH), lambda i: (layer, 0, 0)),
                  _resident((None, D_MODEL, Q_WIDTH), lambda i: (layer, 0, 0)),
                  vec_spec, vec_spec],
        out_specs=[row_spec(A_WIDTH), row_spec(A_WIDTH), row_spec(Q_WIDTH), row_spec(2 * D_MODEL),
                   pl.BlockSpec((2 * KV_WIDTH, ROW_TILE), lambda i: (0, i))],
        compiler_params=_params(1),
        name="inproj_sample",
    )(x, w_in, w_q, ln_g, ln_b)


def _sample_mix_kernel(sinks_ref, u_ref, va_ref, q_ref, kvt_ref, ck_ref, cv_ref, wexp_ref, bsexp_ref,
                       biasc_ref, biasn_ref, prev_k_ref, prev_v_ref, ya_ref, yb_ref, ko_ref, vo_ref, yb_acc):
    del prev_k_ref, prev_v_ref
    n_new = SUBLANES
    rows = SAMPLE_SEQS * n_new

    r_idx = lax.broadcasted_iota(jnp.int32, (rows, rows), 0)
    c_idx = lax.broadcasted_iota(jnp.int32, (rows, rows), 1)
    same_seq_causal = jnp.logical_and(r_idx // n_new == c_idx // n_new, c_idx % n_new <= r_idx % n_new)
    va_b = va_ref[...].astype(BF16)
    per_seq = lambda a: jnp.broadcast_to(a[None], (SAMPLE_SEQS,) + a.shape).reshape(rows, a.shape[-1])
    for g in range(A_GROUPS):
        cols = slice(g * LANES, (g + 1) * LANES)
        w = jnp.where(same_seq_causal, per_seq(wexp_ref[g]), 0.0).astype(BF16)
        mixed = _dot(w, va_b[:, cols]) + per_seq(bsexp_ref[:, cols])
        ya_ref[:, cols] = (u_ref[:, cols].astype(F32) * mixed).astype(BF16)

    group_of_lane = lax.broadcasted_iota(jnp.int32, (1, KV_WIDTH), 1) // HEAD_DIM
    row = lax.broadcasted_iota(jnp.int32, (N_HEADS * n_new, 1), 0)
    head_of_row = Q_PER_KV * ((row // n_new) % N_KV_HEADS) + row // (n_new * N_KV_HEADS)
    sink = jnp.zeros((N_HEADS * n_new, 1), F32)
    for h in range(N_HEADS):
        sink = jnp.where(head_of_row == h, sinks_ref[h], sink)
    lane = lax.broadcasted_iota(jnp.int32, (1, LANES), 1)
    seq_of_lane = lane // n_new
    keep_old = lane < WINDOW - n_new
    q32 = q_ref[...].astype(F32)
    k_new, v_new = kvt_ref[0:KV_WIDTH, :], kvt_ref[KV_WIDTH:2 * KV_WIDTH, :]
    k_new_b, v_new_b = k_new.astype(BF16), v_new.astype(BF16)
    bias_c, bias_n = biasc_ref[...], biasn_ref[...]
    for b in range(SAMPLE_SEQS):
        new = slice(b * n_new, (b + 1) * n_new)
        pieces = []
        for r in range(Q_PER_KV):
            blk = q32[new, r * KV_WIDTH:(r + 1) * KV_WIDTH]
            for g in range(N_KV_HEADS):
                pieces.append(jnp.where(group_of_lane == g, blk, 0.0))
        q_rows = jnp.concatenate(pieces, axis=0).astype(BF16)
        k_old, v_old = ck_ref[b], cv_ref[b]
        k_all = jnp.concatenate([k_old.astype(BF16), k_new_b], axis=1)
        v_all = jnp.concatenate([v_old.astype(BF16), v_new_b], axis=1)
        bias = jnp.concatenate([bias_c, jnp.where(seq_of_lane == b, bias_n, NEG_INF)], axis=1)
        s = _dot(q_rows, k_all) + bias
        m = jnp.maximum(jnp.max(s, axis=-1, keepdims=True), sink)
        p = jnp.exp(s - m)
        denom = jnp.sum(p, axis=-1, keepdims=True) + jnp.exp(sink - m)
        o = _dot_nt(p.astype(BF16), v_all) * (1.0 / denom)
        for r in range(Q_PER_KV):
            acc = jnp.zeros((n_new, KV_WIDTH), F32)
            for g in range(N_KV_HEADS):
                r0 = (r * N_KV_HEADS + g) * n_new
                acc = jnp.where(group_of_lane == g, o[r0:r0 + n_new, :], acc)
            yb_acc[new, r * KV_WIDTH:(r + 1) * KV_WIDTH] = acc
        shift_new = (WINDOW - n_new - b * n_new) % LANES
        ko_ref[b] = jnp.where(keep_old, pltpu.roll(k_old, WINDOW - n_new, 1), pltpu.roll(k_new, shift_new, 1))
        vo_ref[b] = jnp.where(keep_old, pltpu.roll(v_old, WINDOW - n_new, 1), pltpu.roll(v_new, shift_new, 1))
    yb_ref[...] = yb_acc[...].astype(BF16)


def _sample_mix(sinks, u, va32, q, kvt, cache_k, cache_v, wexp, bsexp, bias_c, bias_n, prev, layer):
    n_seq = cache_k.shape[1]
    rows = SAMPLE_SEQS * SUBLANES
    row_spec = lambda width: pl.BlockSpec((rows, width), lambda i: (i, 0))
    cache_spec = pl.BlockSpec((None, SAMPLE_SEQS, KV_WIDTH, WINDOW), lambda i: (layer, i, 0, 0))
    table_spec = pl.BlockSpec((N_HEADS * SUBLANES, LANES), lambda i: (0, 0))
    in_specs = [pl.BlockSpec(memory_space=pltpu.SMEM),
                row_spec(A_WIDTH), row_spec(A_WIDTH), row_spec(Q_WIDTH),
                pl.BlockSpec((2 * KV_WIDTH, rows), lambda i: (0, i)),
                cache_spec, cache_spec,
                pl.BlockSpec((None, A_GROUPS, SUBLANES, rows), lambda i: (layer, 0, 0, 0)),
                pl.BlockSpec((None, SUBLANES, A_WIDTH), lambda i: (layer, 0, 0)),
                table_spec, table_spec,
                pl.BlockSpec(memory_space=pl.ANY), pl.BlockSpec(memory_space=pl.ANY)]
    operands = [sinks, u, va32, q, kvt, cache_k, cache_v, wexp, bsexp, bias_c, bias_n, *prev]
    return pl.pallas_call(
        _sample_mix_kernel,
        out_shape=[jax.ShapeDtypeStruct((n_seq * SUBLANES, A_WIDTH), BF16),
                   jax.ShapeDtypeStruct((n_seq * SUBLANES, Q_WIDTH), BF16),
                   jax.ShapeDtypeStruct(cache_k.shape, F32),
                   jax.ShapeDtypeStruct(cache_v.shape, F32)],
        grid=(n_seq // SAMPLE_SEQS,),
        in_specs=in_specs,
        out_specs=[row_spec(A_WIDTH), row_spec(Q_WIDTH), cache_spec, cache_spec],
        scratch_shapes=[pltpu.VMEM((rows, Q_WIDTH), F32)],
        input_output_aliases={len(operands) - 2: 2, len(operands) - 1: 3},
        compiler_params=_params(1),
        name="sample_mix",
    )(*operands)


def _merge_ffn_kernel(*refs, alpha, n_prompt_tiles, split_x, split_out):
    refs = list(refs)
    x_refs = [refs.pop(0) for _ in range(2 if split_x else 1)]
    (mix_ref, yas_ref, ybs_ref, gates_ref,
     wpa_ref, wpb_ref, wo_ref, wg_ref, wu_ref, wd_ref, ln_ref, *out_refs) = refs
    is_sample = pl.program_id(0) >= n_prompt_tiles

    def pick(prompt, sample_ref):
        return jnp.where(is_sample, sample_ref[...], prompt)

    x = pick(x_refs[0][...], x_refs[1]) if split_x else x_refs[0][...]
    gates = pick(mix_ref[:, MIX_GATE:MIX_WIDTH], gates_ref)
    g_a = _sigmoid(gates[:, 0:D_MODEL].astype(F32))
    g_b = _sigmoid(gates[:, D_MODEL:2 * D_MODEL].astype(F32))
    y_a = pick(mix_ref[:, MIX_YA:MIX_YA + A_WIDTH], yas_ref)
    y_b = pick(mix_ref[:, MIX_YB:MIX_YB + Q_WIDTH], ybs_ref)
    merged = g_a * _dot(y_a, wpa_ref[...]) + g_b * _dot(y_b, wpb_ref[...])
    mix = _dot(merged.astype(BF16), wo_ref[...])
    x1 = _layer_norm(alpha * x + mix, ln_ref[0:1, :], ln_ref[1:2, :])
    x1b = x1.astype(BF16)
    gate = _dot(x1b, wg_ref[...])
    act = (gate * _sigmoid(gate) * _dot(x1b, wu_ref[...])).astype(BF16)
    ffn = _dot(act, wd_ref[...])
    y = _layer_norm(alpha * x1 + ffn, ln_ref[2:3, :], ln_ref[3:4, :])
    if split_out:
        @pl.when(jnp.logical_not(is_sample))
        def _():
            out_refs[0][...] = y

        @pl.when(is_sample)
        def _():
            out_refs[1][...] = y
    else:
        out_refs[0][...] = y


def _merge_ffn(x, mix, sample, w_pa, w_pb, w_o, w_gate, w_up, w_down, ln_pack, layer, alpha, split_out):
    split_x = isinstance(x, tuple)
    n_p, n_s = mix.shape[0] // ROW_TILE, sample[0].shape[0] // ROW_TILE
    all_rows = lambda width: pl.BlockSpec((ROW_TILE, width), lambda i: (i, 0))
    p_rows = lambda width: pl.BlockSpec((ROW_TILE, width), lambda i: (jnp.minimum(i, n_p - 1), 0))
    s_rows = lambda width: pl.BlockSpec((ROW_TILE, width), lambda i: (jnp.maximum(i - n_p, 0), 0),
                                        pipeline_mode=pl.Buffered(1))
    weight = lambda k, n: _resident((None, k, n), lambda i: (layer, 0, 0))
    widths = (A_WIDTH, Q_WIDTH, 2 * D_MODEL)
    x_specs = [p_rows(D_MODEL), s_rows(D_MODEL)] if split_x else [all_rows(D_MODEL)]
    if split_out:
        out_shape = [jax.ShapeDtypeStruct((n_p * ROW_TILE, D_MODEL), F32),
                     jax.ShapeDtypeStruct((n_s * ROW_TILE, D_MODEL), F32)]
        out_specs = [p_rows(D_MODEL), pl.BlockSpec((ROW_TILE, D_MODEL), lambda i: (jnp.maximum(i - n_p, 0), 0))]
    else:
        out_shape = jax.ShapeDtypeStruct(((n_p + n_s) * ROW_TILE, D_MODEL), F32)
        out_specs = all_rows(D_MODEL)
    return pl.pallas_call(
        functools.partial(_merge_ffn_kernel, alpha=alpha, n_prompt_tiles=n_p, split_x=split_x,
                          split_out=split_out),
        out_shape=out_shape,
        grid=(n_p + n_s,),
        in_specs=x_specs + [p_rows(MIX_WIDTH)] + [s_rows(w) for w in widths] + [
            weight(A_WIDTH, D_MODEL), weight(Q_WIDTH, D_MODEL), weight(D_MODEL, D_MODEL),
            weight(D_MODEL, D_FF), weight(D_MODEL, D_FF), weight(D_FF, D_MODEL),
            pl.BlockSpec((None, 4, D_MODEL), lambda i: (layer, 0, 0))],
        out_specs=out_specs,
        compiler_params=_params(1),
        name="merge_ffn",
    )(*(x if split_x else (x,)), mix, *sample, w_pa, w_pb, w_o, w_gate, w_up, w_down, ln_pack)


def kernel(x_prompt, x_sample, cache_swa_k, cache_swa_v, rel_bias, w_in, ln_v_g, ln_v_b, w_s, b_s,
           sinks, w_pa, w_pb, w_o, ln1_g, ln1_b, w_gate, w_up, w_down, ln2_g, ln2_b):
    depth = w_in.shape[0]
    batch, seq, _ = x_prompt.shape
    n_seq, n_new, _ = x_sample.shape
    assert n_new == SUBLANES and seq % ROW_TILE == 0 and n_seq % SAMPLE_SEQS == 0
    assert (n_seq * n_new) % ROW_TILE == 0
    alpha = (2 * depth) ** 0.25

    def heads_rg(w, axis):
        shape = w.shape
        w = w.reshape(shape[:axis] + (N_KV_HEADS, Q_PER_KV, HEAD_DIM) + shape[axis + 1:])
        return jnp.swapaxes(w, axis, axis + 1).reshape(shape)

    w_in_b = w_in.astype(BF16)
    w_q_b = heads_rg(w_in[..., O_Q:O_K], 2).astype(BF16)
    w_pa_b, w_o_b = w_pa.astype(BF16), w_o.astype(BF16)
    w_pb_b = heads_rg(w_pb, 1).astype(BF16)
    w_gate_b, w_up_b, w_down_b = w_gate.astype(BF16), w_up.astype(BF16), w_down.astype(BF16)
    ln_v_g3, ln_v_b3 = ln_v_g[:, None, :], ln_v_b[:, None, :]
    ln_pack = jnp.stack([ln1_g, ln1_b, ln2_g, ln2_b], axis=1)
    b_s_t = jnp.swapaxes(b_s, 1, 2)
    wexp = jnp.tile(w_s[:, :, :n_new, :n_new], (1, 1, 1, SAMPLE_SEQS))
    bsexp = jnp.repeat(jnp.swapaxes(b_s[:, :, :n_new], 1, 2), A_WIDTH // A_GROUPS, axis=-1)

    bias_p = _bias_tables(rel_bias, jnp.stack([_masked_buckets(WINDOW, KEY_PAD, False),
                                               _masked_buckets(WINDOW, KEY_PAD, True)]))
    bias_s = _bias_tables(rel_bias, _masked_buckets(n_new, WINDOW + n_new, False)[None])[0]
    bias_s = bias_s.reshape(N_KV_HEADS, Q_PER_KV, n_new, KEY_PAD)
    bias_s = jnp.swapaxes(bias_s, 0, 1).reshape(N_HEADS * n_new, KEY_PAD)
    bias_c = bias_s[:, :WINDOW]
    bias_n = jnp.tile(bias_s[:, WINDOW:WINDOW + n_new], (1, SAMPLE_SEQS))

    cache_k = jnp.transpose(cache_swa_k, (0, 1, 3, 4, 2)).reshape(depth, n_seq, KV_WIDTH, WINDOW)
    cache_v = jnp.transpose(cache_swa_v, (0, 1, 3, 4, 2)).reshape(depth, n_seq, KV_WIDTH, WINDOW)

    x = (x_prompt.reshape(batch * seq, D_MODEL), x_sample.reshape(n_seq * n_new, D_MODEL))
    n_prompt_tiles, sample_rows = batch * seq // ROW_TILE, n_seq * n_new
    kp_l, vp_l, ga_l = [], [], []
    new_cache = [jnp.zeros(cache_k.shape, F32), jnp.zeros(cache_v.shape, F32)]
    for l in range(depth):
        x_p, x_s, s_tile = (x[0], x[1], 0) if l == 0 else (x, x, n_prompt_tiles)
        mix, kv_tail = _prompt_front(sinks[l], x_p, w_in_b, w_q_b, ln_v_g3, ln_v_b3, w_s, b_s_t, bias_p, l,
                                     batch, seq)
        kp_l.append(kv_tail[..., :KV_WIDTH].reshape(batch, WINDOW, N_KV_HEADS, HEAD_DIM))
        vp_l.append(kv_tail[..., KV_WIDTH:].reshape(batch, WINDOW, N_KV_HEADS, HEAD_DIM))

        u, va32, q, gates_s, kvt = _inproj_sample(x_s, s_tile, sample_rows, w_in_b, w_q_b, ln_v_g3, ln_v_b3, l)
        ya_s, yb_s, *new_cache = _sample_mix(sinks[l], u, va32, q, kvt, cache_k, cache_v, wexp, bsexp,
                                             bias_c, bias_n, new_cache, l)
        ga_l.append(va32.reshape(n_seq, n_new, A_WIDTH))

        x = _merge_ffn(x, mix, (ya_s, yb_s, gates_s), w_pa_b, w_pb_b, w_o_b, w_gate_b, w_up_b,
                       w_down_b, ln_pack, l, alpha, split_out=l == depth - 1)
    xp, xs = x

    def window_major(c):
        c = c.reshape(depth, n_seq, N_KV_HEADS, HEAD_DIM, WINDOW)
        return jnp.transpose(c, (0, 1, 4, 2, 3))

    return (xp.reshape(batch, seq, D_MODEL), xs.reshape(n_seq, n_new, D_MODEL),
            jnp.stack(kp_l), jnp.stack(vp_l), window_major(new_cache[0]), window_major(new_cache[1]),
            jnp.stack(ga_l))
```

```python
import functools
import math

import jax
import jax.numpy as jnp
from jax import lax
from jax.experimental import pallas as pl
from jax.experimental.pallas import tpu as pltpu

D_MODEL = 1024
CHUNK = 128
A_WIDTH = D_MODEL
A_GROUPS = 8
N_HEADS = 16
HEAD_DIM = 64
N_KV_HEADS = 4
Q_PER_KV = N_HEADS // N_KV_HEADS
WINDOW = 128
N_BUCKETS = 32
MAX_DISTANCE = 128
D_FF = 2816
LN_EPS = 1e-5
NEG_INF = -1e30
LOG2_E = math.log2(math.e)

KV_WIDTH = N_KV_HEADS * HEAD_DIM
Q_WIDTH = N_HEADS * HEAD_DIM
O_U = 0
O_V = O_U + A_WIDTH
O_Q = O_V + A_WIDTH
O_K = O_Q + Q_WIDTH
O_G = O_K + 2 * KV_WIDTH
IN_WIDTH = O_G + 2 * D_MODEL

LANES = 128
SUBLANES = 8
BF16_ROWS = 16
ROW_TILE = 512
PIECE = 512
SAMPLE_SEQS = LANES // SUBLANES
KEY_PAD = 2 * WINDOW
MIX_YA, MIX_YB, MIX_GATE = 0, A_WIDTH, A_WIDTH + Q_WIDTH
MIX_WIDTH = MIX_GATE + 2 * D_MODEL
VMEM_LIMIT = 56 * 1024 * 1024

BF16 = jnp.bfloat16
F32 = jnp.float32


def _layer_norm(x, g, b):
    mu = jnp.mean(x, axis=-1, keepdims=True)
    xc = x - mu
    var = jnp.mean(xc * xc, axis=-1, keepdims=True)
    return xc * lax.rsqrt(var + LN_EPS) * g + b


def _gelu(x):
    return jax.nn.gelu(x, approximate=True)


def _sigmoid(x):
    return 0.5 * jnp.tanh(0.5 * x) + 0.5


def _dot(a, b):
    return jnp.dot(a, b, preferred_element_type=F32)


def _dot_nt(a, b):
    return lax.dot_general(a, b, (((1,), (1,)), ((), ())), preferred_element_type=F32)


def _resident(block_shape, index_map):
    return pl.BlockSpec(block_shape, index_map, pipeline_mode=pl.Buffered(1))


def _params(n_axes):
    return pltpu.CompilerParams(dimension_semantics=("arbitrary",) * n_axes,
                                vmem_limit_bytes=VMEM_LIMIT)


def _bias_kernel(rb_ref, bucket_ref, out_ref):
    bk = bucket_ref[...]
    for h in range(N_HEADS):
        acc = jnp.full(bk.shape, NEG_INF, F32)
        for b in range(N_BUCKETS):
            acc = jnp.where(bk == b, rb_ref[b, h], acc)
        out_ref[h] = acc


def _bias_tables(rel_bias, buckets):
    n, t, kp = buckets.shape
    return pl.pallas_call(
        _bias_kernel,
        out_shape=jax.ShapeDtypeStruct((n, N_HEADS, t, kp), F32),
        grid=(n,),
        in_specs=[pl.BlockSpec(memory_space=pltpu.SMEM),
                  pl.BlockSpec((None, t, kp), lambda i: (i, 0, 0))],
        out_specs=pl.BlockSpec((None, N_HEADS, t, kp), lambda i: (i, 0, 0, 0)),
        compiler_params=_params(1),
        name="bias_tables",
    )(rel_bias, buckets)


def _rel_bucket(dist):
    n = jnp.maximum(dist, 0)
    max_exact = N_BUCKETS // 2
    nf = jnp.maximum(n, 1).astype(F32)
    large = max_exact + (jnp.log(nf / max_exact) / math.log(MAX_DISTANCE / max_exact)
                         * (N_BUCKETS - max_exact)).astype(jnp.int32)
    large = jnp.minimum(large, N_BUCKETS - 1)
    return jnp.where(n < max_exact, n, large)


def _masked_buckets(n_q, n_keys, first_block):
    qi = jnp.arange(n_q, dtype=jnp.int32)[:, None]
    kj = jnp.arange(KEY_PAD, dtype=jnp.int32)[None, :]
    dist = qi + WINDOW - kj
    ok = (dist >= 0) & (dist < WINDOW) & (kj < n_keys)
    if first_block:
        ok = ok & (kj >= WINDOW)
    return jnp.where(ok, _rel_bucket(dist), -1)


def _prompt_front_kernel(sinks_ref, x_ref, w_ref, wq_ref, g_ref, b_ref, ws_ref, bs_ref, bias_ref,
                         mix_ref, tail_ref, u_s, va_s, q_s, kv_s, hv_s, bias_s):
    n_chunks = ROW_TILE // CHUNK
    first_tile = pl.program_id(1) == 0

    lane = lax.broadcasted_iota(jnp.int32, (1, LANES), 1)

    @pl.when(jnp.logical_and(pl.program_id(0) == 0, first_tile))
    def _():
        kv_s[0:CHUNK, :] = jnp.zeros((CHUNK, 2 * KV_WIDTH), BF16)
        for table in range(2):
            for h in range(N_HEADS):
                bias_s[table, h, :, 0:LANES] = LOG2_E * jnp.where(lane == 0, sinks_ref[h],
                                                                  bias_ref[table, h, :, 0:LANES])
                bias_s[table, h, :, LANES:] = LOG2_E * bias_ref[table, h, :, LANES:]

    xb = x_ref[...].astype(BF16)
    tri = (lax.broadcasted_iota(jnp.int32, (CHUNK, CHUNK), 0)
           >= lax.broadcasted_iota(jnp.int32, (CHUNK, CHUNK), 1))
    low_half = lane < HEAD_DIM
    zero = jnp.zeros((), BF16)

    def piece_kv(j):
        cols = slice(j * PIECE, (j + 1) * PIECE)
        kv = _dot(xb, w_ref[:, O_K + j * PIECE:O_K + (j + 1) * PIECE])
        kv_s[CHUNK:CHUNK + ROW_TILE, cols] = kv.astype(BF16)
        tail_ref[:, cols] = kv[ROW_TILE - WINDOW:, :]

    def piece_q(j):
        cols = slice(j * PIECE, (j + 1) * PIECE)
        q_s[:, cols] = (_dot(xb, wq_ref[:, cols]) * (LOG2_E * HEAD_DIM ** -0.5)).astype(BF16)

    def piece_v(j):
        cols = slice(j * PIECE, (j + 1) * PIECE)
        hv_s[:, cols] = _dot(xb, w_ref[:, O_V + j * PIECE:O_V + (j + 1) * PIECE])

    def unit_v_norm(c):
        rows = slice(c * CHUNK, (c + 1) * CHUNK)
        va_s[rows, :] = _layer_norm(_gelu(hv_s[rows, :]), g_ref[...], b_ref[...]).astype(BF16)

    def piece_u(j):
        cols = slice(j * PIECE, (j + 1) * PIECE)
        u_s[:, cols] = _dot(xb, w_ref[:, O_U + j * PIECE:O_U + (j + 1) * PIECE]).astype(BF16)

    def piece_gate(j):
        cols = slice(MIX_GATE + j * PIECE, MIX_GATE + (j + 1) * PIECE)
        mix_ref[:, cols] = _dot(xb, w_ref[:, O_G + j * PIECE:O_G + (j + 1) * PIECE]).astype(BF16)

    def unit_spatial(c, g):
        rows, cols = slice(c * CHUNK, (c + 1) * CHUNK), slice(g * LANES, (g + 1) * LANES)
        w = jnp.where(tri, ws_ref[g], 0.0).astype(BF16)
        mixed = _dot(w, va_s[rows, cols]) + bs_ref[:, g:g + 1]
        mix_ref[rows, MIX_YA + g * LANES:MIX_YA + (g + 1) * LANES] = (_gelu(u_s[rows, cols].astype(F32)) * mixed).astype(BF16)

    def band_operands(c, gp, hi):
        band = slice(c * CHUNK, (c + 2) * CHUNK)
        keep = low_half if hi == 0 else jnp.logical_not(low_half)
        kn = jnp.where(keep, kv_s[band, gp * LANES:(gp + 1) * LANES], zero)
        vn = jnp.where(keep, kv_s[band, KV_WIDTH + gp * LANES:KV_WIDTH + (gp + 1) * LANES], zero)
        not_sink = lax.broadcasted_iota(jnp.int32, (BF16_ROWS, 1), 0) > 0
        kn = jnp.concatenate([jnp.where(not_sink, kn[:BF16_ROWS], zero), kn[BF16_ROWS:]], axis=0)
        vn = jnp.concatenate([jnp.where(not_sink, vn[:BF16_ROWS], zero), vn[BF16_ROWS:]], axis=0)
        return kn, vn

    def unit_attention(c, gp):
        rows = slice(c * CHUNK, (c + 1) * CHUNK)
        table = jnp.where(first_tile, 1, 0) if c == 0 else 0
        pair_cols = [slice((r * 2 + gp) * LANES, (r * 2 + gp + 1) * LANES) for r in range(Q_PER_KV)]
        q4 = jnp.concatenate([q_s[rows, cols] for cols in pair_cols], axis=0)
        out = None
        for hi in range(2):
            kn, vn = band_operands(c, gp, hi)
            h0 = Q_PER_KV * (2 * gp + hi)
            bias = bias_s[table, h0:h0 + Q_PER_KV].reshape(Q_PER_KV * CHUNK, KEY_PAD)
            s = _dot_nt(q4, kn) + bias
            p = jnp.exp2(s - jnp.max(s, axis=-1, keepdims=True))
            o = _dot(p.astype(BF16), vn) * (1.0 / jnp.sum(p, axis=-1, keepdims=True))
            out = o if out is None else jnp.where(low_half, out, o)
        for r, cols in enumerate(pair_cols):
            mix_ref[rows, MIX_YB + cols.start:MIX_YB + cols.stop] = out[r * CHUNK:(r + 1) * CHUNK].astype(BF16)

    for j in range(2 * KV_WIDTH // PIECE):
        piece_kv(j)
    for j in range(Q_WIDTH // PIECE):
        piece_q(j)
    for j in range(A_WIDTH // PIECE):
        piece_v(j)
    for j in range(A_WIDTH // PIECE):
        piece_u(j)
    for j in range(2 * D_MODEL // PIECE):
        piece_gate(j)
    for c in range(n_chunks):
        unit_v_norm(c)
        for g in range(A_GROUPS):
            unit_spatial(c, g)
        for gp in range(N_KV_HEADS // 2):
            unit_attention(c, gp)

    kv_s[0:CHUNK, :] = kv_s[ROW_TILE:ROW_TILE + CHUNK, :]


def _prompt_front(sinks, x, w_in, w_q, ln_g, ln_b, w_s, b_s_t, bias, layer, batch, seq):
    n_tiles = seq // ROW_TILE
    tok_spec = lambda width: pl.BlockSpec((ROW_TILE, width), lambda b, i: (b * n_tiles + i, 0))
    vec_spec = pl.BlockSpec((None, 1, A_WIDTH), lambda b, i: (layer, 0, 0))
    rows = batch * seq
    return pl.pallas_call(
        _prompt_front_kernel,
        out_shape=[jax.ShapeDtypeStruct((rows, MIX_WIDTH), BF16),
                   jax.ShapeDtypeStruct((batch, WINDOW, 2 * KV_WIDTH), F32)],
        grid=(batch, n_tiles),
        in_specs=[pl.BlockSpec(memory_space=pltpu.SMEM),
                  tok_spec(D_MODEL),
                  _resident((None, D_MODEL, IN_WIDTH), lambda b, i: (layer, 0, 0)),
                  _resident((None, D_MODEL, Q_WIDTH), lambda b, i: (layer, 0, 0)),
                  vec_spec, vec_spec,
                  _resident((None, A_GROUPS, CHUNK, CHUNK), lambda b, i: (layer, 0, 0, 0)),
                  _resident((None, CHUNK, A_GROUPS), lambda b, i: (layer, 0, 0)),
                  _resident((2, N_HEADS, WINDOW, KEY_PAD), lambda b, i: (0, 0, 0, 0))],
        out_specs=[tok_spec(MIX_WIDTH),
                   pl.BlockSpec((None, WINDOW, 2 * KV_WIDTH), lambda b, i: (b, 0, 0))],
        scratch_shapes=[pltpu.VMEM((ROW_TILE, A_WIDTH), BF16),
                        pltpu.VMEM((ROW_TILE, A_WIDTH), BF16),
                        pltpu.VMEM((ROW_TILE, Q_WIDTH), BF16),
                        pltpu.VMEM((CHUNK + ROW_TILE, 2 * KV_WIDTH), BF16),
                        pltpu.VMEM((ROW_TILE, A_WIDTH), F32),
                        pltpu.VMEM((2, N_HEADS, WINDOW, KEY_PAD), F32)],
        compiler_params=_params(2),
        name="prompt_front",
    )(sinks, x, w_in, w_q, ln_g, ln_b, w_s, b_s_t, bias)


def _inproj_sample_kernel(x_ref, w_ref, wq_ref, g_ref, b_ref, u_ref, va_ref, q_ref, gate_ref, kvt_ref):
    xb = x_ref[...].astype(BF16)

    def proj(c0, c1):
        return _dot(xb, w_ref[:, c0:c1])

    half = A_WIDTH // 2
    for c0 in range(O_U, O_V, half):
        u_ref[:, c0:c0 + half] = _gelu(proj(c0, c0 + half)).astype(BF16)
    va_ref[...] = _layer_norm(_gelu(proj(O_V, O_Q)), g_ref[...], b_ref[...])
    for c0 in range(0, Q_WIDTH, half):
        q_ref[:, c0:c0 + half] = (_dot(xb, wq_ref[:, c0:c0 + half]) * (HEAD_DIM ** -0.5)).astype(BF16)
    kvt_ref[...] = proj(O_K, O_G).T
    for c0 in range(0, 2 * D_MODEL, half):
        gate_ref[:, c0:c0 + half] = proj(O_G + c0, O_G + c0 + half).astype(BF16)


def _inproj_sample(x, first_tile, rows, w_in, w_q, ln_g, ln_b, layer):
    row_spec = lambda width: pl.BlockSpec((ROW_TILE, width), lambda i: (i, 0))
    vec_spec = pl.BlockSpec((None, 1, A_WIDTH), lambda i: (layer, 0, 0))
    return pl.pallas_call(
        _inproj_sample_kernel,
        out_shape=[jax.ShapeDtypeStruct((rows, A_WIDTH), BF16),
                   jax.ShapeDtypeStruct((rows, A_WIDTH), F32),
                   jax.ShapeDtypeStruct((rows, Q_WIDTH), BF16),
                   jax.ShapeDtypeStruct((rows, 2 * D_MODEL), BF16),
                   jax.ShapeDtypeStruct((2 * KV_WIDTH, rows), F32)],
        grid=(rows // ROW_TILE,),
        in_specs=[pl.BlockSpec((ROW_TILE, D_MODEL), lambda i: (first_tile + i, 0)),
                  _resident((None, D_MODEL, IN_WIDTH), lambda i: (layer, 0, 0)),
                  _resident((None, D_MODEL, Q_WIDTH), lambda i: (layer, 0, 0)),
                  vec_spec, vec_spec],
        out_specs=[row_spec(A_WIDTH), row_spec(A_WIDTH), row_spec(Q_WIDTH), row_spec(2 * D_MODEL),
                   pl.BlockSpec((2 * KV_WIDTH, ROW_TILE), lambda i: (0, i))],
        compiler_params=_params(1),
        name="inproj_sample",
    )(x, w_in, w_q, ln_g, ln_b)


def _sample_mix_kernel(sinks_ref, u_ref, va_ref, q_ref, kvt_ref, ck_ref, cv_ref, wexp_ref, bsexp_ref,
                       biasc_ref, biasn_ref, prev_k_ref, prev_v_ref, ya_ref, yb_ref, ko_ref, vo_ref, yb_acc):
    del prev_k_ref, prev_v_ref
    n_new = SUBLANES
    rows = SAMPLE_SEQS * n_new

    r_idx = lax.broadcasted_iota(jnp.int32, (rows, rows), 0)
    c_idx = lax.broadcasted_iota(jnp.int32, (rows, rows), 1)
    same_seq_causal = jnp.logical_and(r_idx // n_new == c_idx // n_new, c_idx % n_new <= r_idx % n_new)
    va_b = va_ref[...].astype(BF16)
    per_seq = lambda a: jnp.broadcast_to(a[None], (SAMPLE_SEQS,) + a.shape).reshape(rows, a.shape[-1])
    for g in range(A_GROUPS):
        cols = slice(g * LANES, (g + 1) * LANES)
        w = jnp.where(same_seq_causal, per_seq(wexp_ref[g]), 0.0).astype(BF16)
        mixed = _dot(w, va_b[:, cols]) + per_seq(bsexp_ref[:, cols])
        ya_ref[:, cols] = (u_ref[:, cols].astype(F32) * mixed).astype(BF16)

    group_of_lane = lax.broadcasted_iota(jnp.int32, (1, KV_WIDTH), 1) // HEAD_DIM
    row = lax.broadcasted_iota(jnp.int32, (N_HEADS * n_new, 1), 0)
    head_of_row = Q_PER_KV * ((row // n_new) % N_KV_HEADS) + row // (n_new * N_KV_HEADS)
    sink = jnp.zeros((N_HEADS * n_new, 1), F32)
    for h in range(N_HEADS):
        sink = jnp.where(head_of_row == h, sinks_ref[h], sink)
    lane = lax.broadcasted_iota(jnp.int32, (1, LANES), 1)
    seq_of_lane = lane // n_new
    keep_old = lane < WINDOW - n_new
    q32 = q_ref[...].astype(F32)
    k_new, v_new = kvt_ref[0:KV_WIDTH, :], kvt_ref[KV_WIDTH:2 * KV_WIDTH, :]
    k_new_b, v_new_b = k_new.astype(BF16), v_new.astype(BF16)
    bias_c, bias_n = biasc_ref[...], biasn_ref[...]
    for b in range(SAMPLE_SEQS):
        new = slice(b * n_new, (b + 1) * n_new)
        pieces = []
        for r in range(Q_PER_KV):
            blk = q32[new, r * KV_WIDTH:(r + 1) * KV_WIDTH]
            for g in range(N_KV_HEADS):
                pieces.append(jnp.where(group_of_lane == g, blk, 0.0))
        q_rows = jnp.concatenate(pieces, axis=0).astype(BF16)
        k_old, v_old = ck_ref[b], cv_ref[b]
        k_all = jnp.concatenate([k_old.astype(BF16), k_new_b], axis=1)
        v_all = jnp.concatenate([v_old.astype(BF16), v_new_b], axis=1)
        bias = jnp.concatenate([bias_c, jnp.where(seq_of_lane == b, bias_n, NEG_INF)], axis=1)
        s = _dot(q_rows, k_all) + bias
        m = jnp.maximum(jnp.max(s, axis=-1, keepdims=True), sink)
        p = jnp.exp(s - m)
        denom = jnp.sum(p, axis=-1, keepdims=True) + jnp.exp(sink - m)
        o = _dot_nt(p.astype(BF16), v_all) * (1.0 / denom)
        for r in range(Q_PER_KV):
            acc = jnp.zeros((n_new, KV_WIDTH), F32)
            for g in range(N_KV_HEADS):
                r0 = (r * N_KV_HEADS + g) * n_new
                acc = jnp.where(group_of_lane == g, o[r0:r0 + n_new, :], acc)
            yb_acc[new, r * KV_WIDTH:(r + 1) * KV_WIDTH] = acc
        shift_new = (WINDOW - n_new - b * n_new) % LANES
        ko_ref[b] = jnp.where(keep_old, pltpu.roll(k_old, WINDOW - n_new, 1), pltpu.roll(k_new, shift_new, 1))
        vo_ref[b] = jnp.where(keep_old, pltpu.roll(v_old, WINDOW - n_new, 1), pltpu.roll(v_new, shift_new, 1))
    yb_ref[...] = yb_acc[...].astype(BF16)


def _sample_mix(sinks, u, va32, q, kvt, cache_k, cache_v, wexp, bsexp, bias_c, bias_n, prev, layer):
    n_seq = cache_k.shape[1]
    rows = SAMPLE_SEQS * SUBLANES
    row_spec = lambda width: pl.BlockSpec((rows, width), lambda i: (i, 0))
    cache_spec = pl.BlockSpec((None, SAMPLE_SEQS, KV_WIDTH, WINDOW), lambda i: (layer, i, 0, 0))
    table_spec = pl.BlockSpec((N_HEADS * SUBLANES, LANES), lambda i: (0, 0))
    in_specs = [pl.BlockSpec(memory_space=pltpu.SMEM),
                row_spec(A_WIDTH), row_spec(A_WIDTH), row_spec(Q_WIDTH),
                pl.BlockSpec((2 * KV_WIDTH, rows), lambda i: (0, i)),
                cache_spec, cache_spec,
                pl.BlockSpec((None, A_GROUPS, SUBLANES, rows), lambda i: (layer, 0, 0, 0)),
                pl.BlockSpec((None, SUBLANES, A_WIDTH), lambda i: (layer, 0, 0)),
                table_spec, table_spec,
                pl.BlockSpec(memory_space=pl.ANY), pl.BlockSpec(memory_space=pl.ANY)]
    operands = [sinks, u, va32, q, kvt, cache_k, cache_v, wexp, bsexp, bias_c, bias_n, *prev]
    return pl.pallas_call(
        _sample_mix_kernel,
        out_shape=[jax.ShapeDtypeStruct((n_seq * SUBLANES, A_WIDTH), BF16),
                   jax.ShapeDtypeStruct((n_seq * SUBLANES, Q_WIDTH), BF16),
                   jax.ShapeDtypeStruct(cache_k.shape, F32),
                   jax.ShapeDtypeStruct(cache_v.shape, F32)],
        grid=(n_seq // SAMPLE_SEQS,),
        in_specs=in_specs,
        out_specs=[row_spec(A_WIDTH), row_spec(Q_WIDTH), cache_spec, cache_spec],
        scratch_shapes=[pltpu.VMEM((rows, Q_WIDTH), F32)],
        input_output_aliases={len(operands) - 2: 2, len(operands) - 1: 3},
        compiler_params=_params(1),
        name="sample_mix",
    )(*operands)


def _merge_ffn_kernel(*refs, alpha, n_prompt_tiles, split_x, split_out):
    refs = list(refs)
    x_refs = [refs.pop(0) for _ in range(2 if split_x else 1)]
    (mix_ref, yas_ref, ybs_ref, gates_ref,
     wpa_ref, wpb_ref, wo_ref, wg_ref, wu_ref, wd_ref, ln_ref, *out_refs) = refs
    is_sample = pl.program_id(0) >= n_prompt_tiles

    def pick(prompt, sample_ref):
        return jnp.where(is_sample, sample_ref[...], prompt)

    x = pick(x_refs[0][...], x_refs[1]) if split_x else x_refs[0][...]
    gates = pick(mix_ref[:, MIX_GATE:MIX_WIDTH], gates_ref)
    g_a = _sigmoid(gates[:, 0:D_MODEL].astype(F32))
    g_b = _sigmoid(gates[:, D_MODEL:2 * D_MODEL].astype(F32))
    y_a = pick(mix_ref[:, MIX_YA:MIX_YA + A_WIDTH], yas_ref)
    y_b = pick(mix_ref[:, MIX_YB:MIX_YB + Q_WIDTH], ybs_ref)
    merged = g_a * _dot(y_a, wpa_ref[...]) + g_b * _dot(y_b, wpb_ref[...])
    mix = _dot(merged.astype(BF16), wo_ref[...])
    x1 = _layer_norm(alpha * x + mix, ln_ref[0:1, :], ln_ref[1:2, :])
    x1b = x1.astype(BF16)
    gate = _dot(x1b, wg_ref[...])
    act = (gate * _sigmoid(gate) * _dot(x1b, wu_ref[...])).astype(BF16)
    ffn = _dot(act, wd_ref[...])
    y = _layer_norm(alpha * x1 + ffn, ln_ref[2:3, :], ln_ref[3:4, :])
    if split_out:
        @pl.when(jnp.logical_not(is_sample))
        def _():
            out_refs[0][...] = y

        @pl.when(is_sample)
        def _():
            out_refs[1][...] = y
    else:
        out_refs[0][...] = y


def _merge_ffn(x, mix, sample, w_pa, w_pb, w_o, w_gate, w_up, w_down, ln_pack, layer, alpha, split_out):
    split_x = isinstance(x, tuple)
    n_p, n_s = mix.shape[0] // ROW_TILE, sample[0].shape[0] // ROW_TILE
    all_rows = lambda width: pl.BlockSpec((ROW_TILE, width), lambda i: (i, 0))
    p_rows = lambda width: pl.BlockSpec((ROW_TILE, width), lambda i: (jnp.minimum(i, n_p - 1), 0))
    s_rows = lambda width: pl.BlockSpec((ROW_TILE, width), lambda i: (jnp.maximum(i - n_p, 0), 0),
                                        pipeline_mode=pl.Buffered(1))
    weight = lambda k, n: _resident((None, k, n), lambda i: (layer, 0, 0))
    widths = (A_WIDTH, Q_WIDTH, 2 * D_MODEL)
    x_specs = [p_rows(D_MODEL), s_rows(D_MODEL)] if split_x else [all_rows(D_MODEL)]
    if split_out:
        out_shape = [jax.ShapeDtypeStruct((n_p * ROW_TILE, D_MODEL), F32),
                     jax.ShapeDtypeStruct((n_s * ROW_TILE, D_MODEL), F32)]
        out_specs = [p_rows(D_MODEL), pl.BlockSpec((ROW_TILE, D_MODEL), lambda i: (jnp.maximum(i - n_p, 0), 0))]
    else:
        out_shape = jax.ShapeDtypeStruct(((n_p + n_s) * ROW_TILE, D_MODEL), F32)
        out_specs = all_rows(D_MODEL)
    return pl.pallas_call(
        functools.partial(_merge_ffn_kernel, alpha=alpha, n_prompt_tiles=n_p, split_x=split_x,
                          split_out=split_out),
        out_shape=out_shape,
        grid=(n_p + n_s,),
        in_specs=x_specs + [p_rows(MIX_WIDTH)] + [s_rows(w) for w in widths] + [
            weight(A_WIDTH, D_MODEL), weight(Q_WIDTH, D_MODEL), weight(D_MODEL, D_MODEL),
            weight(D_MODEL, D_FF), weight(D_MODEL, D_FF), weight(D_FF, D_MODEL),
            pl.BlockSpec((None, 4, D_MODEL), lambda i: (layer, 0, 0))],
        out_specs=out_specs,
        compiler_params=_params(1),
        name="merge_ffn",
    )(*(x if split_x else (x,)), mix, *sample, w_pa, w_pb, w_o, w_gate, w_up, w_down, ln_pack)


def kernel(x_prompt, x_sample, cache_swa_k, cache_swa_v, rel_bias, w_in, ln_v_g, ln_v_b, w_s, b_s,
           sinks, w_pa, w_pb, w_o, ln1_g, ln1_b, w_gate, w_up, w_down, ln2_g, ln2_b):
    depth = w_in.shape[0]
    batch, seq, _ = x_prompt.shape
    n_seq, n_new, _ = x_sample.shape
    assert n_new == SUBLANES and seq % ROW_TILE == 0 and n_seq % SAMPLE_SEQS == 0
    assert (n_seq * n_new) % ROW_TILE == 0
    alpha = (2 * depth) ** 0.25

    def heads_rg(w, axis):
        shape = w.shape
        w = w.reshape(shape[:axis] + (N_KV_HEADS, Q_PER_KV, HEAD_DIM) + shape[axis + 1:])
        return jnp.swapaxes(w, axis, axis + 1).reshape(shape)

    w_in_b = w_in.astype(BF16)
    w_q_b = heads_rg(w_in[..., O_Q:O_K], 2).astype(BF16)
    w_pa_b, w_o_b = w_pa.astype(BF16), w_o.astype(BF16)
    w_pb_b = heads_rg(w_pb, 1).astype(BF16)
    w_gate_b, w_up_b, w_down_b = w_gate.astype(BF16), w_up.astype(BF16), w_down.astype(BF16)
    ln_v_g3, ln_v_b3 = ln_v_g[:, None, :], ln_v_b[:, None, :]
    ln_pack = jnp.stack([ln1_g, ln1_b, ln2_g, ln2_b], axis=1)
    b_s_t = jnp.swapaxes(b_s, 1, 2)
    wexp = jnp.tile(w_s[:, :, :n_new, :n_new], (1, 1, 1, SAMPLE_SEQS))
    bsexp = jnp.repeat(jnp.swapaxes(b_s[:, :, :n_new], 1, 2), A_WIDTH // A_GROUPS, axis=-1)

    bias_p = _bias_tables(rel_bias, jnp.stack([_masked_buckets(WINDOW, KEY_PAD, False),
                                               _masked_buckets(WINDOW, KEY_PAD, True)]))
    bias_s = _bias_tables(rel_bias, _masked_buckets(n_new, WINDOW + n_new, False)[None])[0]
    bias_s = bias_s.reshape(N_KV_HEADS, Q_PER_KV, n_new, KEY_PAD)
    bias_s = jnp.swapaxes(bias_s, 0, 1).reshape(N_HEADS * n_new, KEY_PAD)
    bias_c = bias_s[:, :WINDOW]
    bias_n = jnp.tile(bias_s[:, WINDOW:WINDOW + n_new], (1, SAMPLE_SEQS))

    cache_k = jnp.transpose(cache_swa_k, (0, 1, 3, 4, 2)).reshape(depth, n_seq, KV_WIDTH, WINDOW)
    cache_v = jnp.transpose(cache_swa_v, (0, 1, 3, 4, 2)).reshape(depth, n_seq, KV_WIDTH, WINDOW)

    x = (x_prompt.reshape(batch * seq, D_MODEL), x_sample.reshape(n_seq * n_new, D_MODEL))
    n_prompt_tiles, sample_rows = batch * seq // ROW_TILE, n_seq * n_new
    kp_l, vp_l, ga_l = [], [], []
    new_cache = [jnp.zeros(cache_k.shape, F32), jnp.zeros(cache_v.shape, F32)]
    for l in range(depth):
        x_p, x_s, s_tile = (x[0], x[1], 0) if l == 0 else (x, x, n_prompt_tiles)
        mix, kv_tail = _prompt_front(sinks[l], x_p, w_in_b, w_q_b, ln_v_g3, ln_v_b3, w_s, b_s_t, bias_p, l,
                                     batch, seq)
        kp_l.append(kv_tail[..., :KV_WIDTH].reshape(batch, WINDOW, N_KV_HEADS, HEAD_DIM))
        vp_l.append(kv_tail[..., KV_WIDTH:].reshape(batch, WINDOW, N_KV_HEADS, HEAD_DIM))

        u, va32, q, gates_s, kvt = _inproj_sample(x_s, s_tile, sample_rows, w_in_b, w_q_b, ln_v_g3, ln_v_b3, l)
        ya_s, yb_s, *new_cache = _sample_mix(sinks[l], u, va32, q, kvt, cache_k, cache_v, wexp, bsexp,
                                             bias_c, bias_n, new_cache, l)
        ga_l.append(va32.reshape(n_seq, n_new, A_WIDTH))

        x = _merge_ffn(x, mix, (ya_s, yb_s, gates_s), w_pa_b, w_pb_b, w_o_b, w_gate_b, w_up_b,
                       w_down_b, ln_pack, l, alpha, split_out=l == depth - 1)
    xp, xs = x

    def window_major(c):
        c = c.reshape(depth, n_seq, N_KV_HEADS, HEAD_DIM, WINDOW)
        return jnp.transpose(c, (0, 1, 4, 2, 3))

    return (xp.reshape(batch, seq, D_MODEL), xs.reshape(n_seq, n_new, D_MODEL),
            jnp.stack(kp_l), jnp.stack(vp_l), window_major(new_cache[0]), window_major(new_cache[1]),
            jnp.stack(ga_l))
```
